```python
import math
import jax, jax.numpy as jnp
from jax import lax
import numpy as np

D_MODEL = 1024
BATCH = 32
SEQ = 256
DEPTH = 2
DEC_BATCH = 8
DEC_SEQ = 1024
PAST_LEN = 512

GRID_W = 64
D_GROUP = 512
N_MIXERS = 3
D_MIX = N_MIXERS * D_GROUP
LRU_BLOCKS = 8
LRU_BLOCK = D_GROUP // LRU_BLOCKS
CONV_W = 4
CONV_PAD = (2, 1)
LRU_C = 8.0
HG_HEADS = 4
HG_DK = D_GROUP // HG_HEADS
HG_DV = D_GROUP // HG_HEADS
HG_CHUNK = 16
ML_HEADS = 4
ML_DH = D_GROUP // ML_HEADS
ML_CHUNK = 64
N_EXPERTS = 16
N_GROUPS = 4
EXPERTS_PER_GROUP = N_EXPERTS // N_GROUPS
TOP_K = 2
D_FF = 512
ALPHA = (2.0 * DEPTH) ** 0.25
BETA = (8.0 * DEPTH) ** -0.25
EPS = 1e-5
NEG = -1e30
TINY = 1e-30
IN_COLS = (D_GROUP,) * 11 + (4 * ML_HEADS,)
D_IN = sum(IN_COLS)
IN_SPLITS = tuple(sum(IN_COLS[:i + 1]) for i in range(len(IN_COLS) - 1))
ML_GATE_OFF = D_IN - 4 * ML_HEADS

kernel_name = 'hybrid_rglru_hgrn2_mlstm_moe_diffusion_step'


def _layer_norm(x, g, b):
    xf = x.astype(jnp.float32)
    mu = jnp.mean(xf, -1, keepdims=True)
    var = jnp.mean(jnp.square(xf - mu), -1, keepdims=True)
    return ((xf - mu) * lax.rsqrt(var + EPS) * g + b).astype(x.dtype)


def _head_rms(x, g):
    y = x * lax.rsqrt(jnp.mean(jnp.square(x), -1, keepdims=True) + EPS)
    return y.reshape(x.shape[0], x.shape[1], -1) * g


def _head_ln(x, g):
    mu = jnp.mean(x, -1, keepdims=True)
    var = jnp.mean(jnp.square(x - mu), -1, keepdims=True)
    y = (x - mu) * lax.rsqrt(var + EPS)
    return y.reshape(x.shape[0], x.shape[1], -1) * g


def _grid_pos_embed(n_tokens):
    rows = n_tokens // GRID_W
    r = jnp.repeat(jnp.arange(rows, dtype=jnp.float32), GRID_W)
    c = jnp.tile(jnp.arange(GRID_W, dtype=jnp.float32), rows)
    q = D_MODEL // 4
    freq = jnp.exp(-math.log(10000.0) * jnp.arange(q, dtype=jnp.float32) / q)
    ar = r[:, None] * freq
    ac = c[:, None] * freq
    return jnp.concatenate([jnp.sin(ar), jnp.cos(ar), jnp.sin(ac), jnp.cos(ac)], axis=-1)


def _dwconv(x, w, b):
    y = lax.conv_general_dilated(x, w.astype(x.dtype)[:, None, :], (1,), [CONV_PAD],
                                 dimension_numbers=('NWC', 'WIO', 'NWC'),
                                 feature_group_count=x.shape[-1])
    return y + b.astype(x.dtype)


def _rglru_dir(x, w_a, b_a, w_x, b_x, lam, h0):
    B, T, _ = x.shape
    xb = x.reshape(B, T, LRU_BLOCKS, LRU_BLOCK)
    r = jax.nn.sigmoid(jnp.einsum('btnd,nde->btne', xb, w_a).reshape(B, T, D_GROUP) + b_a)
    i = jax.nn.sigmoid(jnp.einsum('btnd,nde->btne', xb, w_x).reshape(B, T, D_GROUP) + b_x)
    log_a = -LRU_C * r * jax.nn.softplus(-lam)
    a = jnp.exp(log_a)
    u = jnp.sqrt(jnp.maximum(-jnp.expm1(2.0 * log_a), 0.0)) * (i * x)

    def combine(p, q):
        return p[0] * q[0], q[0] * p[1] + q[1]

    a_cum, h = lax.associative_scan(combine, (a, u), axis=1)
    h = h + a_cum * h0[:, None, :]
    return h, h[:, -1]


def _hgrn2_dir(q, k, v, log_f, S0):
    B, T, H, _ = q.shape
    n = T // HG_CHUNK
    rs = lambda z: z.reshape(B, n, HG_CHUNK, H, z.shape[-1])
    q, k, v, log_f = rs(q), rs(k), rs(v), rs(log_f)
    b = jnp.cumsum(log_f, axis=2)
    causal = jnp.tril(jnp.ones((HG_CHUNK, HG_CHUNK), bool))[None, None, :, :, None, None]
    diff = b[:, :, :, None] - b[:, :, None, :]
    decay = jnp.exp(jnp.where(causal, diff, NEG))
    A = jnp.einsum('bnthd,bnshd,bntshd->bnhts', q, k, decay)
    o_intra = jnp.einsum('bnhts,bnshe->bnthe', A, v)
    b_last = b[:, :, -1]
    k_dec = k * jnp.exp(b_last[:, :, None] - b)
    dS = jnp.einsum('bnshd,bnshe->bnhde', k_dec, v)

    def step(S, inp):
        dec, ds = inp
        return dec[..., None] * S + ds, S

    S_final, S_start = lax.scan(step, S0, (jnp.moveaxis(jnp.exp(b_last), 1, 0), jnp.moveaxis(dS, 1, 0)))
    S_start = jnp.moveaxis(S_start, 0, 1)
    o_inter = jnp.einsum('bnthd,bnhde->bnthe', q * jnp.exp(b), S_start)
    return (o_intra + o_inter).reshape(B, T, H, -1), S_final


def _mlstm_dir(q, k, v, log_i, log_f, C0, n0, m0):
    B, T, H, DH = q.shape
    n = T // ML_CHUNK
    L = ML_CHUNK
    rs = lambda z: z.reshape((B, n, L) + z.shape[2:])
    q, k, v, log_i, log_f = rs(q), rs(k), rs(v), rs(log_i), rs(log_f)
    b = jnp.cumsum(log_f, axis=2)
    b_last = b[:, :, -1]
    w_state = b_last[:, :, None] - b + log_i
    m_loc = jnp.max(w_state, axis=2)
    e_state = jnp.exp(w_state - m_loc[:, :, None])
    dC = jnp.einsum('bnsh,bnshd,bnshe->bnhde', e_state, k, v)
    dn = jnp.einsum('bnsh,bnshd->bnhd', e_state, k)

    def step(carry, inp):
        C, nn, m = carry
        bl, ml, dc, dnn = inp
        m_new = jnp.maximum(bl + m, ml)
        s_old = jnp.exp(bl + m - m_new)
        s_new = jnp.exp(ml - m_new)
        C_new = s_old[..., None, None] * C + s_new[..., None, None] * dc
        n_new = s_old[..., None] * nn + s_new[..., None] * dnn
        return (C_new, n_new, m_new), (C, nn, m)

    mv = lambda z: jnp.moveaxis(z, 1, 0)
    (C_f, n_f, m_f), (C_s, n_s, m_s) = lax.scan(step, (C0, n0, m0), (mv(b_last), mv(m_loc), mv(dC), mv(dn)))
    C_s, n_s, m_s = jnp.moveaxis(C_s, 0, 1), jnp.moveaxis(n_s, 0, 1), jnp.moveaxis(m_s, 0, 1)
    causal = jnp.tril(jnp.ones((L, L), bool))[None, None, :, :, None]
    D = jnp.where(causal, b[:, :, :, None, :] - b[:, :, None, :, :] + log_i[:, :, None, :, :], NEG)
    inter = b + m_s[:, :, None]
    m_t = jnp.maximum(inter, jnp.max(D, axis=3))
    P = jnp.exp(D - m_t[:, :, :, None])
    s_inter = jnp.exp(inter - m_t)
    W = P * jnp.einsum('bnthd,bnshd->bntsh', q, k)
    num = jnp.einsum('bntsh,bnshe->bnthe', W, v) + s_inter[..., None] * jnp.einsum('bnthd,bnhde->bnthe', q, C_s)
    den = jnp.sum(W, axis=3) + s_inter * jnp.einsum('bnthd,bnhd->bnth', q, n_s)
    h = num / jnp.maximum(jnp.abs(den), jnp.exp(-m_t))[..., None]
    return h.reshape(B, T, H, DH), (C_f, n_f, m_f)


def _mixer(h, init, w_in, b_in, conv_w, conv_b, lru_wa, lru_ba, lru_wx, lru_bx, lru_lam,
           hg_lower, hg_norm_g, ml_norm_g, w_out):
    B, T, _ = h.shape
    f32 = jnp.float32
    z = (jnp.einsum('btd,de->bte', h, w_in) + b_in).astype(f32)
    (r_x, r_g, g_q, g_ff, g_fb, g_i, g_o, m_q, m_k, m_v, m_o, m_g) = jnp.split(z, IN_SPLITS, axis=-1)
    h0, S0, C0, n0, m0 = [s.astype(f32) for s in init]
    flip = lambda a: jnp.flip(a, axis=1)
    heads = lambda a, H: a.reshape(B, T, H, -1)
    xc = _dwconv(r_x, conv_w, conv_b)
    o_f, hT_f = _rglru_dir(xc, lru_wa[0], lru_ba[0], lru_wx[0], lru_bx[0], lru_lam[0], h0[:, 0])
    o_b, hT_b = _rglru_dir(flip(xc), lru_wa[1], lru_ba[1], lru_wx[1], lru_bx[1], lru_lam[1], h0[:, 1])
    y_r = jax.nn.gelu(r_g) * (o_f + flip(o_b))
    q_h = heads(jax.nn.silu(g_q), HG_HEADS)
    v_h = heads(g_i, HG_HEADS)

    def forget(pre, lb):
        sig = jax.nn.sigmoid(pre)
        log_f = jnp.log(jnp.maximum(lb + (1.0 - lb) * sig, TINY))
        kk = (1.0 - lb) * (1.0 - sig)
        return heads(log_f, HG_HEADS), heads(kk, HG_HEADS)

    lf_f, k_f = forget(g_ff, hg_lower[0])
    lf_b, k_b = forget(g_fb, hg_lower[1])
    s_f, ST_f = _hgrn2_dir(q_h, k_f, v_h, lf_f, S0[:, 0])
    s_b, ST_b = _hgrn2_dir(flip(q_h), flip(k_b), flip(v_h), flip(lf_b), S0[:, 1])
    y_h = _head_rms(s_f + flip(s_b), hg_norm_g) * jax.nn.silu(g_o)
    mq = heads(m_q, ML_HEADS)
    mk = heads(m_k, ML_HEADS) * ML_DH ** -0.5
    mvv = heads(m_v, ML_HEADS)
    mg = m_g.reshape(B, T, 2, 2, ML_HEADS)
    log_i = mg[:, :, :, 0]
    log_fm = jax.nn.log_sigmoid(mg[:, :, :, 1])
    c_f, (CT_f, nT_f, mT_f) = _mlstm_dir(mq, mk, mvv, log_i[:, :, 0], log_fm[:, :, 0], C0[:, 0], n0[:, 0], m0[:, 0])
    c_b, (CT_b, nT_b, mT_b) = _mlstm_dir(flip(mq), flip(mk), flip(mvv), flip(log_i[:, :, 1]), flip(log_fm[:, :, 1]),
                                         C0[:, 1], n0[:, 1], m0[:, 1])
    y_m = jax.nn.sigmoid(m_o) * _head_ln(c_f + flip(c_b), ml_norm_g)
    y = jnp.concatenate([y_r, y_h, y_m], axis=-1).astype(h.dtype)
    out = jnp.einsum('bte,ed->btd', y, w_out)
    finals = (jnp.stack([hT_f, hT_b], 1), jnp.stack([ST_f, ST_b], 1), jnp.stack([CT_f, CT_b], 1),
              jnp.stack([nT_f, nT_b], 1), jnp.stack([mT_f, mT_b], 1))
    return out, finals


def _moe(h, w_router, b_router, w_gate, w_up, w_down):
    f32 = jnp.float32
    scores = jax.nn.softmax(jnp.einsum('btd,de->bte', h, w_router).astype(f32), axis=-1)
    sel = (scores + b_router.astype(f32)).reshape(scores.shape[:-1] + (N_GROUPS, EXPERTS_PER_GROUP))
    grp_score = jnp.sum(lax.top_k(sel, TOP_K)[0], axis=-1)
    best = jnp.argmax(grp_score, axis=-1)[..., None] == jnp.arange(N_GROUPS)
    masked = jnp.where(best[..., None], sel, NEG).reshape(scores.shape)
    _, idx = lax.top_k(masked, TOP_K)
    w = jnp.take_along_axis(scores, idx, axis=-1)
    w = w / jnp.sum(w, -1, keepdims=True)
    combine = jnp.sum((idx[..., None] == jnp.arange(N_EXPERTS)).astype(f32) * w[..., None], axis=-2)
    hg = jnp.einsum('btd,edf->btef', h, w_gate)
    hu = jnp.einsum('btd,edf->btef', h, w_up)
    act = jax.nn.silu(hg) * hu * combine[..., None].astype(h.dtype)
    return jnp.einsum('btef,efd->btd', act, w_down)


def _layer(x, mod, init, w_in, b_in, conv_w, conv_b, lru_wa, lru_ba, lru_wx, lru_bx, lru_lam,
           hg_lower, hg_norm_g, ml_norm_g, w_out, ln1_g, ln1_b, ln2_g, ln2_b,
           w_router, b_router, w_gate, w_up, w_down):
    sh1, sc1, g1, sh2, sc2, g2 = jnp.split(mod.astype(x.dtype), 6, axis=-1)
    y, finals = _mixer(x * (1 + sc1) + sh1, init, w_in, b_in, conv_w, conv_b, lru_wa, lru_ba, lru_wx, lru_bx,
                       lru_lam, hg_lower, hg_norm_g, ml_norm_g, w_out)
    x = _layer_norm(ALPHA * x + g1 * y, ln1_g, ln1_b)
    f = _moe(x * (1 + sc2) + sh2, w_router, b_router, w_gate, w_up, w_down)
    x = _layer_norm(ALPHA * x + g2 * f, ln2_g, ln2_b)
    return x, finals


def setup_inputs(seed: int = 0) -> dict:
    key = jax.random.key(seed)
    ks = iter(jax.random.split(key, 40))
    f32 = jnp.float32
    nrm = lambda shape, scale: scale * jax.random.normal(next(ks), shape, f32)
    x_prompt = nrm((BATCH, SEQ, D_MODEL), 1.0)
    x_sample = nrm((DEC_BATCH, DEC_SEQ, D_MODEL), 1.0)
    state_lru_h = nrm((DEC_BATCH, DEPTH, 2, D_GROUP), 0.5)
    state_hgrn_S = nrm((DEC_BATCH, DEPTH, 2, HG_HEADS, HG_DK, HG_DV), 0.5)
    state_mlstm_C = nrm((DEC_BATCH, DEPTH, 2, ML_HEADS, ML_DH, ML_DH), 0.1)
    state_mlstm_n = jnp.abs(nrm((DEC_BATCH, DEPTH, 2, ML_HEADS, ML_DH), 0.3))
    state_mlstm_m = nrm((DEC_BATCH, DEPTH, 2, ML_HEADS), 1.0)
    c = nrm((DEC_BATCH, D_MODEL), 1.0)
    c_ctx = nrm((D_MODEL,), 1.0)
    w_ada = nrm((DEPTH, D_MODEL, 6 * D_MODEL), D_MODEL ** -0.5)
    b_ada = nrm((DEPTH, 6 * D_MODEL), 0.02)
    w_in = nrm((DEPTH, D_MODEL, D_IN), D_MODEL ** -0.5)
    gate_bias = jnp.zeros((2, 2, ML_HEADS), f32).at[:, 1].set(jnp.linspace(3.0, 6.0, ML_HEADS)).reshape(-1)
    b_in = nrm((DEPTH, D_IN), 0.02).at[:, ML_GATE_OFF:].add(gate_bias)
    conv_w = nrm((DEPTH, CONV_W, D_GROUP), CONV_W ** -0.5)
    conv_b = nrm((DEPTH, D_GROUP), 0.02)
    lru_wa = nrm((DEPTH, 2, LRU_BLOCKS, LRU_BLOCK, LRU_BLOCK), LRU_BLOCK ** -0.5)
    lru_ba = nrm((DEPTH, 2, D_GROUP), 0.02)
    lru_wx = nrm((DEPTH, 2, LRU_BLOCKS, LRU_BLOCK, LRU_BLOCK), LRU_BLOCK ** -0.5)
    lru_bx = nrm((DEPTH, 2, D_GROUP), 0.02)
    a8 = jax.random.uniform(next(ks), (DEPTH, 2, D_GROUP), f32, 0.9, 0.999)
    a1 = a8 ** (1.0 / LRU_C)
    lru_lam = jnp.log(a1) - jnp.log1p(-a1)
    hg_lb = nrm((DEPTH, 2, D_GROUP), 1.0)
    hg_norm_g = 1.0 + nrm((DEPTH, D_GROUP), 0.02)
    ml_norm_g = 1.0 + nrm((DEPTH, D_GROUP), 0.02)
    w_out = nrm((DEPTH, D_MIX, D_MODEL), BETA * D_MIX ** -0.5)
    ln1_g = 1.0 + nrm((DEPTH, D_MODEL), 0.02)
    ln1_b = nrm((DEPTH, D_MODEL), 0.02)
    ln2_g = 1.0 + nrm((DEPTH, D_MODEL), 0.02)
    ln2_b = nrm((DEPTH, D_MODEL), 0.02)
    w_router = nrm((D_MODEL, N_EXPERTS), D_MODEL ** -0.5)
    b_router = nrm((N_EXPERTS,), 0.01)
    w_gate = nrm((DEPTH, N_EXPERTS, D_MODEL, D_FF), D_MODEL ** -0.5)
    w_up = nrm((DEPTH, N_EXPERTS, D_MODEL, D_FF), D_MODEL ** -0.5)
    w_down = nrm((DEPTH, N_EXPERTS, D_FF, D_MODEL), BETA * D_FF ** -0.5)
    return {'x_prompt': x_prompt, 'x_sample': x_sample,
            'state_lru_h': state_lru_h, 'state_hgrn_S': state_hgrn_S, 'state_mlstm_C': state_mlstm_C,
            'state_mlstm_n': state_mlstm_n, 'state_mlstm_m': state_mlstm_m,
            'c': c, 'c_ctx': c_ctx, 'w_ada': w_ada, 'b_ada': b_ada, 'w_in': w_in, 'b_in': b_in,
            'conv_w': conv_w, 'conv_b': conv_b, 'lru_wa': lru_wa, 'lru_ba': lru_ba, 'lru_wx': lru_wx,
            'lru_bx': lru_bx, 'lru_lam': lru_lam, 'hg_lb': hg_lb, 'hg_norm_g': hg_norm_g, 'ml_norm_g': ml_norm_g,
            'w_out': w_out, 'ln1_g': ln1_g, 'ln1_b': ln1_b, 'ln2_g': ln2_g, 'ln2_b': ln2_b,
            'w_router': w_router, 'b_router': b_router, 'w_gate': w_gate, 'w_up': w_up, 'w_down': w_down}


def reference(x_prompt, x_sample, state_lru_h, state_hgrn_S, state_mlstm_C, state_mlstm_n, state_mlstm_m,
              c, c_ctx, w_ada, b_ada, w_in, b_in, conv_w, conv_b, lru_wa, lru_ba, lru_wx, lru_bx, lru_lam,
              hg_lb, hg_norm_g, ml_norm_g, w_out, ln1_g, ln1_b, ln2_g, ln2_b,
              w_router, b_router, w_gate, w_up, w_down):
    f32 = jnp.float32
    Bp = x_prompt.shape[0]
    lb_soft = jax.nn.softmax(hg_lb.astype(f32), axis=0)
    hg_lower = jnp.cumsum(lb_soft, axis=0) - lb_soft[0:1]
    mod_ctx = jnp.einsum('d,lde->le', jax.nn.silu(c_ctx), w_ada) + b_ada
    mod_lat = jnp.einsum('bd,lde->lbe', jax.nn.silu(c), w_ada) + b_ada[:, None]
    init_ctx = (jnp.zeros((Bp, 2, D_GROUP), f32),
                jnp.zeros((Bp, 2, HG_HEADS, HG_DK, HG_DV), f32),
                jnp.zeros((Bp, 2, ML_HEADS, ML_DH, ML_DH), f32),
                jnp.zeros((Bp, 2, ML_HEADS, ML_DH), f32),
                jnp.zeros((Bp, 2, ML_HEADS), f32))
    xp = x_prompt
    xs = x_sample + _grid_pos_embed(x_sample.shape[1]).astype(x_sample.dtype)
    fin_h, fin_S, fin_C, fin_n, fin_m = [], [], [], [], []
    for l in range(DEPTH):
        lp = (w_in[l], b_in[l], conv_w[l], conv_b[l], lru_wa[l], lru_ba[l], lru_wx[l], lru_bx[l], lru_lam[l],
              hg_lower[l], hg_norm_g[l], ml_norm_g[l], w_out[l], ln1_g[l], ln1_b[l], ln2_g[l], ln2_b[l],
              w_router, b_router, w_gate[l], w_up[l], w_down[l])
        xp, (h_l, S_l, C_l, n_l, m_l) = _layer(xp, mod_ctx[l][None, None], init_ctx, *lp)
        fin_h.append(h_l)
        fin_S.append(S_l)
        fin_C.append(C_l)
        fin_n.append(n_l)
        fin_m.append(m_l)
        init_lat = (state_lru_h[:, l], state_hgrn_S[:, l], state_mlstm_C[:, l], state_mlstm_n[:, l], state_mlstm_m[:, l])
        xs, _ = _layer(xs, mod_lat[l][:, None], init_lat, *lp)
    return (xp, xs, jnp.stack(fin_h, 1), jnp.stack(fin_S, 1), jnp.stack(fin_C, 1),
            jnp.stack(fin_n, 1), jnp.stack(fin_m, 1))
```

```python
import functools

import jax
import jax.numpy as jnp
from jax import lax
from jax.experimental import pallas as pl
from jax.experimental.pallas import tpu as pltpu

F32 = jnp.float32
BF16 = jnp.bfloat16
HIGHEST = lax.Precision.HIGHEST

D_MODEL = 1024
DEPTH = 2
GRID_W = 64
D_GROUP = 512
D_MIX = 3 * D_GROUP
LRU_BLOCKS = 8
LRU_BLOCK = D_GROUP // LRU_BLOCKS
LRU_C = 8.0
N_HEADS = 4
D_HEAD = D_GROUP // N_HEADS
N_EXPERTS = 16
N_GROUPS = 4
EXPERTS_PER_GROUP = N_EXPERTS // N_GROUPS
D_FF = 512
ALPHA = (2.0 * DEPTH) ** 0.25
EPS = 1e-5
NEG = -1e30
TINY = 1e-30
N_GATE = 4 * N_HEADS
D_IN = 11 * D_GROUP + N_GATE
N_COLGRP = 12
D_IN_PAD = N_COLGRP * D_GROUP
GATE_BLK = (11 * D_GROUP) // 128

SUBLANES = 8
LANES = 128
VMEM_LIMIT = 56 * 1024 * 1024

LRU_TC = 32
HG_L = 64
HG_SUB = 16
ML_L = 64


def _cparams(n_axes):
    return pltpu.CompilerParams(dimension_semantics=("arbitrary",) * n_axes,
                                vmem_limit_bytes=VMEM_LIMIT)


def _dot(a, b):
    return jnp.dot(a.astype(BF16), b.astype(BF16), preferred_element_type=F32)


def _dot_nt(a, b):
    return lax.dot_general(a.astype(BF16), b.astype(BF16), (((1,), (1,)), ((), ())),
                           preferred_element_type=F32)


def _dot_tn(a, b):
    return lax.dot_general(a.astype(BF16), b.astype(BF16), (((0,), (0,)), ((), ())),
                           preferred_element_type=F32)


def _dot_f32(a, b):
    return jnp.dot(a, b, precision=HIGHEST, preferred_element_type=F32)


def _dot_split3(mask, x, mask_on_left):
    m = jnp.where(mask, 1.0, 0.0).astype(BF16)
    hi = x.astype(BF16)
    r1 = x - hi.astype(F32)
    mid = r1.astype(BF16)
    lo = (r1 - mid.astype(F32)).astype(BF16)
    mm = (lambda p: jnp.dot(m, p, preferred_element_type=F32)) if mask_on_left else (
        lambda p: jnp.dot(p, m, preferred_element_type=F32))
    return mm(hi) + mm(mid) + mm(lo)


def _softplus(x):
    return jnp.maximum(x, 0.0) + jnp.log1p(jnp.exp(-jnp.abs(x)))


def _silu(x):
    return x * jax.nn.sigmoid(x)


def _tri_masks(n):
    r = lax.broadcasted_iota(jnp.int32, (n, n), 0)
    c = lax.broadcasted_iota(jnp.int32, (n, n), 1)
    return r >= c, r <= c


def _ada_kernel(c_ref, w_ref, b_ref, o_ref):
    o_ref[0] = _dot_f32(_silu(c_ref[...]), w_ref[0]) + b_ref[0]


def _ada_call(c16, w_ada, b_ada):
    tn = 1536
    return pl.pallas_call(
        _ada_kernel,
        grid=(DEPTH, 6 * D_MODEL // tn),
        in_specs=[pl.BlockSpec((16, D_MODEL), lambda l, j: (0, 0)),
                  pl.BlockSpec((1, D_MODEL, tn), lambda l, j: (l, 0, j)),
                  pl.BlockSpec((1, 1, tn), lambda l, j: (l, 0, j))],
        out_specs=pl.BlockSpec((1, 16, tn), lambda l, j: (l, 0, j)),
        out_shape=jax.ShapeDtypeStruct((DEPTH, 16, 6 * D_MODEL), F32),
        compiler_params=_cparams(2),
        name="adaln",
    )(c16, w_ada, b_ada.reshape(DEPTH, 1, 6 * D_MODEL))


INPROJ_CW = 768


def _inproj_kernel(x_ref, sh_ref, sc_ref, w_ref, b_ref, z_ref):
    tm = x_ref.shape[0]
    x = x_ref[...].reshape(tm // SUBLANES, SUBLANES, D_MODEL)
    h = x * (1.0 + sc_ref[...])[None] + sh_ref[...][None]
    hb = h.reshape(tm, D_MODEL).astype(BF16)
    for j in range(D_IN_PAD // INPROJ_CW):
        cs = slice(j * INPROJ_CW, (j + 1) * INPROJ_CW)
        z_ref[:, cs] = jnp.dot(hb, w_ref[:, cs], preferred_element_type=F32) + b_ref[:, cs]


def _inproj_call(x, mod, w, b):
    n = x.shape[0]
    tm = 256
    return pl.pallas_call(
        _inproj_kernel,
        grid=(n // tm,),
        in_specs=[pl.BlockSpec((tm, D_MODEL), lambda i: (i, 0)),
                  pl.BlockSpec((SUBLANES, D_MODEL), lambda i: (0, 0)),
                  pl.BlockSpec((SUBLANES, D_MODEL), lambda i: (0, 1)),
                  pl.BlockSpec((D_MODEL, D_IN_PAD), lambda i: (0, 0)),
                  pl.BlockSpec((1, D_IN_PAD), lambda i: (0, 0))],
        out_specs=pl.BlockSpec((tm, D_IN_PAD), lambda i: (i, 0)),
        out_shape=jax.ShapeDtypeStruct((n, D_IN_PAD), F32),
        compiler_params=_cparams(1),
        name="inproj",
    )(x, mod, mod, w, b)


def _rglru_kernel(xr_ref, gr_ref, cw_ref, cb_ref, wg_ref, bg_ref, lam_ref, h0_ref,
                  y_ref, hT_ref, xp_ref, of_ref):
    T = xr_ref.shape[0]
    TC = LRU_TC
    nch = T // TC
    pad = jnp.zeros((4, SUBLANES, LANES), F32)
    xp_ref[0:4] = pad
    xp_ref[T + 4:T + 8] = pad

    def copy_in(c, carry):
        t0 = pl.multiple_of(c * TC, TC)
        xp_ref[pl.ds(t0 + 4, TC)] = xr_ref[pl.ds(t0, TC)]
        return carry

    lax.fori_loop(0, nch, copy_in, 0)

    cw = cw_ref[...]
    cb = cb_ref[...]
    sp = _softplus(-lam_ref[0])

    def gates(t0, d):
        xc = (cw[0:1][None] * xp_ref[pl.ds(t0 + 2, TC)] + cw[1:2][None] * xp_ref[pl.ds(t0 + 3, TC)]
              + cw[2:3][None] * xp_ref[pl.ds(t0 + 4, TC)] + cw[3:4][None] * xp_ref[pl.ds(t0 + 5, TC)]
              + cb[None])
        x2 = xc.reshape(TC * SUBLANES, LANES)
        g = (jnp.dot(x2.astype(BF16), wg_ref[0, :, d * 256:(d + 1) * 256], preferred_element_type=F32)
             + bg_ref[0, :, d * 256:(d + 1) * 256])
        r = jax.nn.sigmoid(g[:, :LANES])
        i = jax.nn.sigmoid(g[:, LANES:])
        log_a = -LRU_C * r * sp[:, d * LANES:(d + 1) * LANES]
        a = jnp.exp(log_a)
        u = jnp.sqrt(jnp.maximum(1.0 - jnp.exp(2.0 * log_a), 0.0)) * (i * x2)
        return a.reshape(TC, SUBLANES, LANES), u.reshape(TC, SUBLANES, LANES)

    def fwd(c, h):
        t0 = pl.multiple_of(c * TC, TC)
        a, u = gates(t0, 0)
        hs = []
        for i in range(TC):
            h = a[i] * h + u[i]
            hs.append(h)
        of_ref[pl.ds(t0, TC)] = jnp.stack(hs)
        return h

    hT_ref[0] = lax.fori_loop(0, nch, fwd, h0_ref[0])

    def bwd(c, h):
        t0 = pl.multiple_of((nch - 1 - c) * TC, TC)
        a, u = gates(t0, 1)
        hs = [None] * TC
        for i in range(TC - 1, -1, -1):
            h = a[i] * h + u[i]
            hs[i] = h
        o = of_ref[pl.ds(t0, TC)] + jnp.stack(hs)
        y_ref[pl.ds(t0, TC)] = jax.nn.gelu(gr_ref[pl.ds(t0, TC)]) * o
        return h

    hT_ref[1] = lax.fori_loop(0, nch, bwd, h0_ref[1])


def _rglru_call(z3, h0, cw, cb, wg, bg, lam):
    T, B, _ = z3.shape
    nb = B // SUBLANES
    ncb = D_GROUP // LANES
    blk = (T, SUBLANES, LANES)
    return pl.pallas_call(
        _rglru_kernel,
        grid=(nb, ncb),
        in_specs=[pl.BlockSpec(blk, lambda b, c: (0, b, c)),
                  pl.BlockSpec(blk, lambda b, c: (0, b, ncb + c)),
                  pl.BlockSpec((4, LANES), lambda b, c: (0, c)),
                  pl.BlockSpec((1, LANES), lambda b, c: (0, c)),
                  pl.BlockSpec((1, LANES, 4 * LANES), lambda b, c: (c, 0, 0)),
                  pl.BlockSpec((1, 1, 4 * LANES), lambda b, c: (c, 0, 0)),
                  pl.BlockSpec((1, 1, 2 * LANES), lambda b, c: (c, 0, 0)),
                  pl.BlockSpec((2, SUBLANES, LANES), lambda b, c: (0, b, c))],
        out_specs=[pl.BlockSpec(blk, lambda b, c: (0, b, c)),
                   pl.BlockSpec((2, SUBLANES, LANES), lambda b, c: (0, b, c))],
        out_shape=[jax.ShapeDtypeStruct((T, B, D_GROUP), F32),
                   jax.ShapeDtypeStruct((2, B, D_GROUP), F32)],
        scratch_shapes=[pltpu.VMEM((T + 8, SUBLANES, LANES), F32),
                        pltpu.VMEM((T, SUBLANES, LANES), F32)],
        compiler_params=_cparams(2),
        name="rglru",
    )(z3, z3, cw, cb, wg, bg, lam, h0)


def _hgrn_kernel(q_ref, ff_ref, fb_ref, v_ref, og_ref, lb_ref, ng_ref, s0_ref,
                 y_ref, sT_ref, st_ref, of_ref):
    T = q_ref.shape[0]
    L = HG_L
    SUB = HG_SUB
    nch = T // L
    nsub = L // SUB
    tril, triu = _tri_masks(L)

    def chunk(t0, d, hd):
        cs = slice(hd * D_HEAD, (hd + 1) * D_HEAD)
        qh = _silu(q_ref[pl.ds(t0, L), cs])
        vh = v_ref[pl.ds(t0, L), cs]
        pre = (ff_ref if d == 0 else fb_ref)[pl.ds(t0, L), cs]
        lb = lb_ref[d:d + 1, cs]
        sig = jax.nn.sigmoid(pre)
        log_f = jnp.log(jnp.maximum(lb + (1.0 - lb) * sig, TINY))
        kk = (1.0 - lb) * (1.0 - sig)
        mask = tril if d == 0 else triu
        b = _dot_split3(mask, log_f, mask_on_left=True)
        b_tot = b[L - 1:L] if d == 0 else b[0:1]
        o_rows = []
        for i in range(nsub):
            rows = slice(i * SUB, (i + 1) * SUB)
            if d == 0:
                cols = slice(0, (i + 1) * SUB)
                edge = b[i * SUB - 1:i * SUB] if i > 0 else 0.0
            else:
                cols = slice(i * SUB, L)
                edge = b[(i + 1) * SUB:(i + 1) * SUB + 1] if i < nsub - 1 else 0.0
            a = _dot_nt(qh[rows] * jnp.exp(b[rows] - edge), kk[cols] * jnp.exp(edge - b[cols]))
            o_rows.append(_dot(jnp.where(mask[rows, cols], a, 0.0), vh[cols]))
        st = st_ref[d, hd]
        o = jnp.concatenate(o_rows, axis=0) + _dot_nt(qh * jnp.exp(b), st)
        st_ref[d, hd] = st * jnp.exp(b_tot) + _dot_tn(vh, kk * jnp.exp(b_tot - b))
        return o

    for d in range(2):
        for hd in range(N_HEADS):
            st_ref[d, hd] = s0_ref[d, hd].T

    def fwd(c, carry):
        t0 = pl.multiple_of(c * L, L)
        for hd in range(N_HEADS):
            of_ref[pl.ds(t0, L), hd * D_HEAD:(hd + 1) * D_HEAD] = chunk(t0, 0, hd)
        return carry

    lax.fori_loop(0, nch, fwd, 0)

    def bwd(c, carry):
        t0 = pl.multiple_of((nch - 1 - c) * L, L)
        for hd in range(N_HEADS):
            cs = slice(hd * D_HEAD, (hd + 1) * D_HEAD)
            o = of_ref[pl.ds(t0, L), cs] + chunk(t0, 1, hd)
            o = o * lax.rsqrt(jnp.mean(jnp.square(o), -1, keepdims=True) + EPS)
            y_ref[pl.ds(t0, L), cs] = o * ng_ref[:, cs] * _silu(og_ref[pl.ds(t0, L), cs])
        return carry

    lax.fori_loop(0, nch, bwd, 0)

    for d in range(2):
        for hd in range(N_HEADS):
            sT_ref[d, hd] = st_ref[d, hd].T


def _seq_col_spec(T, j):
    return pl.BlockSpec((T, D_GROUP), lambda b, j=j: (0, b * N_COLGRP + j))


def _hgrn_call(z2, B, s0, lower, norm_g):
    T = z2.shape[0]
    col = functools.partial(_seq_col_spec, T)
    st_spec = pl.BlockSpec((None, 2, N_HEADS, D_HEAD, D_HEAD), lambda b: (b, 0, 0, 0, 0))
    return pl.pallas_call(
        _hgrn_kernel,
        grid=(B,),
        in_specs=[col(2), col(3), col(4), col(5), col(6),
                  pl.BlockSpec((2, D_GROUP), lambda b: (0, 0)),
                  pl.BlockSpec((1, D_GROUP), lambda b: (0, 0)),
                  st_spec],
        out_specs=[pl.BlockSpec((T, D_GROUP), lambda b: (0, b)), st_spec],
        out_shape=[jax.ShapeDtypeStruct((T, B * D_GROUP), F32),
                   jax.ShapeDtypeStruct((B, 2, N_HEADS, D_HEAD, D_HEAD), F32)],
        scratch_shapes=[pltpu.VMEM((2, N_HEADS, D_HEAD, D_HEAD), F32),
                        pltpu.VMEM((T, D_GROUP), F32)],
        compiler_params=_cparams(1),
        name="hgrn2",
    )(z2, z2, z2, z2, z2, lower, norm_g, s0)


def _mlstm_kernel(q_ref, k_ref, v_ref, og_ref, g_ref, gt_ref, ng_ref, c0_ref, n0_ref, m0_ref,
                  y_ref, cT_ref, nT_ref, mT_ref, of_ref):
    T = q_ref.shape[0]
    L = ML_L
    nch = T // L
    tril, triu = _tri_masks(L)
    gcol = lax.broadcasted_iota(jnp.int32, (L, N_GATE), 1)
    grow = lax.broadcasted_iota(jnp.int32, (N_GATE, L), 0)

    cT_ref[...] = c0_ref[...]
    nT_ref[...] = n0_ref[...]
    mT_ref[...] = m0_ref[...]

    def chunk_gates(c, t0, d):
        g = g_ref[pl.ds(t0, L), 0:N_GATE]
        gt = gt_ref[c]
        g = jnp.where(gcol % 8 >= N_HEADS, jax.nn.log_sigmoid(g), g)
        gt = jnp.where(grow % 8 >= N_HEADS, jax.nn.log_sigmoid(gt), gt)
        bc = _dot_split3(tril if d == 0 else triu, g, mask_on_left=True)
        br = _dot_split3(triu if d == 0 else tril, gt, mask_on_left=False)
        return g, gt, bc, br

    def chunk(t0, d, hd, gates):
        g, gt, bc, br = gates
        cs = slice(hd * D_HEAD, (hd + 1) * D_HEAD)
        ic = d * 8 + hd
        fc = d * 8 + N_HEADS + hd
        li_col, li_row = g[:, ic:ic + 1], gt[ic:ic + 1, :]
        b_col, b_row = bc[:, fc:fc + 1], br[fc:fc + 1, :]
        b_tot = b_col[L - 1:L] if d == 0 else b_col[0:1]
        mask = tril if d == 0 else triu
        qh = q_ref[pl.ds(t0, L), cs]
        kh = k_ref[pl.ds(t0, L), cs] * (D_HEAD ** -0.5)
        vh = v_ref[pl.ds(t0, L), cs]
        c_st = cT_ref[d, hd]
        n_st = nT_ref[d, hd:hd + 1, :]
        m_st = mT_ref[d, hd:hd + 1, 0:1]
        dm = jnp.where(mask, b_col - b_row + li_row, NEG)
        inter = b_col + m_st
        m_t = jnp.maximum(inter, jnp.max(dm, axis=1, keepdims=True))
        s_inter = jnp.exp(inter - m_t)
        w = jnp.exp(dm - m_t) * _dot_nt(qh, kh)
        num = _dot(w, vh) + s_inter * _dot(qh, c_st)
        den = jnp.sum(w, axis=1, keepdims=True) + s_inter * jnp.sum(qh * n_st, axis=1, keepdims=True)
        h = num / jnp.maximum(jnp.abs(den), jnp.exp(-m_t))
        w_state = b_tot - b_col + li_col
        m_loc = jnp.max(w_state, axis=0, keepdims=True)
        ek = jnp.exp(w_state - m_loc) * kh
        m_new = jnp.maximum(b_tot + m_st, m_loc)
        s_old = jnp.exp(b_tot + m_st - m_new)
        s_new = jnp.exp(m_loc - m_new)
        cT_ref[d, hd] = s_old * c_st + s_new * _dot_tn(ek, vh)
        nT_ref[d, hd:hd + 1, :] = s_old * n_st + s_new * jnp.sum(ek, axis=0, keepdims=True)
        mT_ref[d, hd:hd + 1, :] = jnp.broadcast_to(m_new, (1, LANES))
        return h

    def fwd(c, carry):
        t0 = pl.multiple_of(c * L, L)
        gates = chunk_gates(c, t0, 0)
        for hd in range(N_HEADS):
            of_ref[pl.ds(t0, L), hd * D_HEAD:(hd + 1) * D_HEAD] = chunk(t0, 0, hd, gates)
        return carry

    lax.fori_loop(0, nch, fwd, 0)

    def bwd(c, carry):
        cc = nch - 1 - c
        t0 = pl.multiple_of(cc * L, L)
        gates = chunk_gates(cc, t0, 1)
        for hd in range(N_HEADS):
            cs = slice(hd * D_HEAD, (hd + 1) * D_HEAD)
            o = of_ref[pl.ds(t0, L), cs] + chunk(t0, 1, hd, gates)
            mu = jnp.mean(o, -1, keepdims=True)
            var = jnp.mean(jnp.square(o - mu), -1, keepdims=True)
            o = (o - mu) * lax.rsqrt(var + EPS)
            y_ref[pl.ds(t0, L), cs] = jax.nn.sigmoid(og_ref[pl.ds(t0, L), cs]) * (o * ng_ref[:, cs])
        return carry

    lax.fori_loop(0, nch, bwd, 0)


def _mlstm_call(z2, B, gt, c0, n0, m0, norm_g):
    T = z2.shape[0]
    nch = T // ML_L
    col = functools.partial(_seq_col_spec, T)
    c_spec = pl.BlockSpec((None, 2, N_HEADS, D_HEAD, D_HEAD), lambda b: (b, 0, 0, 0, 0))
    v_spec = pl.BlockSpec((None, 2, N_HEADS, D_HEAD), lambda b: (b, 0, 0, 0))
    return pl.pallas_call(
        _mlstm_kernel,
        grid=(B,),
        in_specs=[col(7), col(8), col(9), col(10),
                  pl.BlockSpec((T, LANES), lambda b: (0, b * (D_IN_PAD // LANES) + GATE_BLK)),
                  pl.BlockSpec((None, nch, N_GATE, ML_L), lambda b: (b, 0, 0, 0)),
                  pl.BlockSpec((1, D_GROUP), lambda b: (0, 0)),
                  c_spec, v_spec, v_spec],
        out_specs=[pl.BlockSpec((T, D_GROUP), lambda b: (0, b)), c_spec, v_spec, v_spec],
        out_shape=[jax.ShapeDtypeStruct((T, B * D_GROUP), F32),
                   jax.ShapeDtypeStruct((B, 2, N_HEADS, D_HEAD, D_HEAD), F32),
                   jax.ShapeDtypeStruct((B, 2, N_HEADS, D_HEAD), F32),
                   jax.ShapeDtypeStruct((B, 2, N_HEADS, D_HEAD), F32)],
        scratch_shapes=[pltpu.VMEM((T, D_GROUP), F32)],
        compiler_params=_cparams(1),
        name="mlstm",
    )(z2, z2, z2, z2, z2, gt, norm_g, c0, n0, m0)


def _layer_norm(v, g, b):
    mu = jnp.mean(v, -1, keepdims=True)
    var = jnp.mean(jnp.square(v - mu), -1, keepdims=True)
    return (v - mu) * lax.rsqrt(var + EPS) * g + b


def _modulate(v, scale, shift):
    tm = v.shape[0]
    v3 = v.reshape(tm // SUBLANES, SUBLANES, D_MODEL)
    return (v3 * (1.0 + scale)[None] + shift[None]).reshape(tm, D_MODEL)


def _gate_rows(v, gate):
    tm = v.shape[0]
    return (v.reshape(tm // SUBLANES, SUBLANES, D_MODEL) * gate[None]).reshape(tm, D_MODEL)


def _route(logits_t, b_router):
    mx = jnp.max(logits_t, axis=0, keepdims=True)
    ex = jnp.exp(logits_t - mx)
    scores = ex / jnp.sum(ex, axis=0, keepdims=True)
    sel = scores + b_router
    rows = [sel[e:e + 1] for e in range(N_EXPERTS)]
    first, second, grp = [], [], []
    for gi in range(N_GROUPS):
        r = rows[gi * EXPERTS_PER_GROUP:(gi + 1) * EXPERTS_PER_GROUP]
        m1 = functools.reduce(jnp.maximum, r)
        taken = jnp.zeros_like(m1, dtype=jnp.bool_)
        f = []
        for x in r:
            hit = jnp.logical_and(x == m1, jnp.logical_not(taken))
            f.append(hit)
            taken = jnp.logical_or(taken, hit)
        rest = [jnp.where(fi, -jnp.inf, x) for fi, x in zip(f, r)]
        m2 = functools.reduce(jnp.maximum, rest)
        taken = jnp.zeros_like(m1, dtype=jnp.bool_)
        s = []
        for x in rest:
            hit = jnp.logical_and(x == m2, jnp.logical_not(taken))
            s.append(hit)
            taken = jnp.logical_or(taken, hit)
        first.append(f)
        second.append(s)
        grp.append(m1 + m2)
    gmax = functools.reduce(jnp.maximum, grp)
    taken = jnp.zeros_like(gmax, dtype=jnp.bool_)
    chosen = []
    for gi in range(N_GROUPS):
        best = jnp.logical_and(grp[gi] == gmax, jnp.logical_not(taken))
        taken = jnp.logical_or(taken, best)
        for j in range(EXPERTS_PER_GROUP):
            chosen.append(jnp.logical_and(best, jnp.logical_or(first[gi][j], second[gi][j])))
    picked = [jnp.where(ch, scores[e:e + 1], 0.0) for e, ch in enumerate(chosen)]
    denom = functools.reduce(lambda a, b: a + b, picked)
    return jnp.concatenate([p / denom for p in picked], axis=0)


def _outproj_kernel(yr_ref, yh_ref, ym_ref, x_ref, g1_ref, sh2_ref, sc2_ref, w_ref, lg_ref, lb_ref,
                    wr_ref, br_ref, x1_ref, hm_ref, cmb_ref):
    y = (jnp.dot(yr_ref[...].astype(BF16), w_ref[0:D_GROUP], preferred_element_type=F32)
         + jnp.dot(yh_ref[...].astype(BF16), w_ref[D_GROUP:2 * D_GROUP], preferred_element_type=F32)
         + jnp.dot(ym_ref[...].astype(BF16), w_ref[2 * D_GROUP:], preferred_element_type=F32))
    x1 = _layer_norm(ALPHA * x_ref[...] + _gate_rows(y, g1_ref[...]), lg_ref[...], lb_ref[...])
    x1_ref[...] = x1
    hm = _modulate(x1, sc2_ref[...], sh2_ref[...])
    hm_ref[...] = hm.astype(BF16)
    logits_t = lax.dot_general(wr_ref[...], hm, (((1,), (1,)), ((), ())), precision=HIGHEST,
                               preferred_element_type=F32)
    cmb_ref[...] = _route(logits_t, br_ref[...])


def _outproj_call(yr, yh, ym, x, mod, w_out, ln_g, ln_b, w_router_t, b_router):
    n = x.shape[0]
    tm = 256
    row = lambda w: pl.BlockSpec((tm, w), lambda i: (i, 0))
    modspec = lambda j: pl.BlockSpec((SUBLANES, D_MODEL), lambda i, j=j: (0, j))
    full = lambda a: pl.BlockSpec(a.shape, lambda i: (0,) * a.ndim)
    return pl.pallas_call(
        _outproj_kernel,
        grid=(n // tm,),
        in_specs=[row(D_GROUP), row(D_GROUP), row(D_GROUP), row(D_MODEL),
                  modspec(2), modspec(3), modspec(4),
                  full(w_out), full(ln_g), full(ln_b), full(w_router_t), full(b_router)],
        out_specs=[row(D_MODEL), row(D_MODEL), pl.BlockSpec((N_EXPERTS, tm), lambda i: (0, i))],
        out_shape=[jax.ShapeDtypeStruct((n, D_MODEL), F32),
                   jax.ShapeDtypeStruct((n, D_MODEL), BF16),
                   jax.ShapeDtypeStruct((N_EXPERTS, n), F32)],
        compiler_params=_cparams(1),
        name="outproj_ln_router",
    )(yr, yh, ym, x, mod, mod, mod, w_out, ln_g, ln_b, w_router_t, b_router)


def _moe_kernel(hm_ref, cmb_ref, x1_ref, g2_ref, wg_ref, wu_ref, wd_ref, lg_ref, lb_ref, o_ref, acc_ref):
    e = pl.program_id(1)

    @pl.when(e == 0)
    def _():
        acc_ref[...] = jnp.zeros_like(acc_ref)

    hm = hm_ref[...]
    cmb = cmb_ref[...]
    lane = lax.broadcasted_iota(jnp.int32, cmb.shape, 1)
    ce = jnp.sum(jnp.where(lane == e, cmb, 0.0), axis=1, keepdims=True)
    hg = jnp.dot(hm, wg_ref[0], preferred_element_type=F32)
    hu = jnp.dot(hm, wu_ref[0], preferred_element_type=F32)
    act = _silu(hg) * hu * ce
    acc_ref[...] += jnp.dot(act.astype(BF16), wd_ref[0], preferred_element_type=F32)

    @pl.when(e == N_EXPERTS - 1)
    def _():
        v = ALPHA * x1_ref[...] + _gate_rows(acc_ref[...], g2_ref[...])
        o_ref[...] = _layer_norm(v, lg_ref[...], lb_ref[...])


def _moe_call(hm, cmb, x1, mod, w_gate, w_up, w_down, ln_g, ln_b):
    n = x1.shape[0]
    tm = 1024
    return pl.pallas_call(
        _moe_kernel,
        grid=(n // tm, N_EXPERTS),
        in_specs=[pl.BlockSpec((tm, D_MODEL), lambda i, e: (i, 0)),
                  pl.BlockSpec((tm, N_EXPERTS), lambda i, e: (i, 0)),
                  pl.BlockSpec((tm, D_MODEL), lambda i, e: (i, 0)),
                  pl.BlockSpec((SUBLANES, D_MODEL), lambda i, e: (0, 5)),
                  pl.BlockSpec((1, D_MODEL, D_FF), lambda i, e: (e, 0, 0)),
                  pl.BlockSpec((1, D_MODEL, D_FF), lambda i, e: (e, 0, 0)),
                  pl.BlockSpec((1, D_FF, D_MODEL), lambda i, e: (e, 0, 0)),
                  pl.BlockSpec((1, D_MODEL), lambda i, e: (0, 0)),
                  pl.BlockSpec((1, D_MODEL), lambda i, e: (0, 0))],
        out_specs=pl.BlockSpec((tm, D_MODEL), lambda i, e: (i, 0)),
        out_shape=jax.ShapeDtypeStruct((n, D_MODEL), F32),
        scratch_shapes=[pltpu.VMEM((tm, D_MODEL), F32)],
        compiler_params=_cparams(2),
        name="moe_ln",
    )(hm, cmb, x1, mod, w_gate, w_up, w_down, ln_g, ln_b)


def _grid_pos_embed(n_tokens):
    rows = n_tokens // GRID_W
    r = jnp.repeat(jnp.arange(rows, dtype=F32), GRID_W)
    c = jnp.tile(jnp.arange(GRID_W, dtype=F32), rows)
    q = D_MODEL // 4
    freq = jnp.exp(-jnp.log(10000.0) * jnp.arange(q, dtype=F32) / q)
    ar = r[:, None] * freq
    ac = c[:, None] * freq
    return jnp.concatenate([jnp.sin(ar), jnp.cos(ar), jnp.sin(ac), jnp.cos(ac)], axis=-1)


def _lru_gate_params(wa, ba, wx, bx, lam):
    ncb = D_GROUP // LANES

    def dense(w):
        z = jnp.zeros((ncb, LANES, LANES), F32)
        z = z.at[:, :LRU_BLOCK, :LRU_BLOCK].set(w[0::2])
        return z.at[:, LRU_BLOCK:, LRU_BLOCK:].set(w[1::2])

    wg = jnp.concatenate([dense(wa[0]), dense(wx[0]), dense(wa[1]), dense(wx[1])], axis=-1).astype(BF16)
    per_blk = lambda v: v.reshape(ncb, 1, LANES)
    bg = jnp.concatenate([per_blk(ba[0]), per_blk(bx[0]), per_blk(ba[1]), per_blk(bx[1])], axis=-1)
    lm = jnp.concatenate([per_blk(lam[0]), per_blk(lam[1])], axis=-1)
    return wg, bg, lm


def _mixer_states(B, l, states):
    if states is None:
        return (jnp.zeros((2, B, D_GROUP), F32),
                jnp.zeros((B, 2, N_HEADS, D_HEAD, D_HEAD), F32),
                jnp.zeros((B, 2, N_HEADS, D_HEAD, D_HEAD), F32),
                jnp.zeros((B, 2, N_HEADS, D_HEAD), F32),
                jnp.zeros((B, 2, N_HEADS, D_HEAD), F32))
    h, s, c, n, m = states
    return (jnp.swapaxes(h[:, l], 0, 1), s[:, l], c[:, l], n[:, l],
            jnp.broadcast_to(m[:, l][..., None], (B, 2, N_HEADS, D_HEAD)))


def kernel(x_prompt, x_sample, state_lru_h, state_hgrn_S, state_mlstm_C, state_mlstm_n, state_mlstm_m,
           c, c_ctx, w_ada, b_ada, w_in, b_in, conv_w, conv_b, lru_wa, lru_ba, lru_wx, lru_bx, lru_lam,
           hg_lb, hg_norm_g, ml_norm_g, w_out, ln1_g, ln1_b, ln2_g, ln2_b,
           w_router, b_router, w_gate, w_up, w_down):
    Bp, Tp, _ = x_prompt.shape
    Bs, Ts, _ = x_sample.shape
    assert Bs == SUBLANES and Bp % SUBLANES == 0
    assert Tp % max(HG_L, ML_L, LRU_TC) == 0 and Ts % max(HG_L, ML_L, LRU_TC) == 0

    lb_soft = jax.nn.softmax(hg_lb.astype(F32), axis=0)
    hg_lower = jnp.cumsum(lb_soft, axis=0) - lb_soft[0:1]

    c16 = jnp.concatenate([c, c_ctx[None], jnp.zeros((16 - Bs - 1, D_MODEL), F32)], axis=0)
    mod = _ada_call(c16, w_ada, b_ada)

    w_in_p = jnp.pad(w_in, ((0, 0), (0, 0), (0, D_IN_PAD - D_IN))).astype(BF16)
    b_in_p = jnp.pad(b_in, ((0, 0), (0, D_IN_PAD - D_IN))).reshape(DEPTH, 1, D_IN_PAD)
    w_out_b = w_out.astype(BF16)
    w_gate_b, w_up_b, w_down_b = w_gate.astype(BF16), w_up.astype(BF16), w_down.astype(BF16)
    w_router_t = w_router.T
    b_router_c = b_router.reshape(N_EXPERTS, 1)

    xs = x_sample + _grid_pos_embed(Ts).astype(x_sample.dtype)
    streams = [
        dict(x=jnp.swapaxes(x_prompt, 0, 1).reshape(Tp * Bp, D_MODEL), T=Tp, B=Bp, states=None),
        dict(x=jnp.swapaxes(xs, 0, 1).reshape(Ts * Bs, D_MODEL), T=Ts, B=Bs,
             states=(state_lru_h, state_hgrn_S, state_mlstm_C, state_mlstm_n, state_mlstm_m)),
    ]
    finals = []
    for l in range(DEPTH):
        wg, bg, lm = _lru_gate_params(lru_wa[l], lru_ba[l], lru_wx[l], lru_bx[l], lru_lam[l])
        row = lambda v: v[l].reshape(1, -1)
        for si, st in enumerate(streams):
            T, B = st["T"], st["B"]
            mod_s = mod[l, :SUBLANES] if si == 1 else jnp.broadcast_to(mod[l, Bs:Bs + 1], (SUBLANES, 6 * D_MODEL))
            h0, s0, c0, n0, m0 = _mixer_states(B, l, st["states"])
            z = _inproj_call(st["x"], mod_s, w_in_p[l], b_in_p[l])
            z3 = z.reshape(T, B, D_IN_PAD)
            gt = jnp.transpose(z3[:, :, 11 * D_GROUP:D_IN].reshape(T // ML_L, ML_L, B, N_GATE), (2, 0, 3, 1))
            y_r, h_T = _rglru_call(z3, h0, conv_w[l], row(conv_b), wg, bg, lm)
            z2 = z.reshape(T, B * D_IN_PAD)
            y_h, s_T = _hgrn_call(z2, B, s0, hg_lower[l], row(hg_norm_g))
            y_m, c_T, n_T, m_T = _mlstm_call(z2, B, gt, c0, n0, m0, row(ml_norm_g))
            flat = lambda y: y.reshape(T * B, D_GROUP)
            x1, hm, cmb_t = _outproj_call(flat(y_r), flat(y_h), flat(y_m), st["x"], mod_s, w_out_b[l],
                                          row(ln1_g), row(ln1_b), w_router_t, b_router_c)
            st["x"] = _moe_call(hm, cmb_t.T, x1, mod_s, w_gate_b[l], w_up_b[l], w_down_b[l],
                                row(ln2_g), row(ln2_b))
            if si == 0:
                finals.append((jnp.swapaxes(h_T, 0, 1), s_T, c_T, n_T, m_T[..., 0]))
    outs = [jnp.swapaxes(st["x"].reshape(st["T"], st["B"], D_MODEL), 0, 1) for st in streams]
    stack = lambda i: jnp.stack([f[i] for f in finals], axis=1)
    return (outs[0], outs[1], stack(0), stack(1), stack(2), stack(3), stack(4))
```

```python
import functools

import jax
import jax.numpy as jnp
from jax import lax
from jax.experimental import pallas as pl
from jax.experimental.pallas import tpu as pltpu

F32 = jnp.float32
BF16 = jnp.bfloat16
HIGHEST = lax.Precision.HIGHEST

D_MODEL = 1024
DEPTH = 2
GRID_W = 64
D_GROUP = 512
D_MIX = 3 * D_GROUP
LRU_BLOCKS = 8
LRU_BLOCK = D_GROUP // LRU_BLOCKS
LRU_C = 8.0
N_HEADS = 4
D_HEAD = D_GROUP // N_HEADS
N_EXPERTS = 16
N_GROUPS = 4
EXPERTS_PER_GROUP = N_EXPERTS // N_GROUPS
D_FF = 512
ALPHA = (2.0 * DEPTH) ** 0.25
EPS = 1e-5
NEG = -1e30
TINY = 1e-30
N_GATE = 4 * N_HEADS
D_IN = 11 * D_GROUP + N_GATE
N_COLGRP = 12
D_IN_PAD = N_COLGRP * D_GROUP
GATE_BLK = (11 * D_GROUP) // 128

SUBLANES = 8
LANES = 128
VMEM_LIMIT = 56 * 1024 * 1024

LRU_TC = 32
HG_L = 64
HG_SUB = 16
ML_L = 64


def _cparams(n_axes):
    return pltpu.CompilerParams(dimension_semantics=("arbitrary",) * n_axes,
                                vmem_limit_bytes=VMEM_LIMIT)


def _dot(a, b):
    return jnp.dot(a.astype(BF16), b.astype(BF16), preferred_element_type=F32)


def _dot_nt(a, b):
    return lax.dot_general(a.astype(BF16), b.astype(BF16), (((1,), (1,)), ((), ())),
                           preferred_element_type=F32)


def _dot_tn(a, b):
    return lax.dot_general(a.astype(BF16), b.astype(BF16), (((0,), (0,)), ((), ())),
                           preferred_element_type=F32)


def _dot_f32(a, b):
    return jnp.dot(a, b, precision=HIGHEST, preferred_element_type=F32)


def _dot_split3(mask, x, mask_on_left):
    m = jnp.where(mask, 1.0, 0.0).astype(BF16)
    hi = x.astype(BF16)
    r1 = x - hi.astype(F32)
    mid = r1.astype(BF16)
    lo = (r1 - mid.astype(F32)).astype(BF16)
    mm = (lambda p: jnp.dot(m, p, preferred_element_type=F32)) if mask_on_left else (
        lambda p: jnp.dot(p, m, preferred_element_type=F32))
    return mm(hi) + mm(mid) + mm(lo)


def _softplus(x):
    return jnp.maximum(x, 0.0) + jnp.log1p(jnp.exp(-jnp.abs(x)))


def _silu(x):
    return x * jax.nn.sigmoid(x)


def _tri_masks(n):
    r = lax.broadcasted_iota(jnp.int32, (n, n), 0)
    c = lax.broadcasted_iota(jnp.int32, (n, n), 1)
    return r >= c, r <= c


def _ada_kernel(c_ref, w_ref, b_ref, o_ref):
    o_ref[0] = _dot_f32(_silu(c_ref[...]), w_ref[0]) + b_ref[0]


def _ada_call(c16, w_ada, b_ada):
    tn = 1536
    return pl.pallas_call(
        _ada_kernel,
        grid=(DEPTH, 6 * D_MODEL // tn),
        in_specs=[pl.BlockSpec((16, D_MODEL), lambda l, j: (0, 0)),
                  pl.BlockSpec((1, D_MODEL, tn), lambda l, j: (l, 0, j)),
                  pl.BlockSpec((1, 1, tn), lambda l, j: (l, 0, j))],
        out_specs=pl.BlockSpec((1, 16, tn), lambda l, j: (l, 0, j)),
        out_shape=jax.ShapeDtypeStruct((DEPTH, 16, 6 * D_MODEL), F32),
        compiler_params=_cparams(2),
        name="adaln",
    )(c16, w_ada, b_ada.reshape(DEPTH, 1, 6 * D_MODEL))


INPROJ_CW = 768


def _inproj_kernel(x_ref, sh_ref, sc_ref, w_ref, b_ref, z_ref):
    tm = x_ref.shape[0]
    x = x_ref[...].reshape(tm // SUBLANES, SUBLANES, D_MODEL)
    h = x * (1.0 + sc_ref[...])[None] + sh_ref[...][None]
    hb = h.reshape(tm, D_MODEL).astype(BF16)
    for j in range(D_IN_PAD // INPROJ_CW):
        cs = slice(j * INPROJ_CW, (j + 1) * INPROJ_CW)
        z_ref[:, cs] = jnp.dot(hb, w_ref[:, cs], preferred_element_type=F32) + b_ref[:, cs]


def _inproj_call(x, mod, w, b):
    n = x.shape[0]
    tm = 256
    return pl.pallas_call(
        _inproj_kernel,
        grid=(n // tm,),
        in_specs=[pl.BlockSpec((tm, D_MODEL), lambda i: (i, 0)),
                  pl.BlockSpec((SUBLANES, D_MODEL), lambda i: (0, 0)),
                  pl.BlockSpec((SUBLANES, D_MODEL), lambda i: (0, 1)),
                  pl.BlockSpec((D_MODEL, D_IN_PAD), lambda i: (0, 0)),
                  pl.BlockSpec((1, D_IN_PAD), lambda i: (0, 0))],
        out_specs=pl.BlockSpec((tm, D_IN_PAD), lambda i: (i, 0)),
        out_shape=jax.ShapeDtypeStruct((n, D_IN_PAD), F32),
        compiler_params=_cparams(1),
        name="inproj",
    )(x, mod, mod, w, b)


def _rglru_kernel(xr_ref, gr_ref, cw_ref, cb_ref, wg_ref, bg_ref, lam_ref, h0_ref,
                  y_ref, hT_ref, xp_ref, of_ref):
    T = xr_ref.shape[0]
    TC = LRU_TC
    nch = T // TC
    pad = jnp.zeros((4, SUBLANES, LANES), F32)
    xp_ref[0:4] = pad
    xp_ref[T + 4:T + 8] = pad

    def copy_in(c, carry):
        t0 = pl.multiple_of(c * TC, TC)
        xp_ref[pl.ds(t0 + 4, TC)] = xr_ref[pl.ds(t0, TC)]
        return carry

    lax.fori_loop(0, nch, copy_in, 0)

    cw = cw_ref[...]
    cb = cb_ref[...]
    sp = _softplus(-lam_ref[0])

    def gates(t0, d):
        xc = (cw[0:1][None] * xp_ref[pl.ds(t0 + 2, TC)] + cw[1:2][None] * xp_ref[pl.ds(t0 + 3, TC)]
              + cw[2:3][None] * xp_ref[pl.ds(t0 + 4, TC)] + cw[3:4][None] * xp_ref[pl.ds(t0 + 5, TC)]
              + cb[None])
        x2 = xc.reshape(TC * SUBLANES, LANES)
        g = (jnp.dot(x2.astype(BF16), wg_ref[0, :, d * 256:(d + 1) * 256], preferred_element_type=F32)
             + bg_ref[0, :, d * 256:(d + 1) * 256])
        r = jax.nn.sigmoid(g[:, :LANES])
        i = jax.nn.sigmoid(g[:, LANES:])
        log_a = -LRU_C * r * sp[:, d * LANES:(d + 1) * LANES]
        a = jnp.exp(log_a)
        u = jnp.sqrt(jnp.maximum(1.0 - jnp.exp(2.0 * log_a), 0.0)) * (i * x2)
        return a.reshape(TC, SUBLANES, LANES), u.reshape(TC, SUBLANES, LANES)

    def fwd(c, h):
        t0 = pl.multiple_of(c * TC, TC)
        a, u = gates(t0, 0)
        hs = []
        for i in range(TC):
            h = a[i] * h + u[i]
            hs.append(h)
        of_ref[pl.ds(t0, TC)] = jnp.stack(hs)
        return h

    hT_ref[0] = lax.fori_loop(0, nch, fwd, h0_ref[0])

    def bwd(c, h):
        t0 = pl.multiple_of((nch - 1 - c) * TC, TC)
        a, u = gates(t0, 1)
        hs = [None] * TC
        for i in range(TC - 1, -1, -1):
            h = a[i] * h + u[i]
            hs[i] = h
        o = of_ref[pl.ds(t0, TC)] + jnp.stack(hs)
        y_ref[pl.ds(t0, TC)] = jax.nn.gelu(gr_ref[pl.ds(t0, TC)]) * o
        return h

    hT_ref[1] = lax.fori_loop(0, nch, bwd, h0_ref[1])


def _rglru_call(z3, h0, cw, cb, wg, bg, lam):
    T, B, _ = z3.shape
    nb = B // SUBLANES
    ncb = D_GROUP // LANES
    blk = (T, SUBLANES, LANES)
    return pl.pallas_call(
        _rglru_kernel,
        grid=(nb, ncb),
        in_specs=[pl.BlockSpec(blk, lambda b, c: (0, b, c)),
                  pl.BlockSpec(blk, lambda b, c: (0, b, ncb + c)),
                  pl.BlockSpec((4, LANES), lambda b, c: (0, c)),
                  pl.BlockSpec((1, LANES), lambda b, c: (0, c)),
                  pl.BlockSpec((1, LANES, 4 * LANES), lambda b, c: (c, 0, 0)),
                  pl.BlockSpec((1, 1, 4 * LANES), lambda b, c: (c, 0, 0)),
                  pl.BlockSpec((1, 1, 2 * LANES), lambda b, c: (c, 0, 0)),
                  pl.BlockSpec((2, SUBLANES, LANES), lambda b, c: (0, b, c))],
        out_specs=[pl.BlockSpec(blk, lambda b, c: (0, b, c)),
                   pl.BlockSpec((2, SUBLANES, LANES), lambda b, c: (0, b, c))],
        out_shape=[jax.ShapeDtypeStruct((T, B, D_GROUP), F32),
                   jax.ShapeDtypeStruct((2, B, D_GROUP), F32)],
        scratch_shapes=[pltpu.VMEM((T + 8, SUBLANES, LANES), F32),
                        pltpu.VMEM((T, SUBLANES, LANES), F32)],
        compiler_params=_cparams(2),
        name="rglru",
    )(z3, z3, cw, cb, wg, bg, lam, h0)


def _hgrn_kernel(q_ref, ff_ref, fb_ref, v_ref, og_ref, lb_ref, ng_ref, s0_ref,
                 y_ref, sT_ref, st_ref, of_ref, ob_ref):
    T = q_ref.shape[0]
    L = HG_L
    SUB = HG_SUB
    nch = T // L
    nsub = L // SUB
    tril, triu = _tri_masks(L)
    heads = [slice(hd * D_HEAD, (hd + 1) * D_HEAD) for hd in range(N_HEADS)]

    for d in range(2):
        for hd in range(N_HEADS):
            st_ref[d, hd] = s0_ref[d, hd].T

    def step(c, carry):
        jobs = ((0, pl.multiple_of(c * L, L), of_ref), (1, pl.multiple_of((nch - 1 - c) * L, L), ob_ref))
        gated = []
        for d, t0, _ in jobs:
            q = _silu(q_ref[pl.ds(t0, L), :])
            v = v_ref[pl.ds(t0, L), :].astype(BF16)
            lb = lb_ref[d:d + 1, :]
            sig = jax.nn.sigmoid((ff_ref if d == 0 else fb_ref)[pl.ds(t0, L), :])
            log_f = jnp.log(jnp.maximum(lb + (1.0 - lb) * sig, TINY))
            kk = (1.0 - lb) * (1.0 - sig)
            mask = tril if d == 0 else triu
            b = _dot_split3(mask, log_f, mask_on_left=True)
            gated.append((q, v, kk, b, mask))
        factored = []
        for (d, _, _), (q, v, kk, b, mask) in zip(jobs, gated):
            b_tot = b[L - 1:L] if d == 0 else b[0:1]
            subs = []
            for i in range(nsub):
                rows = slice(i * SUB, (i + 1) * SUB)
                if d == 0:
                    cols = slice(0, (i + 1) * SUB)
                    edge = b[i * SUB - 1:i * SUB] if i > 0 else 0.0
                else:
                    cols = slice(i * SUB, L)
                    edge = b[(i + 1) * SUB:(i + 1) * SUB + 1] if i < nsub - 1 else 0.0
                subs.append((rows, cols, (q[rows] * jnp.exp(b[rows] - edge)).astype(BF16),
                             (kk[cols] * jnp.exp(edge - b[cols])).astype(BF16)))
            factored.append(((q * jnp.exp(b)).astype(BF16), (kk * jnp.exp(b_tot - b)).astype(BF16),
                             jnp.exp(b_tot), subs))
        scores = [[[_dot_nt(qs[:, hs], ks[:, hs]) for (_, _, qs, ks) in subs] for hs in heads]
                  for (_, _, _, subs) in factored]
        states = [[st_ref[d, hd] for hd in range(N_HEADS)] for d, _, _ in jobs]
        inter = [[_dot_nt(qdec[:, hs], states[j][hd]) for hd, hs in enumerate(heads)]
                 for j, (qdec, _, _, _) in enumerate(factored)]
        update = [[_dot_tn(gated[j][1][:, hs], kdec[:, hs]) for hs in heads]
                  for j, (_, kdec, _, _) in enumerate(factored)]
        for j, (d, t0, o_ref) in enumerate(jobs):
            v, mask = gated[j][1], gated[j][4]
            dec, subs = factored[j][2], factored[j][3]
            for hd, hs in enumerate(heads):
                intra = [_dot(jnp.where(mask[rows, cols], scores[j][hd][i], 0.0), v[cols, hs])
                         for i, (rows, cols, _, _) in enumerate(subs)]
                o_ref[pl.ds(t0, L), hs] = jnp.concatenate(intra, axis=0) + inter[j][hd]
                st_ref[d, hd] = states[j][hd] * dec[:, hs] + update[j][hd]
        return carry

    lax.fori_loop(0, nch, step, 0)

    def combine(c, carry):
        t0 = pl.multiple_of(c * L, L)
        for hs in heads:
            o = of_ref[pl.ds(t0, L), hs] + ob_ref[pl.ds(t0, L), hs]
            o = o * lax.rsqrt(jnp.mean(jnp.square(o), -1, keepdims=True) + EPS)
            y_ref[pl.ds(t0, L), hs] = o * ng_ref[:, hs] * _silu(og_ref[pl.ds(t0, L), hs])
        return carry

    lax.fori_loop(0, nch, combine, 0)

    for d in range(2):
        for hd in range(N_HEADS):
            sT_ref[d, hd] = st_ref[d, hd].T


def _seq_col_spec(T, j):
    return pl.BlockSpec((T, D_GROUP), lambda b, j=j: (0, b * N_COLGRP + j))


def _hgrn_call(z2, B, s0, lower, norm_g):
    T = z2.shape[0]
    col = functools.partial(_seq_col_spec, T)
    st_spec = pl.BlockSpec((None, 2, N_HEADS, D_HEAD, D_HEAD), lambda b: (b, 0, 0, 0, 0))
    return pl.pallas_call(
        _hgrn_kernel,
        grid=(B,),
        in_specs=[col(2), col(3), col(4), col(5), col(6),
                  pl.BlockSpec((2, D_GROUP), lambda b: (0, 0)),
                  pl.BlockSpec((1, D_GROUP), lambda b: (0, 0)),
                  st_spec],
        out_specs=[pl.BlockSpec((T, D_GROUP), lambda b: (0, b)), st_spec],
        out_shape=[jax.ShapeDtypeStruct((T, B * D_GROUP), F32),
                   jax.ShapeDtypeStruct((B, 2, N_HEADS, D_HEAD, D_HEAD), F32)],
        scratch_shapes=[pltpu.VMEM((2, N_HEADS, D_HEAD, D_HEAD), F32),
                        pltpu.VMEM((T, D_GROUP), F32),
                        pltpu.VMEM((T, D_GROUP), F32)],
        compiler_params=_cparams(1),
        name="hgrn2",
    )(z2, z2, z2, z2, z2, lower, norm_g, s0)


def _mlstm_kernel(q_ref, k_ref, v_ref, og_ref, g_ref, ng_ref, c0_ref, n0_ref, m0_ref,
                  y_ref, cT_ref, nT_ref, mT_ref, of_ref, ob_ref):
    T = q_ref.shape[0]
    L = ML_L
    nch = T // L
    tril, triu = _tri_masks(L)
    gcol = lax.broadcasted_iota(jnp.int32, (L, N_GATE), 1)
    grow = lax.broadcasted_iota(jnp.int32, (N_GATE, L), 0)
    heads = [slice(hd * D_HEAD, (hd + 1) * D_HEAD) for hd in range(N_HEADS)]

    cT_ref[...] = c0_ref[...]
    nT_ref[...] = n0_ref[...]
    mT_ref[...] = m0_ref[...]

    def step(c, carry):
        jobs = ((0, pl.multiple_of(c * L, L), of_ref), (1, pl.multiple_of((nch - 1 - c) * L, L), ob_ref))
        gates, qkv, qk, qc = [], [], [], []
        for d, t0, _ in jobs:
            g_all = g_ref[pl.ds(t0, L), :]
            g, gt = g_all[:, 0:N_GATE], g_all.T[0:N_GATE, :]
            g = jnp.where(gcol % 8 >= N_HEADS, jax.nn.log_sigmoid(g), g)
            gt = jnp.where(grow % 8 >= N_HEADS, jax.nn.log_sigmoid(gt), gt)
            bc = _dot_split3(tril if d == 0 else triu, g, mask_on_left=True)
            br = _dot_split3(triu if d == 0 else tril, gt, mask_on_left=False)
            gates.append((g, gt, bc, br))
            q = q_ref[pl.ds(t0, L), :]
            kf = k_ref[pl.ds(t0, L), :] * (D_HEAD ** -0.5)
            qb, kb, vb = q.astype(BF16), kf.astype(BF16), v_ref[pl.ds(t0, L), :].astype(BF16)
            qkv.append((q, kf, qb, vb))
            qk.append([_dot_nt(qb[:, hs], kb[:, hs]) for hs in heads])
            qc.append([_dot(qb[:, hs], cT_ref[d, hd]) for hd, hs in enumerate(heads)])
        weights = []
        for j, (d, _, _) in enumerate(jobs):
            g, gt, bc, br = gates[j]
            q, kf, _, _ = qkv[j]
            mask = tril if d == 0 else triu
            per_head = []
            for hd, hs in enumerate(heads):
                ic, fc = d * 8 + hd, d * 8 + N_HEADS + hd
                li_col, li_row = g[:, ic:ic + 1], gt[ic:ic + 1, :]
                b_col, b_row = bc[:, fc:fc + 1], br[fc:fc + 1, :]
                b_tot = b_col[L - 1:L] if d == 0 else b_col[0:1]
                n_st = nT_ref[d, hd:hd + 1, :]
                m_st = mT_ref[d, hd:hd + 1, 0:1]
                dm = jnp.where(mask, b_col - b_row + li_row, NEG)
                inter = b_col + m_st
                m_t = jnp.maximum(inter, jnp.max(dm, axis=1, keepdims=True))
                s_inter = jnp.exp(inter - m_t)
                w = jnp.exp(dm - m_t) * qk[j][hd]
                den = (jnp.sum(w, axis=1, keepdims=True)
                       + s_inter * jnp.sum(q[:, hs] * n_st, axis=1, keepdims=True))
                scale = 1.0 / jnp.maximum(jnp.abs(den), jnp.exp(-m_t))
                w_state = b_tot - b_col + li_col
                m_loc = jnp.max(w_state, axis=0, keepdims=True)
                ek = jnp.exp(w_state - m_loc) * kf[:, hs]
                m_new = jnp.maximum(b_tot + m_st, m_loc)
                s_old = jnp.exp(b_tot + m_st - m_new)
                s_new = jnp.exp(m_loc - m_new)
                per_head.append((w.astype(BF16), s_inter, scale, ek, n_st, m_new, s_old, s_new))
            weights.append(per_head)
        num = [[_dot(weights[j][hd][0], qkv[j][3][:, hs]) for hd, hs in enumerate(heads)] for j in range(2)]
        d_c = [[_dot_tn(weights[j][hd][3], qkv[j][3][:, hs]) for hd, hs in enumerate(heads)] for j in range(2)]
        for j, (d, t0, o_ref) in enumerate(jobs):
            for hd, hs in enumerate(heads):
                _, s_inter, scale, ek, n_st, m_new, s_old, s_new = weights[j][hd]
                o_ref[pl.ds(t0, L), hs] = (num[j][hd] + s_inter * qc[j][hd]) * scale
                cT_ref[d, hd] = s_old * cT_ref[d, hd] + s_new * d_c[j][hd]
                nT_ref[d, hd:hd + 1, :] = s_old * n_st + s_new * jnp.sum(ek, axis=0, keepdims=True)
                mT_ref[d, hd:hd + 1, :] = jnp.broadcast_to(m_new, (1, LANES))
        return carry

    lax.fori_loop(0, nch, step, 0)

    def combine(c, carry):
        t0 = pl.multiple_of(c * L, L)
        for hs in heads:
            o = of_ref[pl.ds(t0, L), hs] + ob_ref[pl.ds(t0, L), hs]
            mu = jnp.mean(o, -1, keepdims=True)
            var = jnp.mean(jnp.square(o - mu), -1, keepdims=True)
            o = (o - mu) * lax.rsqrt(var + EPS)
            y_ref[pl.ds(t0, L), hs] = jax.nn.sigmoid(og_ref[pl.ds(t0, L), hs]) * (o * ng_ref[:, hs])
        return carry

    lax.fori_loop(0, nch, combine, 0)


def _mlstm_call(z2, B, c0, n0, m0, norm_g):
    T = z2.shape[0]
    col = functools.partial(_seq_col_spec, T)
    c_spec = pl.BlockSpec((None, 2, N_HEADS, D_HEAD, D_HEAD), lambda b: (b, 0, 0, 0, 0))
    v_spec = pl.BlockSpec((None, 2, N_HEADS, D_HEAD), lambda b: (b, 0, 0, 0))
    return pl.pallas_call(
        _mlstm_kernel,
        grid=(B,),
        in_specs=[col(7), col(8), col(9), col(10),
                  pl.BlockSpec((T, LANES), lambda b: (0, b * (D_IN_PAD // LANES) + GATE_BLK)),
                  pl.BlockSpec((1, D_GROUP), lambda b: (0, 0)),
                  c_spec, v_spec, v_spec],
        out_specs=[pl.BlockSpec((T, D_GROUP), lambda b: (0, b)), c_spec, v_spec, v_spec],
        out_shape=[jax.ShapeDtypeStruct((T, B * D_GROUP), F32),
                   jax.ShapeDtypeStruct((B, 2, N_HEADS, D_HEAD, D_HEAD), F32),
                   jax.ShapeDtypeStruct((B, 2, N_HEADS, D_HEAD), F32),
                   jax.ShapeDtypeStruct((B, 2, N_HEADS, D_HEAD), F32)],
        scratch_shapes=[pltpu.VMEM((T, D_GROUP), F32), pltpu.VMEM((T, D_GROUP), F32)],
        compiler_params=_cparams(1),
        name="mlstm",
    )(z2, z2, z2, z2, z2, norm_g, c0, n0, m0)


def _layer_norm(v, g, b):
    mu = jnp.mean(v, -1, keepdims=True)
    var = jnp.mean(jnp.square(v - mu), -1, keepdims=True)
    return (v - mu) * lax.rsqrt(var + EPS) * g + b


def _modulate(v, scale, shift):
    tm = v.shape[0]
    v3 = v.reshape(tm // SUBLANES, SUBLANES, D_MODEL)
    return (v3 * (1.0 + scale)[None] + shift[None]).reshape(tm, D_MODEL)


def _gate_rows(v, gate):
    tm = v.shape[0]
    return (v.reshape(tm // SUBLANES, SUBLANES, D_MODEL) * gate[None]).reshape(tm, D_MODEL)


def _route(logits_t, b_router):
    mx = jnp.max(logits_t, axis=0, keepdims=True)
    ex = jnp.exp(logits_t - mx)
    scores = ex / jnp.sum(ex, axis=0, keepdims=True)
    sel = scores + b_router
    rows = [sel[e:e + 1] for e in range(N_EXPERTS)]
    first, second, grp = [], [], []
    for gi in range(N_GROUPS):
        r = rows[gi * EXPERTS_PER_GROUP:(gi + 1) * EXPERTS_PER_GROUP]
        m1 = functools.reduce(jnp.maximum, r)
        taken = jnp.zeros_like(m1, dtype=jnp.bool_)
        f = []
        for x in r:
            hit = jnp.logical_and(x == m1, jnp.logical_not(taken))
            f.append(hit)
            taken = jnp.logical_or(taken, hit)
        rest = [jnp.where(fi, -jnp.inf, x) for fi, x in zip(f, r)]
        m2 = functools.reduce(jnp.maximum, rest)
        taken = jnp.zeros_like(m1, dtype=jnp.bool_)
        s = []
        for x in rest:
            hit = jnp.logical_and(x == m2, jnp.logical_not(taken))
            s.append(hit)
            taken = jnp.logical_or(taken, hit)
        first.append(f)
        second.append(s)
        grp.append(m1 + m2)
    gmax = functools.reduce(jnp.maximum, grp)
    taken = jnp.zeros_like(gmax, dtype=jnp.bool_)
    chosen = []
    for gi in range(N_GROUPS):
        best = jnp.logical_and(grp[gi] == gmax, jnp.logical_not(taken))
        taken = jnp.logical_or(taken, best)
        for j in range(EXPERTS_PER_GROUP):
            chosen.append(jnp.logical_and(best, jnp.logical_or(first[gi][j], second[gi][j])))
    picked = [jnp.where(ch, scores[e:e + 1], 0.0) for e, ch in enumerate(chosen)]
    denom = functools.reduce(lambda a, b: a + b, picked)
    return jnp.concatenate([p / denom for p in picked], axis=0)


def _outproj_kernel(yr_ref, yh_ref, ym_ref, x_ref, g1_ref, sh2_ref, sc2_ref, w_ref, lg_ref, lb_ref,
                    wr_ref, br_ref, x1_ref, hm_ref, cmb_ref):
    y = (jnp.dot(yr_ref[...].astype(BF16), w_ref[0:D_GROUP], preferred_element_type=F32)
         + jnp.dot(yh_ref[...].astype(BF16), w_ref[D_GROUP:2 * D_GROUP], preferred_element_type=F32)
         + jnp.dot(ym_ref[...].astype(BF16), w_ref[2 * D_GROUP:], preferred_element_type=F32))
    x1 = _layer_norm(ALPHA * x_ref[...] + _gate_rows(y, g1_ref[...]), lg_ref[...], lb_ref[...])
    x1_ref[...] = x1
    hm = _modulate(x1, sc2_ref[...], sh2_ref[...])
    hm_ref[...] = hm.astype(BF16)
    logits_t = lax.dot_general(wr_ref[...], hm, (((1,), (1,)), ((), ())), precision=HIGHEST,
                               preferred_element_type=F32)
    cmb_ref[...] = _route(logits_t, br_ref[...])


def _outproj_call(yr, yh, ym, x, mod, w_out, ln_g, ln_b, w_router_t, b_router):
    n = x.shape[0]
    tm = 256
    row = lambda w: pl.BlockSpec((tm, w), lambda i: (i, 0))
    modspec = lambda j: pl.BlockSpec((SUBLANES, D_MODEL), lambda i, j=j: (0, j))
    full = lambda a: pl.BlockSpec(a.shape, lambda i: (0,) * a.ndim)
    return pl.pallas_call(
        _outproj_kernel,
        grid=(n // tm,),
        in_specs=[row(D_GROUP), row(D_GROUP), row(D_GROUP), row(D_MODEL),
                  modspec(2), modspec(3), modspec(4),
                  full(w_out), full(ln_g), full(ln_b), full(w_router_t), full(b_router)],
        out_specs=[row(D_MODEL), row(D_MODEL), pl.BlockSpec((N_EXPERTS, tm), lambda i: (0, i))],
        out_shape=[jax.ShapeDtypeStruct((n, D_MODEL), F32),
                   jax.ShapeDtypeStruct((n, D_MODEL), BF16),
                   jax.ShapeDtypeStruct((N_EXPERTS, n), F32)],
        compiler_params=_cparams(1),
        name="outproj_ln_router",
    )(yr, yh, ym, x, mod, mod, mod, w_out, ln_g, ln_b, w_router_t, b_router)


def _moe_kernel(hm_ref, cmb_ref, x1_ref, g2_ref, wg_ref, wu_ref, wd_ref, lg_ref, lb_ref, o_ref, acc_ref):
    e = pl.program_id(1)

    @pl.when(e == 0)
    def _():
        acc_ref[...] = jnp.zeros_like(acc_ref)

    hm = hm_ref[...]
    cmb = cmb_ref[...]
    lane = lax.broadcasted_iota(jnp.int32, cmb.shape, 1)
    ce = jnp.sum(jnp.where(lane == e, cmb, 0.0), axis=1, keepdims=True)
    hg = jnp.dot(hm, wg_ref[0], preferred_element_type=F32)
    hu = jnp.dot(hm, wu_ref[0], preferred_element_type=F32)
    act = _silu(hg) * hu * ce
    acc_ref[...] += jnp.dot(act.astype(BF16), wd_ref[0], preferred_element_type=F32)

    @pl.when(e == N_EXPERTS - 1)
    def _():
        v = ALPHA * x1_ref[...] + _gate_rows(acc_ref[...], g2_ref[...])
        o_ref[...] = _layer_norm(v, lg_ref[...], lb_ref[...])


def _moe_call(hm, cmb, x1, mod, w_gate, w_up, w_down, ln_g, ln_b):
    n = x1.shape[0]
    tm = 1024
    return pl.pallas_call(
        _moe_kernel,
        grid=(n // tm, N_EXPERTS),
        in_specs=[pl.BlockSpec((tm, D_MODEL), lambda i, e: (i, 0)),
                  pl.BlockSpec((tm, N_EXPERTS), lambda i, e: (i, 0)),
                  pl.BlockSpec((tm, D_MODEL), lambda i, e: (i, 0)),
                  pl.BlockSpec((SUBLANES, D_MODEL), lambda i, e: (0, 5)),
                  pl.BlockSpec((1, D_MODEL, D_FF), lambda i, e: (e, 0, 0)),
                  pl.BlockSpec((1, D_MODEL, D_FF), lambda i, e: (e, 0, 0)),
                  pl.BlockSpec((1, D_FF, D_MODEL), lambda i, e: (e, 0, 0)),
                  pl.BlockSpec((1, D_MODEL), lambda i, e: (0, 0)),
                  pl.BlockSpec((1, D_MODEL), lambda i, e: (0, 0))],
        out_specs=pl.BlockSpec((tm, D_MODEL), lambda i, e: (i, 0)),
        out_shape=jax.ShapeDtypeStruct((n, D_MODEL), F32),
        scratch_shapes=[pltpu.VMEM((tm, D_MODEL), F32)],
        compiler_params=_cparams(2),
        name="moe_ln",
    )(hm, cmb, x1, mod, w_gate, w_up, w_down, ln_g, ln_b)


def _grid_pos_embed(n_tokens):
    rows = n_tokens // GRID_W
    r = jnp.repeat(jnp.arange(rows, dtype=F32), GRID_W)
    c = jnp.tile(jnp.arange(GRID_W, dtype=F32), rows)
    q = D_MODEL // 4
    freq = jnp.exp(-jnp.log(10000.0) * jnp.arange(q, dtype=F32) / q)
    ar = r[:, None] * freq
    ac = c[:, None] * freq
    return jnp.concatenate([jnp.sin(ar), jnp.cos(ar), jnp.sin(ac), jnp.cos(ac)], axis=-1)


def _lru_gate_params(wa, ba, wx, bx, lam):
    ncb = D_GROUP // LANES

    def dense(w):
        z = jnp.zeros((ncb, LANES, LANES), F32)
        z = z.at[:, :LRU_BLOCK, :LRU_BLOCK].set(w[0::2])
        return z.at[:, LRU_BLOCK:, LRU_BLOCK:].set(w[1::2])

    wg = jnp.concatenate([dense(wa[0]), dense(wx[0]), dense(wa[1]), dense(wx[1])], axis=-1).astype(BF16)
    per_blk = lambda v: v.reshape(ncb, 1, LANES)
    bg = jnp.concatenate([per_blk(ba[0]), per_blk(bx[0]), per_blk(ba[1]), per_blk(bx[1])], axis=-1)
    lm = jnp.concatenate([per_blk(lam[0]), per_blk(lam[1])], axis=-1)
    return wg, bg, lm


def _mixer_states(B, l, states):
    if states is None:
        return (jnp.zeros((2, B, D_GROUP), F32),
                jnp.zeros((B, 2, N_HEADS, D_HEAD, D_HEAD), F32),
                jnp.zeros((B, 2, N_HEADS, D_HEAD, D_HEAD), F32),
                jnp.zeros((B, 2, N_HEADS, D_HEAD), F32),
                jnp.zeros((B, 2, N_HEADS, D_HEAD), F32))
    h, s, c, n, m = states
    return (jnp.swapaxes(h[:, l], 0, 1), s[:, l], c[:, l], n[:, l],
            jnp.broadcast_to(m[:, l][..., None], (B, 2, N_HEADS, D_HEAD)))


def kernel(x_prompt, x_sample, state_lru_h, state_hgrn_S, state_mlstm_C, state_mlstm_n, state_mlstm_m,
           c, c_ctx, w_ada, b_ada, w_in, b_in, conv_w, conv_b, lru_wa, lru_ba, lru_wx, lru_bx, lru_lam,
           hg_lb, hg_norm_g, ml_norm_g, w_out, ln1_g, ln1_b, ln2_g, ln2_b,
           w_router, b_router, w_gate, w_up, w_down):
    Bp, Tp, _ = x_prompt.shape
    Bs, Ts, _ = x_sample.shape
    assert Bs == SUBLANES and Bp % SUBLANES == 0
    assert Tp % max(HG_L, ML_L, LRU_TC) == 0 and Ts % max(HG_L, ML_L, LRU_TC) == 0

    lb_soft = jax.nn.softmax(hg_lb.astype(F32), axis=0)
    hg_lower = jnp.cumsum(lb_soft, axis=0) - lb_soft[0:1]

    c16 = jnp.concatenate([c, c_ctx[None], jnp.zeros((16 - Bs - 1, D_MODEL), F32)], axis=0)
    mod = _ada_call(c16, w_ada, b_ada)

    w_in_p = jnp.pad(w_in, ((0, 0), (0, 0), (0, D_IN_PAD - D_IN))).astype(BF16)
    b_in_p = jnp.pad(b_in, ((0, 0), (0, D_IN_PAD - D_IN))).reshape(DEPTH, 1, D_IN_PAD)
    w_out_b = w_out.astype(BF16)
    w_gate_b, w_up_b, w_down_b = w_gate.astype(BF16), w_up.astype(BF16), w_down.astype(BF16)
    w_router_t = w_router.T
    b_router_c = b_router.reshape(N_EXPERTS, 1)

    xs = x_sample + _grid_pos_embed(Ts).astype(x_sample.dtype)
    streams = [
        dict(x=jnp.swapaxes(x_prompt, 0, 1).reshape(Tp * Bp, D_MODEL), T=Tp, B=Bp, states=None),
        dict(x=jnp.swapaxes(xs, 0, 1).reshape(Ts * Bs, D_MODEL), T=Ts, B=Bs,
             states=(state_lru_h, state_hgrn_S, state_mlstm_C, state_mlstm_n, state_mlstm_m)),
    ]
    finals = []
    for l in range(DEPTH):
        wg, bg, lm = _lru_gate_params(lru_wa[l], lru_ba[l], lru_wx[l], lru_bx[l], lru_lam[l])
        row = lambda v: v[l].reshape(1, -1)
        for si, st in enumerate(streams):
            T, B = st["T"], st["B"]
            mod_s = mod[l, :SUBLANES] if si == 1 else jnp.broadcast_to(mod[l, Bs:Bs + 1], (SUBLANES, 6 * D_MODEL))
            h0, s0, c0, n0, m0 = _mixer_states(B, l, st["states"])
            z = _inproj_call(st["x"], mod_s, w_in_p[l], b_in_p[l])
            z3 = z.reshape(T, B, D_IN_PAD)
            y_r, h_T = _rglru_call(z3, h0, conv_w[l], row(conv_b), wg, bg, lm)
            z2 = z.reshape(T, B * D_IN_PAD)
            y_h, s_T = _hgrn_call(z2, B, s0, hg_lower[l], row(hg_norm_g))
            y_m, c_T, n_T, m_T = _mlstm_call(z2, B, c0, n0, m0, row(ml_norm_g))
            flat = lambda y: y.reshape(T * B, D_GROUP)
            x1, hm, cmb_t = _outproj_call(flat(y_r), flat(y_h), flat(y_m), st["x"], mod_s, w_out_b[l],
                                          row(ln1_g), row(ln1_b), w_router_t, b_router_c)
            st["x"] = _moe_call(hm, cmb_t.T, x1, mod_s, w_gate_b[l], w_up_b[l], w_down_b[l],
                                row(ln2_g), row(ln2_b))
            if si == 0:
                finals.append((jnp.swapaxes(h_T, 0, 1), s_T, c_T, n_T, m_T[..., 0]))
    outs = [jnp.swapaxes(st["x"].reshape(st["T"], st["B"], D_MODEL), 0, 1) for st in streams]
    stack = lambda i: jnp.stack([f[i] for f in finals], axis=1)
    return (outs[0], outs[1], stack(0), stack(1), stack(2), stack(3), stack(4))
```

```python
import functools

import jax
import jax.numpy as jnp
from jax import lax
from jax.experimental import pallas as pl
from jax.experimental.pallas import tpu as pltpu

F32 = jnp.float32
BF16 = jnp.bfloat16
HIGHEST = lax.Precision.HIGHEST

D_MODEL = 1024
DEPTH = 2
GRID_W = 64
D_GROUP = 512
D_MIX = 3 * D_GROUP
LRU_BLOCKS = 8
LRU_BLOCK = D_GROUP // LRU_BLOCKS
LRU_C = 8.0
N_HEADS = 4
D_HEAD = D_GROUP // N_HEADS
N_EXPERTS = 16
N_GROUPS = 4
EXPERTS_PER_GROUP = N_EXPERTS // N_GROUPS
D_FF = 512
ALPHA = (2.0 * DEPTH) ** 0.25
EPS = 1e-5
NEG = -1e30
TINY = 1e-30
N_GATE = 4 * N_HEADS
D_IN = 11 * D_GROUP + N_GATE
D_IN_PAD = 45 * 128
GATE_BLK = (11 * D_GROUP) // 128

SUBLANES = 8
LANES = 128
VMEM_LIMIT = 56 * 1024 * 1024

LRU_TC = 32
LRU_PAD = SUBLANES
HG_L = 64
HG_SUB = 16
ML_L = 64


def _cparams(n_axes):
    return pltpu.CompilerParams(dimension_semantics=("arbitrary",) * n_axes,
                                vmem_limit_bytes=VMEM_LIMIT)


def _dot(a, b):
    return jnp.dot(a.astype(BF16), b.astype(BF16), preferred_element_type=F32)


def _dot_nt(a, b):
    return lax.dot_general(a.astype(BF16), b.astype(BF16), (((1,), (1,)), ((), ())),
                           preferred_element_type=F32)


def _dot_tn(a, b):
    return lax.dot_general(a.astype(BF16), b.astype(BF16), (((0,), (0,)), ((), ())),
                           preferred_element_type=F32)


def _dot_f32(a, b):
    return jnp.dot(a, b, precision=HIGHEST, preferred_element_type=F32)


def _dot_split3(mask, x, mask_on_left):
    m = jnp.where(mask, 1.0, 0.0).astype(BF16)
    hi = x.astype(BF16)
    r1 = x - hi.astype(F32)
    mid = r1.astype(BF16)
    lo = (r1 - mid.astype(F32)).astype(BF16)
    mm = (lambda p: jnp.dot(m, p, preferred_element_type=F32)) if mask_on_left else (
        lambda p: jnp.dot(p, m, preferred_element_type=F32))
    return mm(hi) + mm(mid) + mm(lo)


def _softplus(x):
    return jnp.maximum(x, 0.0) + jnp.log1p(jnp.exp(-jnp.abs(x)))


def _silu(x):
    return x * jax.nn.sigmoid(x)


def _tri_masks(n):
    r = lax.broadcasted_iota(jnp.int32, (n, n), 0)
    c = lax.broadcasted_iota(jnp.int32, (n, n), 1)
    return r >= c, r <= c


def _ada_kernel(c_ref, w_ref, b_ref, o_ref):
    o_ref[0] = _dot_f32(_silu(c_ref[...]), w_ref[0]) + b_ref[0]


def _ada_call(c16, w_ada, b_ada):
    tn = 1536
    return pl.pallas_call(
        _ada_kernel,
        grid=(DEPTH, 6 * D_MODEL // tn),
        in_specs=[pl.BlockSpec((16, D_MODEL), lambda l, j: (0, 0)),
                  pl.BlockSpec((1, D_MODEL, tn), lambda l, j: (l, 0, j)),
                  pl.BlockSpec((1, 1, tn), lambda l, j: (l, 0, j))],
        out_specs=pl.BlockSpec((1, 16, tn), lambda l, j: (l, 0, j)),
        out_shape=jax.ShapeDtypeStruct((DEPTH, 16, 6 * D_MODEL), F32),
        compiler_params=_cparams(2),
        name="adaln",
    )(c16, w_ada, b_ada.reshape(DEPTH, 1, 6 * D_MODEL))


INPROJ_CW = 640
ROW_TILE = 256

SHIFT1, SCALE1, GATE1, SHIFT2, SCALE2, GATE2 = range(6)


def _mod_spec(mod, T, tm, n_grid_axes=1):
    per_seq = mod.shape[0] > 1
    if n_grid_axes == 1:
        return pl.BlockSpec((None, 6, D_MODEL), lambda i: ((i * tm) // T if per_seq else 0, 0, 0))
    return pl.BlockSpec((None, 6, D_MODEL), lambda i, e: ((i * tm) // T if per_seq else 0, 0, 0))


def _inproj_kernel(x_ref, mod_ref, w_ref, b_ref, z_ref):
    m = mod_ref[...]
    hb = (x_ref[...] * (1.0 + m[SCALE1:SCALE1 + 1]) + m[SHIFT1:SHIFT1 + 1]).astype(BF16)
    for j in range(D_IN_PAD // INPROJ_CW):
        cs = slice(j * INPROJ_CW, (j + 1) * INPROJ_CW)
        z_ref[:, cs] = jnp.dot(hb, w_ref[:, cs], preferred_element_type=F32) + b_ref[:, cs]


def _inproj_call(x, T, mod, w, b):
    n = x.shape[0]
    tm = ROW_TILE
    return pl.pallas_call(
        _inproj_kernel,
        grid=(n // tm,),
        in_specs=[pl.BlockSpec((tm, D_MODEL), lambda i: (i, 0)),
                  _mod_spec(mod, T, tm),
                  pl.BlockSpec((D_MODEL, D_IN_PAD), lambda i: (0, 0)),
                  pl.BlockSpec((1, D_IN_PAD), lambda i: (0, 0))],
        out_specs=pl.BlockSpec((tm, D_IN_PAD), lambda i: (i, 0)),
        out_shape=jax.ShapeDtypeStruct((n, D_IN_PAD), F32),
        compiler_params=_cparams(1),
        name="inproj",
    )(x, mod, w, b)


def _rglru_kernel(xr_ref, gr_ref, cw_ref, cb_ref, wg_ref, bg_ref, lam_ref, h0_ref,
                  y_ref, hT_ref, xs_ref, os_ref):
    T = xr_ref.shape[1]
    P = T + LRU_PAD
    TC = LRU_TC
    nch = T // TC
    CP = 256
    zeros = jnp.zeros((LRU_PAD, LANES), F32)
    for s in range(SUBLANES + 1):
        xs_ref[s * P:s * P + LRU_PAD] = zeros
    for s in range(SUBLANES):
        def copy_in(c, carry, s=s):
            r0 = pl.multiple_of(c * CP, CP)
            xs_ref[pl.ds(s * P + LRU_PAD + r0, CP)] = xr_ref[s, pl.ds(r0, CP)]
            return carry
        lax.fori_loop(0, T // CP, copy_in, 0)

    cw = cw_ref[...]
    cb = cb_ref[...]
    sp = _softplus(-lam_ref[0])

    def gates(t0, d):
        win = [xs_ref[pl.ds(t0 + k + LRU_PAD - 2, SUBLANES, stride=P)] for k in range(TC + 3)]
        xc = jnp.stack([cw[0:1] * win[i] + cw[1:2] * win[i + 1] + cw[2:3] * win[i + 2] + cw[3:4] * win[i + 3] + cb
                        for i in range(TC)])
        x2 = xc.reshape(TC * SUBLANES, LANES)
        g = (jnp.dot(x2.astype(BF16), wg_ref[0, :, d * 256:(d + 1) * 256], preferred_element_type=F32)
             + bg_ref[0, :, d * 256:(d + 1) * 256])
        r = jax.nn.sigmoid(g[:, :LANES])
        i = jax.nn.sigmoid(g[:, LANES:])
        log_a = -LRU_C * r * sp[:, d * LANES:(d + 1) * LANES]
        a = jnp.exp(log_a)
        u = jnp.sqrt(jnp.maximum(1.0 - jnp.exp(2.0 * log_a), 0.0)) * (i * x2)
        return a.reshape(TC, SUBLANES, LANES), u.reshape(TC, SUBLANES, LANES)

    def fwd(c, h):
        t0 = pl.multiple_of(c * TC, TC)
        a, u = gates(t0, 0)
        for i in range(TC):
            h = a[i] * h + u[i]
            os_ref[pl.ds(t0 + i, SUBLANES, stride=P)] = h
        return h

    hT_ref[0] = lax.fori_loop(0, nch, fwd, h0_ref[0])

    def bwd(c, h):
        t0 = pl.multiple_of((nch - 1 - c) * TC, TC)
        a, u = gates(t0, 1)
        for i in range(TC - 1, -1, -1):
            h = a[i] * h + u[i]
            os_ref[pl.ds(t0 + i, SUBLANES, stride=P)] = os_ref[pl.ds(t0 + i, SUBLANES, stride=P)] + h
        return h

    hT_ref[1] = lax.fori_loop(0, nch, bwd, h0_ref[1])

    for s in range(SUBLANES):
        def copy_out(c, carry, s=s):
            r0 = pl.multiple_of(c * CP, CP)
            y_ref[s, pl.ds(r0, CP)] = jax.nn.gelu(gr_ref[s, pl.ds(r0, CP)]) * os_ref[pl.ds(s * P + r0, CP)]
            return carry
        lax.fori_loop(0, T // CP, copy_out, 0)


def _rglru_call(z3, h0, cw, cb, wg, bg, lam):
    B, T, _ = z3.shape
    nb = B // SUBLANES
    ncb = D_GROUP // LANES
    blk = (SUBLANES, T, LANES)
    pitch = T + LRU_PAD
    return pl.pallas_call(
        _rglru_kernel,
        grid=(nb, ncb),
        in_specs=[pl.BlockSpec(blk, lambda b, c: (b, 0, c)),
                  pl.BlockSpec(blk, lambda b, c: (b, 0, ncb + c)),
                  pl.BlockSpec((4, LANES), lambda b, c: (0, c)),
                  pl.BlockSpec((1, LANES), lambda b, c: (0, c)),
                  pl.BlockSpec((1, LANES, 4 * LANES), lambda b, c: (c, 0, 0)),
                  pl.BlockSpec((1, 1, 4 * LANES), lambda b, c: (c, 0, 0)),
                  pl.BlockSpec((1, 1, 2 * LANES), lambda b, c: (c, 0, 0)),
                  pl.BlockSpec((2, SUBLANES, LANES), lambda b, c: (0, b, c))],
        out_specs=[pl.BlockSpec(blk, lambda b, c: (b, 0, c)),
                   pl.BlockSpec((2, SUBLANES, LANES), lambda b, c: (0, b, c))],
        out_shape=[jax.ShapeDtypeStruct((B, T, D_GROUP), F32),
                   jax.ShapeDtypeStruct((2, B, D_GROUP), F32)],
        scratch_shapes=[pltpu.VMEM((SUBLANES * pitch + LRU_PAD, LANES), F32),
                        pltpu.VMEM((SUBLANES * pitch, LANES), F32)],
        compiler_params=_cparams(2),
        name="rglru",
    )(z3, z3, cw, cb, wg, bg, lam, h0)


def _hgrn_kernel(q_ref, ff_ref, fb_ref, v_ref, og_ref, lb_ref, ng_ref, s0_ref,
                 y_ref, sT_ref, st_ref, of_ref, ob_ref):
    T = q_ref.shape[0]
    L = HG_L
    SUB = HG_SUB
    nch = T // L
    nsub = L // SUB
    tril, triu = _tri_masks(L)
    heads = [slice(hd * D_HEAD, (hd + 1) * D_HEAD) for hd in range(N_HEADS)]

    for d in range(2):
        for hd in range(N_HEADS):
            st_ref[d, hd] = s0_ref[d, hd].T

    def step(c, carry):
        jobs = ((0, pl.multiple_of(c * L, L), of_ref), (1, pl.multiple_of((nch - 1 - c) * L, L), ob_ref))
        gated = []
        for d, t0, _ in jobs:
            q = _silu(q_ref[pl.ds(t0, L), :])
            v = v_ref[pl.ds(t0, L), :].astype(BF16)
            lb = lb_ref[d:d + 1, :]
            sig = jax.nn.sigmoid((ff_ref if d == 0 else fb_ref)[pl.ds(t0, L), :])
            log_f = jnp.log(jnp.maximum(lb + (1.0 - lb) * sig, TINY))
            kk = (1.0 - lb) * (1.0 - sig)
            mask = tril if d == 0 else triu
            b = _dot_split3(mask, log_f, mask_on_left=True)
            gated.append((q, v, kk, b, mask))
        factored = []
        for (d, _, _), (q, v, kk, b, mask) in zip(jobs, gated):
            b_tot = b[L - 1:L] if d == 0 else b[0:1]
            subs = []
            for i in range(nsub):
                rows = slice(i * SUB, (i + 1) * SUB)
                if d == 0:
                    cols = slice(0, (i + 1) * SUB)
                    edge = b[i * SUB - 1:i * SUB] if i > 0 else 0.0
                else:
                    cols = slice(i * SUB, L)
                    edge = b[(i + 1) * SUB:(i + 1) * SUB + 1] if i < nsub - 1 else 0.0
                subs.append((rows, cols, (q[rows] * jnp.exp(b[rows] - edge)).astype(BF16),
                             (kk[cols] * jnp.exp(edge - b[cols])).astype(BF16)))
            factored.append(((q * jnp.exp(b)).astype(BF16), (kk * jnp.exp(b_tot - b)).astype(BF16),
                             jnp.exp(b_tot), subs))
        scores = [[[_dot_nt(qs[:, hs], ks[:, hs]) for (_, _, qs, ks) in subs] for hs in heads]
                  for (_, _, _, subs) in factored]
        states = [[st_ref[d, hd] for hd in range(N_HEADS)] for d, _, _ in jobs]
        inter = [[_dot_nt(qdec[:, hs], states[j][hd]) for hd, hs in enumerate(heads)]
                 for j, (qdec, _, _, _) in enumerate(factored)]
        update = [[_dot_tn(gated[j][1][:, hs], kdec[:, hs]) for hs in heads]
                  for j, (_, kdec, _, _) in enumerate(factored)]
        for j, (d, t0, o_ref) in enumerate(jobs):
            v, mask = gated[j][1], gated[j][4]
            dec, subs = factored[j][2], factored[j][3]
            for hd, hs in enumerate(heads):
                intra = [_dot(jnp.where(mask[rows, cols], scores[j][hd][i], 0.0), v[cols, hs])
                         for i, (rows, cols, _, _) in enumerate(subs)]
                o_ref[pl.ds(t0, L), hs] = jnp.concatenate(intra, axis=0) + inter[j][hd]
                st_ref[d, hd] = states[j][hd] * dec[:, hs] + update[j][hd]
        return carry

    lax.fori_loop(0, nch, step, 0)

    def combine(c, carry):
        t0 = pl.multiple_of(c * L, L)
        for hs in heads:
            o = of_ref[pl.ds(t0, L), hs] + ob_ref[pl.ds(t0, L), hs]
            o = o * lax.rsqrt(jnp.mean(jnp.square(o), -1, keepdims=True) + EPS)
            y_ref[pl.ds(t0, L), hs] = o * ng_ref[:, hs] * _silu(og_ref[pl.ds(t0, L), hs])
        return carry

    lax.fori_loop(0, nch, combine, 0)

    for d in range(2):
        for hd in range(N_HEADS):
            sT_ref[d, hd] = st_ref[d, hd].T


def _seq_col_spec(T, j):
    return pl.BlockSpec((T, D_GROUP), lambda b, j=j: (b, j))


def _hgrn_call(z2, B, s0, lower, norm_g):
    T = z2.shape[0] // B
    col = functools.partial(_seq_col_spec, T)
    st_spec = pl.BlockSpec((None, 2, N_HEADS, D_HEAD, D_HEAD), lambda b: (b, 0, 0, 0, 0))
    return pl.pallas_call(
        _hgrn_kernel,
        grid=(B,),
        in_specs=[col(2), col(3), col(4), col(5), col(6),
                  pl.BlockSpec((2, D_GROUP), lambda b: (0, 0)),
                  pl.BlockSpec((1, D_GROUP), lambda b: (0, 0)),
                  st_spec],
        out_specs=[pl.BlockSpec((T, D_GROUP), lambda b: (b, 0)), st_spec],
        out_shape=[jax.ShapeDtypeStruct((B * T, D_GROUP), F32),
                   jax.ShapeDtypeStruct((B, 2, N_HEADS, D_HEAD, D_HEAD), F32)],
        scratch_shapes=[pltpu.VMEM((2, N_HEADS, D_HEAD, D_HEAD), F32),
                        pltpu.VMEM((T, D_GROUP), F32),
                        pltpu.VMEM((T, D_GROUP), F32)],
        compiler_params=_cparams(1),
        name="hgrn2",
    )(z2, z2, z2, z2, z2, lower, norm_g, s0)


def _mlstm_kernel(q_ref, k_ref, v_ref, og_ref, g_ref, ng_ref, c0_ref, n0_ref, m0_ref,
                  y_ref, cT_ref, nT_ref, mT_ref, of_ref, ob_ref):
    T = q_ref.shape[0]
    L = ML_L
    nch = T // L
    tril, triu = _tri_masks(L)
    gcol = lax.broadcasted_iota(jnp.int32, (L, N_GATE), 1)
    grow = lax.broadcasted_iota(jnp.int32, (N_GATE, L), 0)
    heads = [slice(hd * D_HEAD, (hd + 1) * D_HEAD) for hd in range(N_HEADS)]

    cT_ref[...] = c0_ref[...]
    nT_ref[...] = n0_ref[...]
    mT_ref[...] = m0_ref[...]

    def step(c, carry):
        jobs = ((0, pl.multiple_of(c * L, L), of_ref), (1, pl.multiple_of((nch - 1 - c) * L, L), ob_ref))
        gates, qkv, qk, qc = [], [], [], []
        for d, t0, _ in jobs:
            g_all = g_ref[pl.ds(t0, L), :]
            g, gt = g_all[:, 0:N_GATE], g_all.T[0:N_GATE, :]
            g = jnp.where(gcol % 8 >= N_HEADS, jax.nn.log_sigmoid(g), g)
            gt = jnp.where(grow % 8 >= N_HEADS, jax.nn.log_sigmoid(gt), gt)
            bc = _dot_split3(tril if d == 0 else triu, g, mask_on_left=True)
            br = _dot_split3(triu if d == 0 else tril, gt, mask_on_left=False)
            gates.append((g, gt, bc, br))
            q = q_ref[pl.ds(t0, L), :]
            kf = k_ref[pl.ds(t0, L), :] * (D_HEAD ** -0.5)
            qb, kb, vb = q.astype(BF16), kf.astype(BF16), v_ref[pl.ds(t0, L), :].astype(BF16)
            qkv.append((q, kf, qb, vb))
            qk.append([_dot_nt(qb[:, hs], kb[:, hs]) for hs in heads])
            qc.append([_dot(qb[:, hs], cT_ref[d, hd]) for hd, hs in enumerate(heads)])
        weights = []
        for j, (d, _, _) in enumerate(jobs):
            g, gt, bc, br = gates[j]
            q, kf, _, _ = qkv[j]
            mask = tril if d == 0 else triu
            per_head = []
            for hd, hs in enumerate(heads):
                ic, fc = d * 8 + hd, d * 8 + N_HEADS + hd
                li_col, li_row = g[:, ic:ic + 1], gt[ic:ic + 1, :]
                b_col, b_row = bc[:, fc:fc + 1], br[fc:fc + 1, :]
                b_tot = b_col[L - 1:L] if d == 0 else b_col[0:1]
                n_st = nT_ref[d, hd:hd + 1, :]
                m_st = mT_ref[d, hd:hd + 1, 0:1]
                dm = jnp.where(mask, b_col - b_row + li_row, NEG)
                inter = b_col + m_st
                m_t = jnp.maximum(inter, jnp.max(dm, axis=1, keepdims=True))
                s_inter = jnp.exp(inter - m_t)
                w = jnp.exp(dm - m_t) * qk[j][hd]
                den = (jnp.sum(w, axis=1, keepdims=True)
                       + s_inter * jnp.sum(q[:, hs] * n_st, axis=1, keepdims=True))
                scale = 1.0 / jnp.maximum(jnp.abs(den), jnp.exp(-m_t))
                w_state = b_tot - b_col + li_col
                m_loc = jnp.max(w_state, axis=0, keepdims=True)
                ek = jnp.exp(w_state - m_loc) * kf[:, hs]
                m_new = jnp.maximum(b_tot + m_st, m_loc)
                s_old = jnp.exp(b_tot + m_st - m_new)
                s_new = jnp.exp(m_loc - m_new)
                per_head.append((w.astype(BF16), s_inter, scale, ek, n_st, m_new, s_old, s_new))
            weights.append(per_head)
        num = [[_dot(weights[j][hd][0], qkv[j][3][:, hs]) for hd, hs in enumerate(heads)] for j in range(2)]
        d_c = [[_dot_tn(weights[j][hd][3], qkv[j][3][:, hs]) for hd, hs in enumerate(heads)] for j in range(2)]
        for j, (d, t0, o_ref) in enumerate(jobs):
            for hd, hs in enumerate(heads):
                _, s_inter, scale, ek, n_st, m_new, s_old, s_new = weights[j][hd]
                o_ref[pl.ds(t0, L), hs] = (num[j][hd] + s_inter * qc[j][hd]) * scale
                cT_ref[d, hd] = s_old * cT_ref[d, hd] + s_new * d_c[j][hd]
                nT_ref[d, hd:hd + 1, :] = s_old * n_st + s_new * jnp.sum(ek, axis=0, keepdims=True)
                mT_ref[d, hd:hd + 1, :] = jnp.broadcast_to(m_new, (1, LANES))
        return carry

    lax.fori_loop(0, nch, step, 0)

    def combine(c, carry):
        t0 = pl.multiple_of(c * L, L)
        for hs in heads:
            o = of_ref[pl.ds(t0, L), hs] + ob_ref[pl.ds(t0, L), hs]
            mu = jnp.mean(o, -1, keepdims=True)
            var = jnp.mean(jnp.square(o - mu), -1, keepdims=True)
            o = (o - mu) * lax.rsqrt(var + EPS)
            y_ref[pl.ds(t0, L), hs] = jax.nn.sigmoid(og_ref[pl.ds(t0, L), hs]) * (o * ng_ref[:, hs])
        return carry

    lax.fori_loop(0, nch, combine, 0)


def _mlstm_call(z2, B, c0, n0, m0, norm_g):
    T = z2.shape[0] // B
    col = functools.partial(_seq_col_spec, T)
    c_spec = pl.BlockSpec((None, 2, N_HEADS, D_HEAD, D_HEAD), lambda b: (b, 0, 0, 0, 0))
    v_spec = pl.BlockSpec((None, 2, N_HEADS, D_HEAD), lambda b: (b, 0, 0, 0))
    return pl.pallas_call(
        _mlstm_kernel,
        grid=(B,),
        in_specs=[col(7), col(8), col(9), col(10),
                  pl.BlockSpec((T, LANES), lambda b: (b, GATE_BLK)),
                  pl.BlockSpec((1, D_GROUP), lambda b: (0, 0)),
                  c_spec, v_spec, v_spec],
        out_specs=[pl.BlockSpec((T, D_GROUP), lambda b: (b, 0)), c_spec, v_spec, v_spec],
        out_shape=[jax.ShapeDtypeStruct((B * T, D_GROUP), F32),
                   jax.ShapeDtypeStruct((B, 2, N_HEADS, D_HEAD, D_HEAD), F32),
                   jax.ShapeDtypeStruct((B, 2, N_HEADS, D_HEAD), F32),
                   jax.ShapeDtypeStruct((B, 2, N_HEADS, D_HEAD), F32)],
        scratch_shapes=[pltpu.VMEM((T, D_GROUP), F32), pltpu.VMEM((T, D_GROUP), F32)],
        compiler_params=_cparams(1),
        name="mlstm",
    )(z2, z2, z2, z2, z2, norm_g, c0, n0, m0)


def _layer_norm(v, g, b):
    mu = jnp.mean(v, -1, keepdims=True)
    var = jnp.mean(jnp.square(v - mu), -1, keepdims=True)
    return (v - mu) * lax.rsqrt(var + EPS) * g + b


def _route(logits_t, b_router):
    mx = jnp.max(logits_t, axis=0, keepdims=True)
    ex = jnp.exp(logits_t - mx)
    scores = ex / jnp.sum(ex, axis=0, keepdims=True)
    sel = scores + b_router
    rows = [sel[e:e + 1] for e in range(N_EXPERTS)]
    first, second, grp = [], [], []
    for gi in range(N_GROUPS):
        r = rows[gi * EXPERTS_PER_GROUP:(gi + 1) * EXPERTS_PER_GROUP]
        m1 = functools.reduce(jnp.maximum, r)
        taken = jnp.zeros_like(m1, dtype=jnp.bool_)
        f = []
        for x in r:
            hit = jnp.logical_and(x == m1, jnp.logical_not(taken))
            f.append(hit)
            taken = jnp.logical_or(taken, hit)
        rest = [jnp.where(fi, -jnp.inf, x) for fi, x in zip(f, r)]
        m2 = functools.reduce(jnp.maximum, rest)
        taken = jnp.zeros_like(m1, dtype=jnp.bool_)
        s = []
        for x in rest:
            hit = jnp.logical_and(x == m2, jnp.logical_not(taken))
            s.append(hit)
            taken = jnp.logical_or(taken, hit)
        first.append(f)
        second.append(s)
        grp.append(m1 + m2)
    gmax = functools.reduce(jnp.maximum, grp)
    taken = jnp.zeros_like(gmax, dtype=jnp.bool_)
    chosen = []
    for gi in range(N_GROUPS):
        best = jnp.logical_and(grp[gi] == gmax, jnp.logical_not(taken))
        taken = jnp.logical_or(taken, best)
        for j in range(EXPERTS_PER_GROUP):
            chosen.append(jnp.logical_and(best, jnp.logical_or(first[gi][j], second[gi][j])))
    picked = [jnp.where(ch, scores[e:e + 1], 0.0) for e, ch in enumerate(chosen)]
    denom = functools.reduce(lambda a, b: a + b, picked)
    return jnp.concatenate([p / denom for p in picked], axis=0)


def _outproj_kernel(yr_ref, yh_ref, ym_ref, x_ref, mod_ref, w_ref, lg_ref, lb_ref,
                    wr_ref, br_ref, x1_ref, hm_ref, cmb_ref):
    m = mod_ref[...]
    y = (jnp.dot(yr_ref[...].astype(BF16), w_ref[0:D_GROUP], preferred_element_type=F32)
         + jnp.dot(yh_ref[...].astype(BF16), w_ref[D_GROUP:2 * D_GROUP], preferred_element_type=F32)
         + jnp.dot(ym_ref[...].astype(BF16), w_ref[2 * D_GROUP:], preferred_element_type=F32))
    x1 = _layer_norm(ALPHA * x_ref[...] + m[GATE1:GATE1 + 1] * y, lg_ref[...], lb_ref[...])
    x1_ref[...] = x1
    hm = x1 * (1.0 + m[SCALE2:SCALE2 + 1]) + m[SHIFT2:SHIFT2 + 1]
    hm_ref[...] = hm.astype(BF16)
    logits_t = lax.dot_general(wr_ref[...], hm, (((1,), (1,)), ((), ())), precision=HIGHEST,
                               preferred_element_type=F32)
    cmb_ref[...] = _route(logits_t, br_ref[...])


def _outproj_call(yr, yh, ym, x, T, mod, w_out, ln_g, ln_b, w_router_t, b_router):
    n = x.shape[0]
    tm = ROW_TILE
    row = lambda w: pl.BlockSpec((tm, w), lambda i: (i, 0))
    full = lambda a: pl.BlockSpec(a.shape, lambda i: (0,) * a.ndim)
    return pl.pallas_call(
        _outproj_kernel,
        grid=(n // tm,),
        in_specs=[row(D_GROUP), row(D_GROUP), row(D_GROUP), row(D_MODEL), _mod_spec(mod, T, tm),
                  full(w_out), full(ln_g), full(ln_b), full(w_router_t), full(b_router)],
        out_specs=[row(D_MODEL), row(D_MODEL), pl.BlockSpec((N_EXPERTS, tm), lambda i: (0, i))],
        out_shape=[jax.ShapeDtypeStruct((n, D_MODEL), F32),
                   jax.ShapeDtypeStruct((n, D_MODEL), BF16),
                   jax.ShapeDtypeStruct((N_EXPERTS, n), F32)],
        compiler_params=_cparams(1),
        name="outproj_ln_router",
    )(yr, yh, ym, x, mod, w_out, ln_g, ln_b, w_router_t, b_router)


def _moe_kernel(hm_ref, cmb_ref, x1_ref, mod_ref, wg_ref, wu_ref, wd_ref, lg_ref, lb_ref, o_ref, acc_ref):
    e = pl.program_id(1)

    @pl.when(e == 0)
    def _():
        acc_ref[...] = jnp.zeros_like(acc_ref)

    hm = hm_ref[...]
    cmb = cmb_ref[...]
    lane = lax.broadcasted_iota(jnp.int32, cmb.shape, 1)
    ce = jnp.sum(jnp.where(lane == e, cmb, 0.0), axis=1, keepdims=True)
    hg = jnp.dot(hm, wg_ref[0], preferred_element_type=F32)
    hu = jnp.dot(hm, wu_ref[0], preferred_element_type=F32)
    act = _silu(hg) * hu * ce
    acc_ref[...] += jnp.dot(act.astype(BF16), wd_ref[0], preferred_element_type=F32)

    @pl.when(e == N_EXPERTS - 1)
    def _():
        v = ALPHA * x1_ref[...] + mod_ref[GATE2:GATE2 + 1, :] * acc_ref[...]
        o_ref[...] = _layer_norm(v, lg_ref[...], lb_ref[...])


def _moe_call(hm, cmb, x1, T, mod, w_gate, w_up, w_down, ln_g, ln_b):
    n = x1.shape[0]
    tm = 1024 if mod.shape[0] == 1 else min(1024, T)
    return pl.pallas_call(
        _moe_kernel,
        grid=(n // tm, N_EXPERTS),
        in_specs=[pl.BlockSpec((tm, D_MODEL), lambda i, e: (i, 0)),
                  pl.BlockSpec((tm, N_EXPERTS), lambda i, e: (i, 0)),
                  pl.BlockSpec((tm, D_MODEL), lambda i, e: (i, 0)),
                  _mod_spec(mod, T, tm, n_grid_axes=2),
                  pl.BlockSpec((1, D_MODEL, D_FF), lambda i, e: (e, 0, 0)),
                  pl.BlockSpec((1, D_MODEL, D_FF), lambda i, e: (e, 0, 0)),
                  pl.BlockSpec((1, D_FF, D_MODEL), lambda i, e: (e, 0, 0)),
                  pl.BlockSpec((1, D_MODEL), lambda i, e: (0, 0)),
                  pl.BlockSpec((1, D_MODEL), lambda i, e: (0, 0))],
        out_specs=pl.BlockSpec((tm, D_MODEL), lambda i, e: (i, 0)),
        out_shape=jax.ShapeDtypeStruct((n, D_MODEL), F32),
        scratch_shapes=[pltpu.VMEM((tm, D_MODEL), F32)],
        compiler_params=_cparams(2),
        name="moe_ln",
    )(hm, cmb, x1, mod, w_gate, w_up, w_down, ln_g, ln_b)


def _grid_pos_embed(n_tokens):
    rows = n_tokens // GRID_W
    r = jnp.repeat(jnp.arange(rows, dtype=F32), GRID_W)
    c = jnp.tile(jnp.arange(GRID_W, dtype=F32), rows)
    q = D_MODEL // 4
    freq = jnp.exp(-jnp.log(10000.0) * jnp.arange(q, dtype=F32) / q)
    ar = r[:, None] * freq
    ac = c[:, None] * freq
    return jnp.concatenate([jnp.sin(ar), jnp.cos(ar), jnp.sin(ac), jnp.cos(ac)], axis=-1)


def _lru_gate_params(wa, ba, wx, bx, lam):
    ncb = D_GROUP // LANES

    def dense(w):
        z = jnp.zeros((ncb, LANES, LANES), F32)
        z = z.at[:, :LRU_BLOCK, :LRU_BLOCK].set(w[0::2])
        return z.at[:, LRU_BLOCK:, LRU_BLOCK:].set(w[1::2])

    wg = jnp.concatenate([dense(wa[0]), dense(wx[0]), dense(wa[1]), dense(wx[1])], axis=-1).astype(BF16)
    per_blk = lambda v: v.reshape(ncb, 1, LANES)
    bg = jnp.concatenate([per_blk(ba[0]), per_blk(bx[0]), per_blk(ba[1]), per_blk(bx[1])], axis=-1)
    lm = jnp.concatenate([per_blk(lam[0]), per_blk(lam[1])], axis=-1)
    return wg, bg, lm


def _mixer_states(B, l, states):
    if states is None:
        return (jnp.zeros((2, B, D_GROUP), F32),
                jnp.zeros((B, 2, N_HEADS, D_HEAD, D_HEAD), F32),
                jnp.zeros((B, 2, N_HEADS, D_HEAD, D_HEAD), F32),
                jnp.zeros((B, 2, N_HEADS, D_HEAD), F32),
                jnp.zeros((B, 2, N_HEADS, D_HEAD), F32))
    h, s, c, n, m = states
    return (jnp.swapaxes(h[:, l], 0, 1), s[:, l], c[:, l], n[:, l],
            jnp.broadcast_to(m[:, l][..., None], (B, 2, N_HEADS, D_HEAD)))


def kernel(x_prompt, x_sample, state_lru_h, state_hgrn_S, state_mlstm_C, state_mlstm_n, state_mlstm_m,
           c, c_ctx, w_ada, b_ada, w_in, b_in, conv_w, conv_b, lru_wa, lru_ba, lru_wx, lru_bx, lru_lam,
           hg_lb, hg_norm_g, ml_norm_g, w_out, ln1_g, ln1_b, ln2_g, ln2_b,
           w_router, b_router, w_gate, w_up, w_down):
    Bp, Tp, _ = x_prompt.shape
    Bs, Ts, _ = x_sample.shape
    assert Bs % SUBLANES == 0 and Bp % SUBLANES == 0 and Bs + 1 <= 16
    assert Tp % ROW_TILE == 0 and Ts % ROW_TILE == 0

    lb_soft = jax.nn.softmax(hg_lb.astype(F32), axis=0)
    hg_lower = jnp.cumsum(lb_soft, axis=0) - lb_soft[0:1]

    c16 = jnp.concatenate([c, c_ctx[None], jnp.zeros((16 - Bs - 1, D_MODEL), F32)], axis=0)
    mod = _ada_call(c16, w_ada, b_ada)

    w_in_p = jnp.pad(w_in, ((0, 0), (0, 0), (0, D_IN_PAD - D_IN))).astype(BF16)
    b_in_p = jnp.pad(b_in, ((0, 0), (0, D_IN_PAD - D_IN))).reshape(DEPTH, 1, D_IN_PAD)
    w_out_b = w_out.astype(BF16)
    w_gate_b, w_up_b, w_down_b = w_gate.astype(BF16), w_up.astype(BF16), w_down.astype(BF16)
    w_router_t = w_router.T
    b_router_c = b_router.reshape(N_EXPERTS, 1)

    xs = x_sample + _grid_pos_embed(Ts).astype(x_sample.dtype)
    streams = [
        dict(x=x_prompt.reshape(Bp * Tp, D_MODEL), T=Tp, B=Bp, states=None),
        dict(x=xs.reshape(Bs * Ts, D_MODEL), T=Ts, B=Bs,
             states=(state_lru_h, state_hgrn_S, state_mlstm_C, state_mlstm_n, state_mlstm_m)),
    ]
    finals = []
    for l in range(DEPTH):
        wg, bg, lm = _lru_gate_params(lru_wa[l], lru_ba[l], lru_wx[l], lru_bx[l], lru_lam[l])
        row = lambda v: v[l].reshape(1, -1)
        for si, st in enumerate(streams):
            T, B = st["T"], st["B"]
            mod_s = (mod[l, :Bs] if si == 1 else mod[l, Bs:Bs + 1]).reshape(-1, 6, D_MODEL)
            h0, s0, c0, n0, m0 = _mixer_states(B, l, st["states"])
            z = _inproj_call(st["x"], T, mod_s, w_in_p[l], b_in_p[l])
            y_r, h_T = _rglru_call(z.reshape(B, T, D_IN_PAD), h0, conv_w[l], row(conv_b), wg, bg, lm)
            y_h, s_T = _hgrn_call(z, B, s0, hg_lower[l], row(hg_norm_g))
            y_m, c_T, n_T, m_T = _mlstm_call(z, B, c0, n0, m0, row(ml_norm_g))
            x1, hm, cmb_t = _outproj_call(y_r.reshape(B * T, D_GROUP), y_h, y_m, st["x"], T, mod_s, w_out_b[l],
                                          row(ln1_g), row(ln1_b), w_router_t, b_router_c)
            st["x"] = _moe_call(hm, cmb_t.T, x1, T, mod_s, w_gate_b[l], w_up_b[l], w_down_b[l],
                                row(ln2_g), row(ln2_b))
            if si == 0:
                finals.append((jnp.swapaxes(h_T, 0, 1), s_T, c_T, n_T, m_T[..., 0]))
    outs = [st["x"].reshape(st["B"], st["T"], D_MODEL) for st in streams]
    stack = lambda i: jnp.stack([f[i] for f in finals], axis=1)
    return (outs[0], outs[1], stack(0), stack(1), stack(2), stack(3), stack(4))
```

```python
import functools

import jax
import jax.numpy as jnp
from jax import lax
from jax.experimental import pallas as pl
from jax.experimental.pallas import tpu as pltpu

F32 = jnp.float32
BF16 = jnp.bfloat16
HIGHEST = lax.Precision.HIGHEST

D_MODEL = 1024
DEPTH = 2
GRID_W = 64
D_GROUP = 512
D_MIX = 3 * D_GROUP
LRU_BLOCKS = 8
LRU_BLOCK = D_GROUP // LRU_BLOCKS
LRU_C = 8.0
N_HEADS = 4
D_HEAD = D_GROUP // N_HEADS
N_EXPERTS = 16
N_GROUPS = 4
EXPERTS_PER_GROUP = N_EXPERTS // N_GROUPS
D_FF = 512
ALPHA = (2.0 * DEPTH) ** 0.25
EPS = 1e-5
NEG = -1e30
TINY = 1e-30
N_GATE = 4 * N_HEADS
D_IN = 11 * D_GROUP + N_GATE
D_IN_PAD = 45 * 128
GATE_BLK = (11 * D_GROUP) // 128

SUBLANES = 8
LANES = 128
VMEM_LIMIT = 56 * 1024 * 1024

LRU_TC = 32
LRU_PAD = SUBLANES
HG_L = 64
HG_SUB = 16
ML_L = 64


def _cparams(n_axes):
    return pltpu.CompilerParams(dimension_semantics=("arbitrary",) * n_axes,
                                vmem_limit_bytes=VMEM_LIMIT)


def _dot(a, b):
    return jnp.dot(a.astype(BF16), b.astype(BF16), preferred_element_type=F32)


def _dot_nt(a, b):
    return lax.dot_general(a.astype(BF16), b.astype(BF16), (((1,), (1,)), ((), ())),
                           preferred_element_type=F32)


def _dot_tn(a, b):
    return lax.dot_general(a.astype(BF16), b.astype(BF16), (((0,), (0,)), ((), ())),
                           preferred_element_type=F32)


def _dot_f32(a, b):
    return jnp.dot(a, b, precision=HIGHEST, preferred_element_type=F32)


def _dot_split3(mask, x, mask_on_left):
    m = jnp.where(mask, 1.0, 0.0).astype(BF16)
    hi = x.astype(BF16)
    r1 = x - hi.astype(F32)
    mid = r1.astype(BF16)
    lo = (r1 - mid.astype(F32)).astype(BF16)
    mm = (lambda p: jnp.dot(m, p, preferred_element_type=F32)) if mask_on_left else (
        lambda p: jnp.dot(p, m, preferred_element_type=F32))
    return mm(hi) + mm(mid) + mm(lo)


def _softplus(x):
    return jnp.maximum(x, 0.0) + jnp.log1p(jnp.exp(-jnp.abs(x)))


def _silu(x):
    return x * jax.nn.sigmoid(x)


def _tri_masks(n):
    r = lax.broadcasted_iota(jnp.int32, (n, n), 0)
    c = lax.broadcasted_iota(jnp.int32, (n, n), 1)
    return r >= c, r <= c


def _ada_kernel(c_ref, w_ref, b_ref, o_ref):
    o_ref[0] = _dot_f32(_silu(c_ref[...]), w_ref[0]) + b_ref[0]


def _ada_call(c16, w_ada, b_ada):
    tn = 1536
    return pl.pallas_call(
        _ada_kernel,
        grid=(DEPTH, 6 * D_MODEL // tn),
        in_specs=[pl.BlockSpec((16, D_MODEL), lambda l, j: (0, 0)),
                  pl.BlockSpec((1, D_MODEL, tn), lambda l, j: (l, 0, j)),
                  pl.BlockSpec((1, 1, tn), lambda l, j: (l, 0, j))],
        out_specs=pl.BlockSpec((1, 16, tn), lambda l, j: (l, 0, j)),
        out_shape=jax.ShapeDtypeStruct((DEPTH, 16, 6 * D_MODEL), F32),
        compiler_params=_cparams(2),
        name="adaln",
    )(c16, w_ada, b_ada.reshape(DEPTH, 1, 6 * D_MODEL))


INPROJ_CW = 640
ROW_TILE = 256

SHIFT1, SCALE1, GATE1, SHIFT2, SCALE2, GATE2 = range(6)


def _mod_spec(mod, T, tm, n_grid_axes=1):
    per_seq = mod.shape[0] > 1
    if n_grid_axes == 1:
        return pl.BlockSpec((None, 6, D_MODEL), lambda i: ((i * tm) // T if per_seq else 0, 0, 0))
    return pl.BlockSpec((None, 6, D_MODEL), lambda i, e: ((i * tm) // T if per_seq else 0, 0, 0))


def _inproj_kernel(x_ref, mod_ref, w_ref, b_ref, z_ref):
    m = mod_ref[...]
    hb = (x_ref[...] * (1.0 + m[SCALE1:SCALE1 + 1]) + m[SHIFT1:SHIFT1 + 1]).astype(BF16)
    for j in range(D_IN_PAD // INPROJ_CW):
        cs = slice(j * INPROJ_CW, (j + 1) * INPROJ_CW)
        z_ref[:, cs] = jnp.dot(hb, w_ref[:, cs], preferred_element_type=F32) + b_ref[:, cs]


def _inproj_call(x, T, mod, w, b):
    n = x.shape[0]
    tm = ROW_TILE
    return pl.pallas_call(
        _inproj_kernel,
        grid=(n // tm,),
        in_specs=[pl.BlockSpec((tm, D_MODEL), lambda i: (i, 0)),
                  _mod_spec(mod, T, tm),
                  pl.BlockSpec((D_MODEL, D_IN_PAD), lambda i: (0, 0)),
                  pl.BlockSpec((1, D_IN_PAD), lambda i: (0, 0))],
        out_specs=pl.BlockSpec((tm, D_IN_PAD), lambda i: (i, 0)),
        out_shape=jax.ShapeDtypeStruct((n, D_IN_PAD), F32),
        compiler_params=_cparams(1),
        name="inproj",
    )(x, mod, w, b)


def _rglru_kernel(xr_ref, gr_ref, cw_ref, cb_ref, wg_ref, bg_ref, lam_ref, h0_ref,
                  y_ref, hT_ref, xs_ref, os_ref):
    T = xr_ref.shape[1]
    P = T + LRU_PAD
    TC = LRU_TC
    nch = T // TC
    CP = 256
    zeros = jnp.zeros((LRU_PAD, LANES), F32)
    for s in range(SUBLANES + 1):
        xs_ref[s * P:s * P + LRU_PAD] = zeros
    for s in range(SUBLANES):
        def copy_in(c, carry, s=s):
            r0 = pl.multiple_of(c * CP, CP)
            xs_ref[pl.ds(s * P + LRU_PAD + r0, CP)] = xr_ref[s, pl.ds(r0, CP)]
            return carry
        lax.fori_loop(0, T // CP, copy_in, 0)

    cw = cw_ref[...]
    cb = cb_ref[...]
    sp = _softplus(-lam_ref[0])

    def gates(t0, d):
        win = [xs_ref[pl.ds(t0 + k + LRU_PAD - 2, SUBLANES, stride=P)] for k in range(TC + 3)]
        xc = jnp.stack([cw[0:1] * win[i] + cw[1:2] * win[i + 1] + cw[2:3] * win[i + 2] + cw[3:4] * win[i + 3] + cb
                        for i in range(TC)])
        x2 = xc.reshape(TC * SUBLANES, LANES)
        g = (jnp.dot(x2.astype(BF16), wg_ref[0, :, d * 256:(d + 1) * 256], preferred_element_type=F32)
             + bg_ref[0, :, d * 256:(d + 1) * 256])
        r = jax.nn.sigmoid(g[:, :LANES])
        i = jax.nn.sigmoid(g[:, LANES:])
        log_a = -LRU_C * r * sp[:, d * LANES:(d + 1) * LANES]
        a = jnp.exp(log_a)
        u = jnp.sqrt(jnp.maximum(1.0 - jnp.exp(2.0 * log_a), 0.0)) * (i * x2)
        return a.reshape(TC, SUBLANES, LANES), u.reshape(TC, SUBLANES, LANES)

    def fwd(c, h):
        t0 = pl.multiple_of(c * TC, TC)
        a, u = gates(t0, 0)
        for i in range(TC):
            h = a[i] * h + u[i]
            os_ref[pl.ds(t0 + i, SUBLANES, stride=P)] = h
        return h

    hT_ref[0] = lax.fori_loop(0, nch, fwd, h0_ref[0])

    def bwd(c, h):
        t0 = pl.multiple_of((nch - 1 - c) * TC, TC)
        a, u = gates(t0, 1)
        for i in range(TC - 1, -1, -1):
            h = a[i] * h + u[i]
            os_ref[pl.ds(t0 + i, SUBLANES, stride=P)] = os_ref[pl.ds(t0 + i, SUBLANES, stride=P)] + h
        return h

    hT_ref[1] = lax.fori_loop(0, nch, bwd, h0_ref[1])

    for s in range(SUBLANES):
        def copy_out(c, carry, s=s):
            r0 = pl.multiple_of(c * CP, CP)
            y_ref[s, pl.ds(r0, CP)] = jax.nn.gelu(gr_ref[s, pl.ds(r0, CP)]) * os_ref[pl.ds(s * P + r0, CP)]
            return carry
        lax.fori_loop(0, T // CP, copy_out, 0)


def _rglru_call(z3, h0, cw, cb, wg, bg, lam):
    B, T, _ = z3.shape
    nb = B // SUBLANES
    ncb = D_GROUP // LANES
    blk = (SUBLANES, T, LANES)
    pitch = T + LRU_PAD
    return pl.pallas_call(
        _rglru_kernel,
        grid=(nb, ncb),
        in_specs=[pl.BlockSpec(blk, lambda b, c: (b, 0, c)),
                  pl.BlockSpec(blk, lambda b, c: (b, 0, ncb + c)),
                  pl.BlockSpec((4, LANES), lambda b, c: (0, c)),
                  pl.BlockSpec((1, LANES), lambda b, c: (0, c)),
                  pl.BlockSpec((1, LANES, 4 * LANES), lambda b, c: (c, 0, 0)),
                  pl.BlockSpec((1, 1, 4 * LANES), lambda b, c: (c, 0, 0)),
                  pl.BlockSpec((1, 1, 2 * LANES), lambda b, c: (c, 0, 0)),
                  pl.BlockSpec((2, SUBLANES, LANES), lambda b, c: (0, b, c))],
        out_specs=[pl.BlockSpec(blk, lambda b, c: (b, 0, c)),
                   pl.BlockSpec((2, SUBLANES, LANES), lambda b, c: (0, b, c))],
        out_shape=[jax.ShapeDtypeStruct((B, T, D_GROUP), F32),
                   jax.ShapeDtypeStruct((2, B, D_GROUP), F32)],
        scratch_shapes=[pltpu.VMEM((SUBLANES * pitch + LRU_PAD, LANES), F32),
                        pltpu.VMEM((SUBLANES * pitch, LANES), F32)],
        compiler_params=_cparams(2),
        name="rglru",
    )(z3, z3, cw, cb, wg, bg, lam, h0)


def _hgrn_kernel(q_ref, ff_ref, fb_ref, v_ref, og_ref, lb_ref, ng_ref, s0_ref,
                 y_ref, sT_ref, st_ref, of_ref, ob_ref):
    T = q_ref.shape[0]
    L = HG_L
    SUB = HG_SUB
    nch = T // L
    nsub = L // SUB
    tril, triu = _tri_masks(L)
    heads = [slice(hd * D_HEAD, (hd + 1) * D_HEAD) for hd in range(N_HEADS)]

    for d in range(2):
        for hd in range(N_HEADS):
            st_ref[d, hd] = s0_ref[d, hd].T

    def step(c, carry):
        jobs = ((0, pl.multiple_of(c * L, L), of_ref), (1, pl.multiple_of((nch - 1 - c) * L, L), ob_ref))
        gated = []
        for d, t0, _ in jobs:
            q = _silu(q_ref[pl.ds(t0, L), :])
            v = v_ref[pl.ds(t0, L), :].astype(BF16)
            lb = lb_ref[d:d + 1, :]
            sig = jax.nn.sigmoid((ff_ref if d == 0 else fb_ref)[pl.ds(t0, L), :])
            log_f = jnp.log(jnp.maximum(lb + (1.0 - lb) * sig, TINY))
            kk = (1.0 - lb) * (1.0 - sig)
            mask = tril if d == 0 else triu
            b = _dot_split3(mask, log_f, mask_on_left=True)
            gated.append((q, v, kk, b, mask))
        factored = []
        for (d, _, _), (q, v, kk, b, mask) in zip(jobs, gated):
            b_tot = b[L - 1:L] if d == 0 else b[0:1]
            subs = []
            for i in range(nsub):
                rows = slice(i * SUB, (i + 1) * SUB)
                if d == 0:
                    cols = slice(0, (i + 1) * SUB)
                    edge = b[i * SUB - 1:i * SUB] if i > 0 else 0.0
                else:
                    cols = slice(i * SUB, L)
                    edge = b[(i + 1) * SUB:(i + 1) * SUB + 1] if i < nsub - 1 else 0.0
                subs.append((rows, cols, (q[rows] * jnp.exp(b[rows] - edge)).astype(BF16),
                             (kk[cols] * jnp.exp(edge - b[cols])).astype(BF16)))
            factored.append(((q * jnp.exp(b)).astype(BF16), (kk * jnp.exp(b_tot - b)).astype(BF16),
                             jnp.exp(b_tot), subs))
        scores = [[[_dot_nt(qs[:, hs], ks[:, hs]) for (_, _, qs, ks) in subs] for hs in heads]
                  for (_, _, _, subs) in factored]
        states = [[st_ref[d, hd] for hd in range(N_HEADS)] for d, _, _ in jobs]
        inter = [[_dot_nt(qdec[:, hs], states[j][hd]) for hd, hs in enumerate(heads)]
                 for j, (qdec, _, _, _) in enumerate(factored)]
        update = [[_dot_tn(gated[j][1][:, hs], kdec[:, hs]) for hs in heads]
                  for j, (_, kdec, _, _) in enumerate(factored)]
        for j, (d, t0, o_ref) in enumerate(jobs):
            v, mask = gated[j][1], gated[j][4]
            dec, subs = factored[j][2], factored[j][3]
            for hd, hs in enumerate(heads):
                intra = [_dot(jnp.where(mask[rows, cols], scores[j][hd][i], 0.0), v[cols, hs])
                         for i, (rows, cols, _, _) in enumerate(subs)]
                o_ref[pl.ds(t0, L), hs] = jnp.concatenate(intra, axis=0) + inter[j][hd]
                st_ref[d, hd] = states[j][hd] * dec[:, hs] + update[j][hd]
        return carry

    lax.fori_loop(0, nch, step, 0)

    def combine(c, carry):
        t0 = pl.multiple_of(c * L, L)
        for hs in heads:
            o = of_ref[pl.ds(t0, L), hs] + ob_ref[pl.ds(t0, L), hs]
            o = o * lax.rsqrt(jnp.mean(jnp.square(o), -1, keepdims=True) + EPS)
            y_ref[pl.ds(t0, L), hs] = o * ng_ref[:, hs] * _silu(og_ref[pl.ds(t0, L), hs])
        return carry

    lax.fori_loop(0, nch, combine, 0)

    for d in range(2):
        for hd in range(N_HEADS):
            sT_ref[d, hd] = st_ref[d, hd].T


def _seq_col_spec(T, j):
    return pl.BlockSpec((T, D_GROUP), lambda b, j=j: (b, j))


def _hgrn_call(z2, B, s0, lower, norm_g):
    T = z2.shape[0] // B
    col = functools.partial(_seq_col_spec, T)
    st_spec = pl.BlockSpec((None, 2, N_HEADS, D_HEAD, D_HEAD), lambda b: (b, 0, 0, 0, 0))
    return pl.pallas_call(
        _hgrn_kernel,
        grid=(B,),
        in_specs=[col(2), col(3), col(4), col(5), col(6),
                  pl.BlockSpec((2, D_GROUP), lambda b: (0, 0)),
                  pl.BlockSpec((1, D_GROUP), lambda b: (0, 0)),
                  st_spec],
        out_specs=[pl.BlockSpec((T, D_GROUP), lambda b: (b, 0)), st_spec],
        out_shape=[jax.ShapeDtypeStruct((B * T, D_GROUP), F32),
                   jax.ShapeDtypeStruct((B, 2, N_HEADS, D_HEAD, D_HEAD), F32)],
        scratch_shapes=[pltpu.VMEM((2, N_HEADS, D_HEAD, D_HEAD), F32),
                        pltpu.VMEM((T, D_GROUP), F32),
                        pltpu.VMEM((T, D_GROUP), F32)],
        compiler_params=_cparams(1),
        name="hgrn2",
    )(z2, z2, z2, z2, z2, lower, norm_g, s0)


def _mlstm_kernel(q_ref, k_ref, v_ref, og_ref, g_ref, ng_ref, c0_ref, n0_ref, m0_ref,
                  y_ref, cT_ref, nT_ref, mT_ref, of_ref, ob_ref):
    T = q_ref.shape[0]
    L = ML_L
    nch = T // L
    tril, triu = _tri_masks(L)
    gcol = lax.broadcasted_iota(jnp.int32, (L, N_GATE), 1)
    grow = lax.broadcasted_iota(jnp.int32, (N_GATE, L), 0)
    heads = [slice(hd * D_HEAD, (hd + 1) * D_HEAD) for hd in range(N_HEADS)]

    cT_ref[...] = c0_ref[...]
    nT_ref[...] = n0_ref[...]
    mT_ref[...] = m0_ref[...]

    def step(c, carry):
        jobs = ((0, pl.multiple_of(c * L, L), of_ref), (1, pl.multiple_of((nch - 1 - c) * L, L), ob_ref))
        gates, qkv, qk, qc = [], [], [], []
        for d, t0, _ in jobs:
            g_all = g_ref[pl.ds(t0, L), :]
            g, gt = g_all[:, 0:N_GATE], g_all.T[0:N_GATE, :]
            g = jnp.where(gcol % 8 >= N_HEADS, jax.nn.log_sigmoid(g), g)
            gt = jnp.where(grow % 8 >= N_HEADS, jax.nn.log_sigmoid(gt), gt)
            bc = _dot_split3(tril if d == 0 else triu, g, mask_on_left=True)
            br = _dot_split3(triu if d == 0 else tril, gt, mask_on_left=False)
            gates.append((g, gt, bc, br))
            q = q_ref[pl.ds(t0, L), :]
            kf = k_ref[pl.ds(t0, L), :] * (D_HEAD ** -0.5)
            qb, kb, vb = q.astype(BF16), kf.astype(BF16), v_ref[pl.ds(t0, L), :].astype(BF16)
            qkv.append((q, kf, qb, vb))
            qk.append([_dot_nt(qb[:, hs], kb[:, hs]) for hs in heads])
            qc.append([_dot(qb[:, hs], cT_ref[d, hd]) for hd, hs in enumerate(heads)])
        weights = []
        for j, (d, _, _) in enumerate(jobs):
            g, gt, bc, br = gates[j]
            q, kf, _, _ = qkv[j]
            mask = tril if d == 0 else triu
            per_head = []
            for hd, hs in enumerate(heads):
                ic, fc = d * 8 + hd, d * 8 + N_HEADS + hd
                li_col, li_row = g[:, ic:ic + 1], gt[ic:ic + 1, :]
                b_col, b_row = bc[:, fc:fc + 1], br[fc:fc + 1, :]
                b_tot = b_col[L - 1:L] if d == 0 else b_col[0:1]
                n_st = nT_ref[d, hd:hd + 1, :]
                m_st = mT_ref[d, hd:hd + 1, 0:1]
                dm = jnp.where(mask, b_col - b_row + li_row, NEG)
                inter = b_col + m_st
                m_t = jnp.maximum(inter, jnp.max(dm, axis=1, keepdims=True))
                s_inter = jnp.exp(inter - m_t)
                w = jnp.exp(dm - m_t) * qk[j][hd]
                den = (jnp.sum(w, axis=1, keepdims=True)
                       + s_inter * jnp.sum(q[:, hs] * n_st, axis=1, keepdims=True))
                scale = 1.0 / jnp.maximum(jnp.abs(den), jnp.exp(-m_t))
                w_state = b_tot - b_col + li_col
                m_loc = jnp.max(w_state, axis=0, keepdims=True)
                ek = jnp.exp(w_state - m_loc) * kf[:, hs]
                m_new = jnp.maximum(b_tot + m_st, m_loc)
                s_old = jnp.exp(b_tot + m_st - m_new)
                s_new = jnp.exp(m_loc - m_new)
                per_head.append((w.astype(BF16), s_inter, scale, ek, n_st, m_new, s_old, s_new))
            weights.append(per_head)
        num = [[_dot(weights[j][hd][0], qkv[j][3][:, hs]) for hd, hs in enumerate(heads)] for j in range(2)]
        d_c = [[_dot_tn(weights[j][hd][3], qkv[j][3][:, hs]) for hd, hs in enumerate(heads)] for j in range(2)]
        for j, (d, t0, o_ref) in enumerate(jobs):
            for hd, hs in enumerate(heads):
                _, s_inter, scale, ek, n_st, m_new, s_old, s_new = weights[j][hd]
                o_ref[pl.ds(t0, L), hs] = (num[j][hd] + s_inter * qc[j][hd]) * scale
                cT_ref[d, hd] = s_old * cT_ref[d, hd] + s_new * d_c[j][hd]
                nT_ref[d, hd:hd + 1, :] = s_old * n_st + s_new * jnp.sum(ek, axis=0, keepdims=True)
                mT_ref[d, hd:hd + 1, :] = jnp.broadcast_to(m_new, (1, LANES))
        return carry

    lax.fori_loop(0, nch, step, 0)

    def combine(c, carry):
        t0 = pl.multiple_of(c * L, L)
        for hs in heads:
            o = of_ref[pl.ds(t0, L), hs] + ob_ref[pl.ds(t0, L), hs]
            mu = jnp.mean(o, -1, keepdims=True)
            var = jnp.mean(jnp.square(o - mu), -1, keepdims=True)
            o = (o - mu) * lax.rsqrt(var + EPS)
            y_ref[pl.ds(t0, L), hs] = jax.nn.sigmoid(og_ref[pl.ds(t0, L), hs]) * (o * ng_ref[:, hs])
        return carry

    lax.fori_loop(0, nch, combine, 0)


def _mlstm_call(z2, B, c0, n0, m0, norm_g):
    T = z2.shape[0] // B
    col = functools.partial(_seq_col_spec, T)
    c_spec = pl.BlockSpec((None, 2, N_HEADS, D_HEAD, D_HEAD), lambda b: (b, 0, 0, 0, 0))
    v_spec = pl.BlockSpec((None, 2, N_HEADS, D_HEAD), lambda b: (b, 0, 0, 0))
    return pl.pallas_call(
        _mlstm_kernel,
        grid=(B,),
        in_specs=[col(7), col(8), col(9), col(10),
                  pl.BlockSpec((T, LANES), lambda b: (b, GATE_BLK)),
                  pl.BlockSpec((1, D_GROUP), lambda b: (0, 0)),
                  c_spec, v_spec, v_spec],
        out_specs=[pl.BlockSpec((T, D_GROUP), lambda b: (b, 0)), c_spec, v_spec, v_spec],
        out_shape=[jax.ShapeDtypeStruct((B * T, D_GROUP), F32),
                   jax.ShapeDtypeStruct((B, 2, N_HEADS, D_HEAD, D_HEAD), F32),
                   jax.ShapeDtypeStruct((B, 2, N_HEADS, D_HEAD), F32),
                   jax.ShapeDtypeStruct((B, 2, N_HEADS, D_HEAD), F32)],
        scratch_shapes=[pltpu.VMEM((T, D_GROUP), F32), pltpu.VMEM((T, D_GROUP), F32)],
        compiler_params=_cparams(1),
        name="mlstm",
    )(z2, z2, z2, z2, z2, norm_g, c0, n0, m0)


def _layer_norm(v, g, b):
    mu = jnp.mean(v, -1, keepdims=True)
    var = jnp.mean(jnp.square(v - mu), -1, keepdims=True)
    return (v - mu) * lax.rsqrt(var + EPS) * g + b


def _route(logits_t, b_router):
    mx = jnp.max(logits_t, axis=0, keepdims=True)
    ex = jnp.exp(logits_t - mx)
    scores = ex / jnp.sum(ex, axis=0, keepdims=True)
    sel = scores + b_router
    rows = [sel[e:e + 1] for e in range(N_EXPERTS)]
    first, second, grp = [], [], []
    for gi in range(N_GROUPS):
        r = rows[gi * EXPERTS_PER_GROUP:(gi + 1) * EXPERTS_PER_GROUP]
        m1 = functools.reduce(jnp.maximum, r)
        taken = jnp.zeros_like(m1, dtype=jnp.bool_)
        f = []
        for x in r:
            hit = jnp.logical_and(x == m1, jnp.logical_not(taken))
            f.append(hit)
            taken = jnp.logical_or(taken, hit)
        rest = [jnp.where(fi, -jnp.inf, x) for fi, x in zip(f, r)]
        m2 = functools.reduce(jnp.maximum, rest)
        taken = jnp.zeros_like(m1, dtype=jnp.bool_)
        s = []
        for x in rest:
            hit = jnp.logical_and(x == m2, jnp.logical_not(taken))
            s.append(hit)
            taken = jnp.logical_or(taken, hit)
        first.append(f)
        second.append(s)
        grp.append(m1 + m2)
    gmax = functools.reduce(jnp.maximum, grp)
    taken = jnp.zeros_like(gmax, dtype=jnp.bool_)
    chosen = []
    group = jnp.zeros_like(gmax, dtype=jnp.int32)
    for gi in range(N_GROUPS):
        best = jnp.logical_and(grp[gi] == gmax, jnp.logical_not(taken))
        taken = jnp.logical_or(taken, best)
        group = jnp.where(best, gi, group)
        for j in range(EXPERTS_PER_GROUP):
            chosen.append(jnp.logical_and(best, jnp.logical_or(first[gi][j], second[gi][j])))
    picked = [jnp.where(ch, scores[e:e + 1], 0.0) for e, ch in enumerate(chosen)]
    denom = functools.reduce(lambda a, b: a + b, picked)
    return jnp.concatenate([p / denom for p in picked], axis=0), group


D_DISPATCH = D_MODEL + LANES


def _outproj_kernel(yr_ref, yh_ref, ym_ref, x_ref, mod_ref, w_ref, lg_ref, lb_ref,
                    wr_ref, br_ref, x1_ref, hx_ref, grp_ref):
    tm = x_ref.shape[0]
    m = mod_ref[...]
    y = (jnp.dot(yr_ref[...].astype(BF16), w_ref[0:D_GROUP], preferred_element_type=F32)
         + jnp.dot(yh_ref[...].astype(BF16), w_ref[D_GROUP:2 * D_GROUP], preferred_element_type=F32)
         + jnp.dot(ym_ref[...].astype(BF16), w_ref[2 * D_GROUP:], preferred_element_type=F32))
    x1 = _layer_norm(ALPHA * x_ref[...] + m[GATE1:GATE1 + 1] * y, lg_ref[...], lb_ref[...])
    x1_ref[...] = x1
    hm = x1 * (1.0 + m[SCALE2:SCALE2 + 1]) + m[SHIFT2:SHIFT2 + 1]
    hx_ref[:, 0:D_MODEL] = hm
    logits_t = lax.dot_general(wr_ref[...], hm, (((1,), (1,)), ((), ())), precision=HIGHEST,
                               preferred_element_type=F32)
    cmb_t, group = _route(logits_t, br_ref[...])
    cmb_rows = jnp.concatenate([cmb_t, jnp.zeros((LANES - N_EXPERTS, tm), F32)], axis=0)
    hx_ref[:, D_MODEL:] = cmb_rows.T
    grp_ref[...] = group


def _outproj_call(yr, yh, ym, x, T, mod, w_out, ln_g, ln_b, w_router_t, b_router):
    n = x.shape[0]
    tm = ROW_TILE
    row = lambda w: pl.BlockSpec((tm, w), lambda i: (i, 0))
    full = lambda a: pl.BlockSpec(a.shape, lambda i: (0,) * a.ndim)
    return pl.pallas_call(
        _outproj_kernel,
        grid=(n // tm,),
        in_specs=[row(D_GROUP), row(D_GROUP), row(D_GROUP), row(D_MODEL), _mod_spec(mod, T, tm),
                  full(w_out), full(ln_g), full(ln_b), full(w_router_t), full(b_router)],
        out_specs=[row(D_MODEL), row(D_DISPATCH), pl.BlockSpec((1, tm), lambda i: (0, i))],
        out_shape=[jax.ShapeDtypeStruct((n, D_MODEL), F32),
                   jax.ShapeDtypeStruct((n, D_DISPATCH), F32),
                   jax.ShapeDtypeStruct((1, n), jnp.int32)],
        compiler_params=_cparams(1),
        name="outproj_ln_router",
    )(yr, yh, ym, x, mod, w_out, ln_g, ln_b, w_router_t, b_router)


MOE_TILE = 256


def _row_gather(idx_ref, first, n_rows, src_hbm, dst_ref, sem):
    def body(r, carry):
        pltpu.make_async_copy(src_hbm.at[pl.ds(idx_ref[first + r], 1)], dst_ref.at[pl.ds(r, 1)], sem).start()
        return carry
    lax.fori_loop(0, n_rows, body, 0, unroll=8)


def _row_gather_wait(n_rows, src_hbm, dst_ref, sem):
    def body(r, carry):
        pltpu.make_async_copy(src_hbm.at[pl.ds(0, 1)], dst_ref.at[pl.ds(0, 1)], sem).wait()
        return carry
    lax.fori_loop(0, n_rows, body, 0, unroll=8)


def _gathered_tile(idx_ref, src_hbm, buf_ref, sem_ref, tm):
    i = pl.program_id(0)
    slot = lax.rem(i, 2)

    @pl.when(i == 0)
    def _():
        _row_gather(idx_ref, 0, tm, src_hbm, buf_ref.at[0], sem_ref.at[0])

    @pl.when(i + 1 < pl.num_programs(0))
    def _():
        _row_gather(idx_ref, (i + 1) * tm, tm, src_hbm, buf_ref.at[1 - slot], sem_ref.at[1 - slot])

    _row_gather_wait(tm, src_hbm, buf_ref.at[slot], sem_ref.at[slot])
    return buf_ref[slot]


def _moe_kernel(tile_group_ref, src_ref, hx_hbm, wg_ref, wu_ref, wd_ref, o_ref, buf_ref, sem_ref):
    tm = o_ref.shape[0]
    rows = _gathered_tile(src_ref, hx_hbm, buf_ref, sem_ref, tm)
    hm = rows[:, 0:D_MODEL].astype(BF16)
    cmb = rows[:, D_MODEL:]
    lane = lax.broadcasted_iota(jnp.int32, cmb.shape, 1)
    first_expert = tile_group_ref[pl.program_id(0)] * EXPERTS_PER_GROUP
    acc = None
    for k in range(EXPERTS_PER_GROUP):
        ce = jnp.sum(jnp.where(lane == first_expert + k, cmb, 0.0), axis=1, keepdims=True)
        hg = jnp.dot(hm, wg_ref[k], preferred_element_type=F32)
        hu = jnp.dot(hm, wu_ref[k], preferred_element_type=F32)
        part = jnp.dot((_silu(hg) * hu * ce).astype(BF16), wd_ref[k], preferred_element_type=F32)
        acc = part if acc is None else acc + part
    o_ref[...] = acc


def _moe_call(hx, tile_group, src, w_gate, w_up, w_down):
    n_pad = src.shape[0]
    tm = MOE_TILE
    grp_w = lambda shape: pl.BlockSpec(shape, lambda i, tg, sr: (tg[i], 0, 0))
    return pl.pallas_call(
        _moe_kernel,
        grid_spec=pltpu.PrefetchScalarGridSpec(
            num_scalar_prefetch=2,
            grid=(n_pad // tm,),
            in_specs=[pl.BlockSpec(memory_space=pl.ANY),
                      grp_w((EXPERTS_PER_GROUP, D_MODEL, D_FF)), grp_w((EXPERTS_PER_GROUP, D_MODEL, D_FF)),
                      grp_w((EXPERTS_PER_GROUP, D_FF, D_MODEL))],
            out_specs=pl.BlockSpec((tm, D_MODEL), lambda i, tg, sr: (i, 0)),
            scratch_shapes=[pltpu.VMEM((2, tm, D_DISPATCH), F32), pltpu.SemaphoreType.DMA((2,))]),
        out_shape=jax.ShapeDtypeStruct((n_pad, D_MODEL), F32),
        compiler_params=_cparams(1),
        name="moe_sorted",
    )(tile_group, src, hx, w_gate, w_up, w_down)


def _ln2_kernel(pos_ref, f_hbm, x1_ref, mod_ref, lg_ref, lb_ref, o_ref, buf_ref, sem_ref):
    tm = o_ref.shape[0]
    f = _gathered_tile(pos_ref, f_hbm, buf_ref, sem_ref, tm)
    v = ALPHA * x1_ref[...] + mod_ref[GATE2:GATE2 + 1, :] * f
    o_ref[...] = _layer_norm(v, lg_ref[...], lb_ref[...])


def _ln2_call(f_sorted, pos, x1, T, mod, ln_g, ln_b):
    n = x1.shape[0]
    tm = ROW_TILE
    per_seq = mod.shape[0] > 1
    return pl.pallas_call(
        _ln2_kernel,
        grid_spec=pltpu.PrefetchScalarGridSpec(
            num_scalar_prefetch=1,
            grid=(n // tm,),
            in_specs=[pl.BlockSpec(memory_space=pl.ANY),
                      pl.BlockSpec((tm, D_MODEL), lambda i, ps: (i, 0)),
                      pl.BlockSpec((None, 6, D_MODEL), lambda i, ps: ((i * tm) // T if per_seq else 0, 0, 0)),
                      pl.BlockSpec((1, D_MODEL), lambda i, ps: (0, 0)),
                      pl.BlockSpec((1, D_MODEL), lambda i, ps: (0, 0))],
            out_specs=pl.BlockSpec((tm, D_MODEL), lambda i, ps: (i, 0)),
            scratch_shapes=[pltpu.VMEM((2, tm, D_MODEL), F32), pltpu.SemaphoreType.DMA((2,))]),
        out_shape=jax.ShapeDtypeStruct((n, D_MODEL), F32),
        compiler_params=_cparams(1),
        name="moe_combine_ln",
    )(pos, f_sorted, x1, mod, ln_g, ln_b)


def _dispatch_plan(group, tm):
    n = group.shape[0]
    n_pad = n + N_GROUPS * tm
    onehot = (group[:, None] == jnp.arange(N_GROUPS, dtype=jnp.int32)[None, :]).astype(jnp.int32)
    csum = jnp.cumsum(onehot, axis=0)
    rank = jnp.take_along_axis(csum, group[:, None], axis=1)[:, 0] - 1
    padded = ((csum[-1] + tm - 1) // tm) * tm
    ends = jnp.cumsum(padded)
    pos = (ends - padded)[group] + rank
    src = jnp.zeros((n_pad,), jnp.int32).at[pos].set(jnp.arange(n, dtype=jnp.int32))
    tile_start = jnp.arange(n_pad // tm, dtype=jnp.int32) * tm
    tile_group = jnp.sum((tile_start[:, None] >= ends[None, :]).astype(jnp.int32), axis=1)
    tile_group = jnp.minimum(tile_group, N_GROUPS - 1)
    return pos, src, tile_group


def _grid_pos_embed(n_tokens):
    rows = n_tokens // GRID_W
    r = jnp.repeat(jnp.arange(rows, dtype=F32), GRID_W)
    c = jnp.tile(jnp.arange(GRID_W, dtype=F32), rows)
    q = D_MODEL // 4
    freq = jnp.exp(-jnp.log(10000.0) * jnp.arange(q, dtype=F32) / q)
    ar = r[:, None] * freq
    ac = c[:, None] * freq
    return jnp.concatenate([jnp.sin(ar), jnp.cos(ar), jnp.sin(ac), jnp.cos(ac)], axis=-1)


def _lru_gate_params(wa, ba, wx, bx, lam):
    ncb = D_GROUP // LANES

    def dense(w):
        z = jnp.zeros((ncb, LANES, LANES), F32)
        z = z.at[:, :LRU_BLOCK, :LRU_BLOCK].set(w[0::2])
        return z.at[:, LRU_BLOCK:, LRU_BLOCK:].set(w[1::2])

    wg = jnp.concatenate([dense(wa[0]), dense(wx[0]), dense(wa[1]), dense(wx[1])], axis=-1).astype(BF16)
    per_blk = lambda v: v.reshape(ncb, 1, LANES)
    bg = jnp.concatenate([per_blk(ba[0]), per_blk(bx[0]), per_blk(ba[1]), per_blk(bx[1])], axis=-1)
    lm = jnp.concatenate([per_blk(lam[0]), per_blk(lam[1])], axis=-1)
    return wg, bg, lm


def _mixer_states(B, l, states):
    if states is None:
        return (jnp.zeros((2, B, D_GROUP), F32),
                jnp.zeros((B, 2, N_HEADS, D_HEAD, D_HEAD), F32),
                jnp.zeros((B, 2, N_HEADS, D_HEAD, D_HEAD), F32),
                jnp.zeros((B, 2, N_HEADS, D_HEAD), F32),
                jnp.zeros((B, 2, N_HEADS, D_HEAD), F32))
    h, s, c, n, m = states
    return (jnp.swapaxes(h[:, l], 0, 1), s[:, l], c[:, l], n[:, l],
            jnp.broadcast_to(m[:, l][..., None], (B, 2, N_HEADS, D_HEAD)))


def kernel(x_prompt, x_sample, state_lru_h, state_hgrn_S, state_mlstm_C, state_mlstm_n, state_mlstm_m,
           c, c_ctx, w_ada, b_ada, w_in, b_in, conv_w, conv_b, lru_wa, lru_ba, lru_wx, lru_bx, lru_lam,
           hg_lb, hg_norm_g, ml_norm_g, w_out, ln1_g, ln1_b, ln2_g, ln2_b,
           w_router, b_router, w_gate, w_up, w_down):
    Bp, Tp, _ = x_prompt.shape
    Bs, Ts, _ = x_sample.shape
    assert Bs % SUBLANES == 0 and Bp % SUBLANES == 0 and Bs + 1 <= 16
    assert Tp % ROW_TILE == 0 and Ts % ROW_TILE == 0

    lb_soft = jax.nn.softmax(hg_lb.astype(F32), axis=0)
    hg_lower = jnp.cumsum(lb_soft, axis=0) - lb_soft[0:1]

    c16 = jnp.concatenate([c, c_ctx[None], jnp.zeros((16 - Bs - 1, D_MODEL), F32)], axis=0)
    mod = _ada_call(c16, w_ada, b_ada)

    w_in_p = jnp.pad(w_in, ((0, 0), (0, 0), (0, D_IN_PAD - D_IN))).astype(BF16)
    b_in_p = jnp.pad(b_in, ((0, 0), (0, D_IN_PAD - D_IN))).reshape(DEPTH, 1, D_IN_PAD)
    w_out_b = w_out.astype(BF16)
    w_gate_b, w_up_b, w_down_b = w_gate.astype(BF16), w_up.astype(BF16), w_down.astype(BF16)
    w_router_t = w_router.T
    b_router_c = b_router.reshape(N_EXPERTS, 1)

    xs = x_sample + _grid_pos_embed(Ts).astype(x_sample.dtype)
    streams = [
        dict(x=x_prompt.reshape(Bp * Tp, D_MODEL), T=Tp, B=Bp, states=None),
        dict(x=xs.reshape(Bs * Ts, D_MODEL), T=Ts, B=Bs,
             states=(state_lru_h, state_hgrn_S, state_mlstm_C, state_mlstm_n, state_mlstm_m)),
    ]
    finals = []
    for l in range(DEPTH):
        wg, bg, lm = _lru_gate_params(lru_wa[l], lru_ba[l], lru_wx[l], lru_bx[l], lru_lam[l])
        row = lambda v: v[l].reshape(1, -1)
        for si, st in enumerate(streams):
            T, B = st["T"], st["B"]
            mod_s = (mod[l, :Bs] if si == 1 else mod[l, Bs:Bs + 1]).reshape(-1, 6, D_MODEL)
            h0, s0, c0, n0, m0 = _mixer_states(B, l, st["states"])
            z = _inproj_call(st["x"], T, mod_s, w_in_p[l], b_in_p[l])
            y_r, h_T = _rglru_call(z.reshape(B, T, D_IN_PAD), h0, conv_w[l], row(conv_b), wg, bg, lm)
            y_h, s_T = _hgrn_call(z, B, s0, hg_lower[l], row(hg_norm_g))
            y_m, c_T, n_T, m_T = _mlstm_call(z, B, c0, n0, m0, row(ml_norm_g))
            x1, hx, group = _outproj_call(y_r.reshape(B * T, D_GROUP), y_h, y_m, st["x"], T, mod_s, w_out_b[l],
                                          row(ln1_g), row(ln1_b), w_router_t, b_router_c)
            pos, src, tile_group = _dispatch_plan(group[0], MOE_TILE)
            f_sorted = _moe_call(hx, tile_group, src, w_gate_b[l], w_up_b[l], w_down_b[l])
            st["x"] = _ln2_call(f_sorted, pos, x1, T, mod_s, row(ln2_g), row(ln2_b))
            if si == 0:
                finals.append((jnp.swapaxes(h_T, 0, 1), s_T, c_T, n_T, m_T[..., 0]))
    outs = [st["x"].reshape(st["B"], st["T"], D_MODEL) for st in streams]
    stack = lambda i: jnp.stack([f[i] for f in finals], axis=1)
    return (outs[0], outs[1], stack(0), stack(1), stack(2), stack(3), stack(4))
```

```python
import functools

import jax
import jax.numpy as jnp
from jax import lax
from jax.experimental import pallas as pl
from jax.experimental.pallas import tpu as pltpu

F32 = jnp.float32
BF16 = jnp.bfloat16
HIGHEST = lax.Precision.HIGHEST

D_MODEL = 1024
DEPTH = 2
GRID_W = 64
D_GROUP = 512
D_MIX = 3 * D_GROUP
LRU_BLOCKS = 8
LRU_BLOCK = D_GROUP // LRU_BLOCKS
LRU_C = 8.0
N_HEADS = 4
D_HEAD = D_GROUP // N_HEADS
N_EXPERTS = 16
N_GROUPS = 4
EXPERTS_PER_GROUP = N_EXPERTS // N_GROUPS
D_FF = 512
ALPHA = (2.0 * DEPTH) ** 0.25
EPS = 1e-5
NEG = -1e30
TINY = 1e-30
N_GATE = 4 * N_HEADS
D_IN = 11 * D_GROUP + N_GATE
D_IN_PAD = 45 * 128
GATE_BLK = (11 * D_GROUP) // 128

SUBLANES = 8
LANES = 128
VMEM_LIMIT = 56 * 1024 * 1024

LRU_TC = 32
LRU_PAD = SUBLANES
HG_L = 64
HG_SUB = 16
ML_L = 128
ML_SUB = 32


def _cparams(n_axes):
    return pltpu.CompilerParams(dimension_semantics=("arbitrary",) * n_axes,
                                vmem_limit_bytes=VMEM_LIMIT)


def _dot(a, b):
    return jnp.dot(a.astype(BF16), b.astype(BF16), preferred_element_type=F32)


def _dot_nt(a, b):
    return lax.dot_general(a.astype(BF16), b.astype(BF16), (((1,), (1,)), ((), ())),
                           preferred_element_type=F32)


def _dot_tn(a, b):
    return lax.dot_general(a.astype(BF16), b.astype(BF16), (((0,), (0,)), ((), ())),
                           preferred_element_type=F32)


def _dot_f32(a, b):
    return jnp.dot(a, b, precision=HIGHEST, preferred_element_type=F32)


def _dot_split3(mask, x, mask_on_left):
    m = jnp.where(mask, 1.0, 0.0).astype(BF16)
    hi = x.astype(BF16)
    r1 = x - hi.astype(F32)
    mid = r1.astype(BF16)
    lo = (r1 - mid.astype(F32)).astype(BF16)
    mm = (lambda p: jnp.dot(m, p, preferred_element_type=F32)) if mask_on_left else (
        lambda p: jnp.dot(p, m, preferred_element_type=F32))
    return mm(hi) + mm(mid) + mm(lo)


def _softplus(x):
    return jnp.maximum(x, 0.0) + jnp.log1p(jnp.exp(-jnp.abs(x)))


def _silu(x):
    return x * jax.nn.sigmoid(x)


def _tri_masks(n):
    r = lax.broadcasted_iota(jnp.int32, (n, n), 0)
    c = lax.broadcasted_iota(jnp.int32, (n, n), 1)
    return r >= c, r <= c


def _ada_kernel(c_ref, w_ref, b_ref, o_ref):
    o_ref[0] = _dot_f32(_silu(c_ref[...]), w_ref[0]) + b_ref[0]


def _ada_call(c16, w_ada, b_ada):
    tn = 1536
    return pl.pallas_call(
        _ada_kernel,
        grid=(DEPTH, 6 * D_MODEL // tn),
        in_specs=[pl.BlockSpec((16, D_MODEL), lambda l, j: (0, 0)),
                  pl.BlockSpec((1, D_MODEL, tn), lambda l, j: (l, 0, j)),
                  pl.BlockSpec((1, 1, tn), lambda l, j: (l, 0, j))],
        out_specs=pl.BlockSpec((1, 16, tn), lambda l, j: (l, 0, j)),
        out_shape=jax.ShapeDtypeStruct((DEPTH, 16, 6 * D_MODEL), F32),
        compiler_params=_cparams(2),
        name="adaln",
    )(c16, w_ada, b_ada.reshape(DEPTH, 1, 6 * D_MODEL))


INPROJ_CW = 640
ROW_TILE = 256

SHIFT1, SCALE1, GATE1, SHIFT2, SCALE2, GATE2 = range(6)


def _mod_spec(mod, T, tm, n_grid_axes=1):
    per_seq = mod.shape[0] > 1
    if n_grid_axes == 1:
        return pl.BlockSpec((None, 6, D_MODEL), lambda i: ((i * tm) // T if per_seq else 0, 0, 0))
    return pl.BlockSpec((None, 6, D_MODEL), lambda i, e: ((i * tm) // T if per_seq else 0, 0, 0))


def _inproj_kernel(x_ref, mod_ref, w_ref, b_ref, z_ref):
    m = mod_ref[...]
    hb = (x_ref[...] * (1.0 + m[SCALE1:SCALE1 + 1]) + m[SHIFT1:SHIFT1 + 1]).astype(BF16)
    for j in range(D_IN_PAD // INPROJ_CW):
        cs = slice(j * INPROJ_CW, (j + 1) * INPROJ_CW)
        z_ref[:, cs] = jnp.dot(hb, w_ref[:, cs], preferred_element_type=F32) + b_ref[:, cs]


def _inproj_call(x, T, mod, w, b):
    n = x.shape[0]
    tm = ROW_TILE
    return pl.pallas_call(
        _inproj_kernel,
        grid=(n // tm,),
        in_specs=[pl.BlockSpec((tm, D_MODEL), lambda i: (i, 0)),
                  _mod_spec(mod, T, tm),
                  pl.BlockSpec((D_MODEL, D_IN_PAD), lambda i: (0, 0)),
                  pl.BlockSpec((1, D_IN_PAD), lambda i: (0, 0))],
        out_specs=pl.BlockSpec((tm, D_IN_PAD), lambda i: (i, 0)),
        out_shape=jax.ShapeDtypeStruct((n, D_IN_PAD), F32),
        compiler_params=_cparams(1),
        name="inproj",
    )(x, mod, w, b)


def _rglru_kernel(xr_ref, gr_ref, cw_ref, cb_ref, wg_ref, bg_ref, lam_ref, h0_ref,
                  y_ref, hT_ref, xs_ref, os_ref):
    T = xr_ref.shape[1]
    P = T + LRU_PAD
    TC = LRU_TC
    nch = T // TC
    CP = 256
    zeros = jnp.zeros((LRU_PAD, LANES), F32)
    for s in range(SUBLANES + 1):
        xs_ref[s * P:s * P + LRU_PAD] = zeros
    for s in range(SUBLANES):
        def copy_in(c, carry, s=s):
            r0 = pl.multiple_of(c * CP, CP)
            xs_ref[pl.ds(s * P + LRU_PAD + r0, CP)] = xr_ref[s, pl.ds(r0, CP)]
            return carry
        lax.fori_loop(0, T // CP, copy_in, 0)

    cw = cw_ref[...]
    cb = cb_ref[...]
    sp = _softplus(-lam_ref[0])

    def gates(t0, d):
        win = [xs_ref[pl.ds(t0 + k + LRU_PAD - 2, SUBLANES, stride=P)] for k in range(TC + 3)]
        xc = jnp.stack([cw[0:1] * win[i] + cw[1:2] * win[i + 1] + cw[2:3] * win[i + 2] + cw[3:4] * win[i + 3] + cb
                        for i in range(TC)])
        x2 = xc.reshape(TC * SUBLANES, LANES)
        g = (jnp.dot(x2.astype(BF16), wg_ref[0, :, d * 256:(d + 1) * 256], preferred_element_type=F32)
             + bg_ref[0, :, d * 256:(d + 1) * 256])
        r = jax.nn.sigmoid(g[:, :LANES])
        i = jax.nn.sigmoid(g[:, LANES:])
        log_a = -LRU_C * r * sp[:, d * LANES:(d + 1) * LANES]
        a = jnp.exp(log_a)
        u = jnp.sqrt(jnp.maximum(1.0 - jnp.exp(2.0 * log_a), 0.0)) * (i * x2)
        return a.reshape(TC, SUBLANES, LANES), u.reshape(TC, SUBLANES, LANES)

    def fwd(c, h):
        t0 = pl.multiple_of(c * TC, TC)
        a, u = gates(t0, 0)
        for i in range(TC):
            h = a[i] * h + u[i]
            os_ref[pl.ds(t0 + i, SUBLANES, stride=P)] = h
        return h

    hT_ref[0] = lax.fori_loop(0, nch, fwd, h0_ref[0])

    def bwd(c, h):
        t0 = pl.multiple_of((nch - 1 - c) * TC, TC)
        a, u = gates(t0, 1)
        for i in range(TC - 1, -1, -1):
            h = a[i] * h + u[i]
            os_ref[pl.ds(t0 + i, SUBLANES, stride=P)] = os_ref[pl.ds(t0 + i, SUBLANES, stride=P)] + h
        return h

    hT_ref[1] = lax.fori_loop(0, nch, bwd, h0_ref[1])

    for s in range(SUBLANES):
        def copy_out(c, carry, s=s):
            r0 = pl.multiple_of(c * CP, CP)
            y_ref[s, pl.ds(r0, CP)] = jax.nn.gelu(gr_ref[s, pl.ds(r0, CP)]) * os_ref[pl.ds(s * P + r0, CP)]
            return carry
        lax.fori_loop(0, T // CP, copy_out, 0)


def _rglru_call(z3, h0, cw, cb, wg, bg, lam):
    B, T, _ = z3.shape
    nb = B // SUBLANES
    ncb = D_GROUP // LANES
    blk = (SUBLANES, T, LANES)
    pitch = T + LRU_PAD
    return pl.pallas_call(
        _rglru_kernel,
        grid=(nb, ncb),
        in_specs=[pl.BlockSpec(blk, lambda b, c: (b, 0, c)),
                  pl.BlockSpec(blk, lambda b, c: (b, 0, ncb + c)),
                  pl.BlockSpec((4, LANES), lambda b, c: (0, c)),
                  pl.BlockSpec((1, LANES), lambda b, c: (0, c)),
                  pl.BlockSpec((1, LANES, 4 * LANES), lambda b, c: (c, 0, 0)),
                  pl.BlockSpec((1, 1, 4 * LANES), lambda b, c: (c, 0, 0)),
                  pl.BlockSpec((1, 1, 2 * LANES), lambda b, c: (c, 0, 0)),
                  pl.BlockSpec((2, SUBLANES, LANES), lambda b, c: (0, b, c))],
        out_specs=[pl.BlockSpec(blk, lambda b, c: (b, 0, c)),
                   pl.BlockSpec((2, SUBLANES, LANES), lambda b, c: (0, b, c))],
        out_shape=[jax.ShapeDtypeStruct((B, T, D_GROUP), F32),
                   jax.ShapeDtypeStruct((2, B, D_GROUP), F32)],
        scratch_shapes=[pltpu.VMEM((SUBLANES * pitch + LRU_PAD, LANES), F32),
                        pltpu.VMEM((SUBLANES * pitch, LANES), F32)],
        compiler_params=_cparams(2),
        name="rglru",
    )(z3, z3, cw, cb, wg, bg, lam, h0)


def _hgrn_kernel(q_ref, ff_ref, fb_ref, v_ref, og_ref, lb_ref, ng_ref, s0_ref,
                 y_ref, sT_ref, st_ref, of_ref, ob_ref):
    T = q_ref.shape[0]
    L = HG_L
    SUB = HG_SUB
    nch = T // L
    nsub = L // SUB
    tril, triu = _tri_masks(L)
    heads = [slice(hd * D_HEAD, (hd + 1) * D_HEAD) for hd in range(N_HEADS)]

    for d in range(2):
        for hd in range(N_HEADS):
            st_ref[d, hd] = s0_ref[d, hd].T

    def step(c, carry):
        jobs = ((0, pl.multiple_of(c * L, L), of_ref), (1, pl.multiple_of((nch - 1 - c) * L, L), ob_ref))
        gated = []
        for d, t0, _ in jobs:
            q = _silu(q_ref[pl.ds(t0, L), :])
            v = v_ref[pl.ds(t0, L), :].astype(BF16)
            lb = lb_ref[d:d + 1, :]
            sig = jax.nn.sigmoid((ff_ref if d == 0 else fb_ref)[pl.ds(t0, L), :])
            log_f = jnp.log(jnp.maximum(lb + (1.0 - lb) * sig, TINY))
            kk = (1.0 - lb) * (1.0 - sig)
            mask = tril if d == 0 else triu
            b = _dot_split3(mask, log_f, mask_on_left=True)
            gated.append((q, v, kk, b, mask))
        factored = []
        for (d, _, _), (q, v, kk, b, mask) in zip(jobs, gated):
            b_tot = b[L - 1:L] if d == 0 else b[0:1]
            subs = []
            for i in range(nsub):
                rows = slice(i * SUB, (i + 1) * SUB)
                if d == 0:
                    cols = slice(0, (i + 1) * SUB)
                    edge = b[i * SUB - 1:i * SUB] if i > 0 else 0.0
                else:
                    cols = slice(i * SUB, L)
                    edge = b[(i + 1) * SUB:(i + 1) * SUB + 1] if i < nsub - 1 else 0.0
                subs.append((rows, cols, (q[rows] * jnp.exp(b[rows] - edge)).astype(BF16),
                             (kk[cols] * jnp.exp(edge - b[cols])).astype(BF16)))
            factored.append(((q * jnp.exp(b)).astype(BF16), (kk * jnp.exp(b_tot - b)).astype(BF16),
                             jnp.exp(b_tot), subs))
        scores = [[[_dot_nt(qs[:, hs], ks[:, hs]) for (_, _, qs, ks) in subs] for hs in heads]
                  for (_, _, _, subs) in factored]
        states = [[st_ref[d, hd] for hd in range(N_HEADS)] for d, _, _ in jobs]
        inter = [[_dot_nt(qdec[:, hs], states[j][hd]) for hd, hs in enumerate(heads)]
                 for j, (qdec, _, _, _) in enumerate(factored)]
        update = [[_dot_tn(gated[j][1][:, hs], kdec[:, hs]) for hs in heads]
                  for j, (_, kdec, _, _) in enumerate(factored)]
        for j, (d, t0, o_ref) in enumerate(jobs):
            v, mask = gated[j][1], gated[j][4]
            dec, subs = factored[j][2], factored[j][3]
            for hd, hs in enumerate(heads):
                intra = [_dot(jnp.where(mask[rows, cols], scores[j][hd][i], 0.0), v[cols, hs])
                         for i, (rows, cols, _, _) in enumerate(subs)]
                o_ref[pl.ds(t0, L), hs] = jnp.concatenate(intra, axis=0) + inter[j][hd]
                st_ref[d, hd] = states[j][hd] * dec[:, hs] + update[j][hd]
        return carry

    lax.fori_loop(0, nch, step, 0)

    def combine(c, carry):
        t0 = pl.multiple_of(c * L, L)
        for hs in heads:
            o = of_ref[pl.ds(t0, L), hs] + ob_ref[pl.ds(t0, L), hs]
            o = o * lax.rsqrt(jnp.mean(jnp.square(o), -1, keepdims=True) + EPS)
            y_ref[pl.ds(t0, L), hs] = o * ng_ref[:, hs] * _silu(og_ref[pl.ds(t0, L), hs])
        return carry

    lax.fori_loop(0, nch, combine, 0)

    for d in range(2):
        for hd in range(N_HEADS):
            sT_ref[d, hd] = st_ref[d, hd].T


def _seq_col_spec(T, j):
    return pl.BlockSpec((T, D_GROUP), lambda b, j=j: (b, j))


def _hgrn_call(z2, B, s0, lower, norm_g):
    T = z2.shape[0] // B
    col = functools.partial(_seq_col_spec, T)
    st_spec = pl.BlockSpec((None, 2, N_HEADS, D_HEAD, D_HEAD), lambda b: (b, 0, 0, 0, 0))
    return pl.pallas_call(
        _hgrn_kernel,
        grid=(B,),
        in_specs=[col(2), col(3), col(4), col(5), col(6),
                  pl.BlockSpec((2, D_GROUP), lambda b: (0, 0)),
                  pl.BlockSpec((1, D_GROUP), lambda b: (0, 0)),
                  st_spec],
        out_specs=[pl.BlockSpec((T, D_GROUP), lambda b: (b, 0)), st_spec],
        out_shape=[jax.ShapeDtypeStruct((B * T, D_GROUP), F32),
                   jax.ShapeDtypeStruct((B, 2, N_HEADS, D_HEAD, D_HEAD), F32)],
        scratch_shapes=[pltpu.VMEM((2, N_HEADS, D_HEAD, D_HEAD), F32),
                        pltpu.VMEM((T, D_GROUP), F32),
                        pltpu.VMEM((T, D_GROUP), F32)],
        compiler_params=_cparams(1),
        name="hgrn2",
    )(z2, z2, z2, z2, z2, lower, norm_g, s0)


def _mlstm_kernel(q_ref, k_ref, v_ref, og_ref, g_ref, ng_ref, c0_ref, n0_ref, m0_ref,
                  y_ref, cT_ref, nT_ref, mT_ref, cn_ref, of_ref, ob_ref):
    T = q_ref.shape[0]
    L = ML_L
    SUB = ML_SUB
    nch = T // L
    nsub = L // SUB
    tril, triu = _tri_masks(L)
    gcol = lax.broadcasted_iota(jnp.int32, (L, N_GATE), 1)
    heads = [slice(hd * D_HEAD, (hd + 1) * D_HEAD) for hd in range(N_HEADS)]
    ones = jnp.ones((L, D_HEAD), BF16)

    for d in range(2):
        for hd in range(N_HEADS):
            cn_ref[d, hd, :, 0:D_HEAD] = c0_ref[d, hd]
            cn_ref[d, hd, :, D_HEAD:] = jnp.broadcast_to(n0_ref[d, hd:hd + 1, :], (D_HEAD, D_HEAD)).T
    mT_ref[...] = m0_ref[...]

    def step(c, carry):
        jobs = ((0, pl.multiple_of(c * L, L), of_ref), (1, pl.multiple_of((nch - 1 - c) * L, L), ob_ref))
        loaded, q_state = [], []
        for d, t0, _ in jobs:
            g = g_ref[pl.ds(t0, L), 0:N_GATE]
            g = jnp.where(gcol % 8 >= N_HEADS, jax.nn.log_sigmoid(g), g)
            bc = _dot_split3(tril if d == 0 else triu, g, mask_on_left=True)
            qb = q_ref[pl.ds(t0, L), :].astype(BF16)
            kf = k_ref[pl.ds(t0, L), :] * (D_HEAD ** -0.5)
            vb = v_ref[pl.ds(t0, L), :].astype(BF16)
            v1 = [jnp.concatenate([vb[:, hs], ones], axis=1) for hs in heads]
            loaded.append((g, bc, qb, kf, v1))
            q_state.append([_dot(qb[:, hs], cn_ref[d, hd]) for hd, hs in enumerate(heads)])
        scaled = []
        for j, (d, _, _) in enumerate(jobs):
            g, bc, qb, kf, v1 = loaded[j]
            per_head = []
            for hd, hs in enumerate(heads):
                ic, fc = d * 8 + hd, d * 8 + N_HEADS + hd
                b_col = bc[:, fc:fc + 1]
                r_col = g[:, ic:ic + 1] - b_col
                b_tot = b_col[L - 1:L] if d == 0 else b_col[0:1]
                m_st = mT_ref[d, hd:hd + 1, 0:1]
                blk_max = [jnp.max(r_col[i * SUB:(i + 1) * SUB], axis=0, keepdims=True) for i in range(nsub)]
                r_rep = jnp.broadcast_to(r_col, (L, D_HEAD))
                subs = []
                for i in range(nsub):
                    rows = slice(i * SUB, (i + 1) * SUB)
                    cols = slice(0, (i + 1) * SUB) if d == 0 else slice(i * SUB, L)
                    seen = blk_max[:i + 1] if d == 0 else blk_max[i:]
                    mu = jnp.maximum(m_st, functools.reduce(jnp.maximum, seen))
                    kp = (kf[cols, hs] * jnp.exp(r_rep[cols] - mu)).astype(BF16)
                    floor = jnp.exp(-b_col[rows] - mu)
                    subs.append((rows, cols, mu, kp, floor))
                per_head.append((b_tot, m_st, subs))
            scaled.append(per_head)
        scores = [[[_dot_nt(loaded[j][2][rows, hs], kp) for (rows, _, _, kp, _) in scaled[j][hd][2]]
                   for hd, hs in enumerate(heads)] for j in range(2)]
        full = [nsub - 1, 0]
        update = [[_dot_tn(scaled[j][hd][2][full[j]][3], loaded[j][4][hd]) for hd in range(N_HEADS)]
                  for j in range(2)]
        for j, (d, t0, o_ref) in enumerate(jobs):
            mask = tril if d == 0 else triu
            for hd, hs in enumerate(heads):
                b_tot, m_st, subs = scaled[j][hd]
                outs = []
                for i, (rows, cols, mu, _, floor) in enumerate(subs):
                    w = jnp.where(mask[rows, cols], scores[j][hd][i], 0.0)
                    nd = _dot(w, loaded[j][4][hd][cols]) + jnp.exp(m_st - mu) * q_state[j][hd][rows]
                    outs.append(nd[:, 0:D_HEAD] / jnp.maximum(jnp.abs(nd[:, D_HEAD:]), floor))
                o_ref[pl.ds(t0, L), hs] = jnp.concatenate(outs, axis=0)
                mu_full = subs[full[j]][2]
                cn_ref[d, hd] = jnp.exp(m_st - mu_full) * cn_ref[d, hd] + update[j][hd]
                mT_ref[d, hd:hd + 1, :] = jnp.broadcast_to(b_tot + mu_full, (1, LANES))
        return carry

    lax.fori_loop(0, nch, step, 0)

    for d in range(2):
        for hd in range(N_HEADS):
            cT_ref[d, hd] = cn_ref[d, hd, :, 0:D_HEAD]
            nT_ref[d, hd:hd + 1, :] = cn_ref[d, hd, :, D_HEAD:].T[0:1, :]

    def combine(c, carry):
        t0 = pl.multiple_of(c * L, L)
        for hs in heads:
            o = of_ref[pl.ds(t0, L), hs] + ob_ref[pl.ds(t0, L), hs]
            mu = jnp.mean(o, -1, keepdims=True)
            var = jnp.mean(jnp.square(o - mu), -1, keepdims=True)
            o = (o - mu) * lax.rsqrt(var + EPS)
            y_ref[pl.ds(t0, L), hs] = jax.nn.sigmoid(og_ref[pl.ds(t0, L), hs]) * (o * ng_ref[:, hs])
        return carry

    lax.fori_loop(0, nch, combine, 0)


def _mlstm_call(z2, B, c0, n0, m0, norm_g):
    T = z2.shape[0] // B
    col = functools.partial(_seq_col_spec, T)
    c_spec = pl.BlockSpec((None, 2, N_HEADS, D_HEAD, D_HEAD), lambda b: (b, 0, 0, 0, 0))
    v_spec = pl.BlockSpec((None, 2, N_HEADS, D_HEAD), lambda b: (b, 0, 0, 0))
    return pl.pallas_call(
        _mlstm_kernel,
        grid=(B,),
        in_specs=[col(7), col(8), col(9), col(10),
                  pl.BlockSpec((T, LANES), lambda b: (b, GATE_BLK)),
                  pl.BlockSpec((1, D_GROUP), lambda b: (0, 0)),
                  c_spec, v_spec, v_spec],
        out_specs=[pl.BlockSpec((T, D_GROUP), lambda b: (b, 0)), c_spec, v_spec, v_spec],
        out_shape=[jax.ShapeDtypeStruct((B * T, D_GROUP), F32),
                   jax.ShapeDtypeStruct((B, 2, N_HEADS, D_HEAD, D_HEAD), F32),
                   jax.ShapeDtypeStruct((B, 2, N_HEADS, D_HEAD), F32),
                   jax.ShapeDtypeStruct((B, 2, N_HEADS, D_HEAD), F32)],
        scratch_shapes=[pltpu.VMEM((2, N_HEADS, D_HEAD, 2 * D_HEAD), F32),
                        pltpu.VMEM((T, D_GROUP), F32), pltpu.VMEM((T, D_GROUP), F32)],
        compiler_params=_cparams(1),
        name="mlstm",
    )(z2, z2, z2, z2, z2, norm_g, c0, n0, m0)


def _layer_norm(v, g, b):
    mu = jnp.mean(v, -1, keepdims=True)
    var = jnp.mean(jnp.square(v - mu), -1, keepdims=True)
    return (v - mu) * lax.rsqrt(var + EPS) * g + b


def _route(logits_t, b_router):
    mx = jnp.max(logits_t, axis=0, keepdims=True)
    ex = jnp.exp(logits_t - mx)
    scores = ex / jnp.sum(ex, axis=0, keepdims=True)
    sel = scores + b_router
    rows = [sel[e:e + 1] for e in range(N_EXPERTS)]
    first, second, grp = [], [], []
    for gi in range(N_GROUPS):
        r = rows[gi * EXPERTS_PER_GROUP:(gi + 1) * EXPERTS_PER_GROUP]
        m1 = functools.reduce(jnp.maximum, r)
        taken = jnp.zeros_like(m1, dtype=jnp.bool_)
        f = []
        for x in r:
            hit = jnp.logical_and(x == m1, jnp.logical_not(taken))
            f.append(hit)
            taken = jnp.logical_or(taken, hit)
        rest = [jnp.where(fi, -jnp.inf, x) for fi, x in zip(f, r)]
        m2 = functools.reduce(jnp.maximum, rest)
        taken = jnp.zeros_like(m1, dtype=jnp.bool_)
        s = []
        for x in rest:
            hit = jnp.logical_and(x == m2, jnp.logical_not(taken))
            s.append(hit)
            taken = jnp.logical_or(taken, hit)
        first.append(f)
        second.append(s)
        grp.append(m1 + m2)
    gmax = functools.reduce(jnp.maximum, grp)
    taken = jnp.zeros_like(gmax, dtype=jnp.bool_)
    chosen = []
    group = jnp.zeros_like(gmax, dtype=jnp.int32)
    for gi in range(N_GROUPS):
        best = jnp.logical_and(grp[gi] == gmax, jnp.logical_not(taken))
        taken = jnp.logical_or(taken, best)
        group = jnp.where(best, gi, group)
        for j in range(EXPERTS_PER_GROUP):
            chosen.append(jnp.logical_and(best, jnp.logical_or(first[gi][j], second[gi][j])))
    picked = [jnp.where(ch, scores[e:e + 1], 0.0) for e, ch in enumerate(chosen)]
    denom = functools.reduce(lambda a, b: a + b, picked)
    return jnp.concatenate([p / denom for p in picked], axis=0), group


D_DISPATCH = D_MODEL + LANES


def _outproj_kernel(yr_ref, yh_ref, ym_ref, x_ref, mod_ref, w_ref, lg_ref, lb_ref,
                    wr_ref, br_ref, x1_ref, hx_ref, grp_ref):
    tm = x_ref.shape[0]
    m = mod_ref[...]
    y = (jnp.dot(yr_ref[...].astype(BF16), w_ref[0:D_GROUP], preferred_element_type=F32)
         + jnp.dot(yh_ref[...].astype(BF16), w_ref[D_GROUP:2 * D_GROUP], preferred_element_type=F32)
         + jnp.dot(ym_ref[...].astype(BF16), w_ref[2 * D_GROUP:], preferred_element_type=F32))
    x1 = _layer_norm(ALPHA * x_ref[...] + m[GATE1:GATE1 + 1] * y, lg_ref[...], lb_ref[...])
    x1_ref[...] = x1
    hm = x1 * (1.0 + m[SCALE2:SCALE2 + 1]) + m[SHIFT2:SHIFT2 + 1]
    hx_ref[:, 0:D_MODEL] = hm
    logits_t = lax.dot_general(wr_ref[...], hm, (((1,), (1,)), ((), ())), precision=HIGHEST,
                               preferred_element_type=F32)
    cmb_t, group = _route(logits_t, br_ref[...])
    cmb_rows = jnp.concatenate([cmb_t, jnp.zeros((LANES - N_EXPERTS, tm), F32)], axis=0)
    hx_ref[:, D_MODEL:] = cmb_rows.T
    grp_ref[...] = group


def _outproj_call(yr, yh, ym, x, T, mod, w_out, ln_g, ln_b, w_router_t, b_router):
    n = x.shape[0]
    tm = ROW_TILE
    row = lambda w: pl.BlockSpec((tm, w), lambda i: (i, 0))
    full = lambda a: pl.BlockSpec(a.shape, lambda i: (0,) * a.ndim)
    return pl.pallas_call(
        _outproj_kernel,
        grid=(n // tm,),
        in_specs=[row(D_GROUP), row(D_GROUP), row(D_GROUP), row(D_MODEL), _mod_spec(mod, T, tm),
                  full(w_out), full(ln_g), full(ln_b), full(w_router_t), full(b_router)],
        out_specs=[row(D_MODEL), row(D_DISPATCH), pl.BlockSpec((1, tm), lambda i: (0, i))],
        out_shape=[jax.ShapeDtypeStruct((n, D_MODEL), F32),
                   jax.ShapeDtypeStruct((n, D_DISPATCH), F32),
                   jax.ShapeDtypeStruct((1, n), jnp.int32)],
        compiler_params=_cparams(1),
        name="outproj_ln_router",
    )(yr, yh, ym, x, mod, w_out, ln_g, ln_b, w_router_t, b_router)


MOE_TILE = 256


def _row_gather(idx_ref, first, n_rows, src_hbm, dst_ref, sem):
    def body(r, carry):
        pltpu.make_async_copy(src_hbm.at[pl.ds(idx_ref[first + r], 1)], dst_ref.at[pl.ds(r, 1)], sem).start()
        return carry
    lax.fori_loop(0, n_rows, body, 0, unroll=8)


def _row_gather_wait(n_rows, src_hbm, dst_ref, sem):
    def body(r, carry):
        pltpu.make_async_copy(src_hbm.at[pl.ds(0, 1)], dst_ref.at[pl.ds(0, 1)], sem).wait()
        return carry
    lax.fori_loop(0, n_rows, body, 0, unroll=8)


def _gathered_tile(idx_ref, src_hbm, buf_ref, sem_ref, tm):
    i = pl.program_id(0)
    slot = lax.rem(i, 2)

    @pl.when(i == 0)
    def _():
        _row_gather(idx_ref, 0, tm, src_hbm, buf_ref.at[0], sem_ref.at[0])

    @pl.when(i + 1 < pl.num_programs(0))
    def _():
        _row_gather(idx_ref, (i + 1) * tm, tm, src_hbm, buf_ref.at[1 - slot], sem_ref.at[1 - slot])

    _row_gather_wait(tm, src_hbm, buf_ref.at[slot], sem_ref.at[slot])
    return buf_ref[slot]


def _moe_kernel(tile_group_ref, src_ref, hx_hbm, wg_ref, wu_ref, wd_ref, o_ref, buf_ref, sem_ref):
    tm = o_ref.shape[0]
    rows = _gathered_tile(src_ref, hx_hbm, buf_ref, sem_ref, tm)
    hm = rows[:, 0:D_MODEL].astype(BF16)
    cmb = rows[:, D_MODEL:]
    lane = lax.broadcasted_iota(jnp.int32, cmb.shape, 1)
    first_expert = tile_group_ref[pl.program_id(0)] * EXPERTS_PER_GROUP
    acc = None
    for k in range(EXPERTS_PER_GROUP):
        ce = jnp.sum(jnp.where(lane == first_expert + k, cmb, 0.0), axis=1, keepdims=True)
        hg = jnp.dot(hm, wg_ref[k], preferred_element_type=F32)
        hu = jnp.dot(hm, wu_ref[k], preferred_element_type=F32)
        part = jnp.dot((_silu(hg) * hu * ce).astype(BF16), wd_ref[k], preferred_element_type=F32)
        acc = part if acc is None else acc + part
    o_ref[...] = acc


def _moe_call(hx, tile_group, src, w_gate, w_up, w_down):
    n_pad = src.shape[0]
    tm = MOE_TILE
    grp_w = lambda shape: pl.BlockSpec(shape, lambda i, tg, sr: (tg[i], 0, 0))
    return pl.pallas_call(
        _moe_kernel,
        grid_spec=pltpu.PrefetchScalarGridSpec(
            num_scalar_prefetch=2,
            grid=(n_pad // tm,),
            in_specs=[pl.BlockSpec(memory_space=pl.ANY),
                      grp_w((EXPERTS_PER_GROUP, D_MODEL, D_FF)), grp_w((EXPERTS_PER_GROUP, D_MODEL, D_FF)),
                      grp_w((EXPERTS_PER_GROUP, D_FF, D_MODEL))],
            out_specs=pl.BlockSpec((tm, D_MODEL), lambda i, tg, sr: (i, 0)),
            scratch_shapes=[pltpu.VMEM((2, tm, D_DISPATCH), F32), pltpu.SemaphoreType.DMA((2,))]),
        out_shape=jax.ShapeDtypeStruct((n_pad, D_MODEL), F32),
        compiler_params=_cparams(1),
        name="moe_sorted",
    )(tile_group, src, hx, w_gate, w_up, w_down)


def _ln2_kernel(pos_ref, f_hbm, x1_ref, mod_ref, lg_ref, lb_ref, o_ref, buf_ref, sem_ref):
    tm = o_ref.shape[0]
    f = _gathered_tile(pos_ref, f_hbm, buf_ref, sem_ref, tm)
    v = ALPHA * x1_ref[...] + mod_ref[GATE2:GATE2 + 1, :] * f
    o_ref[...] = _layer_norm(v, lg_ref[...], lb_ref[...])


def _ln2_call(f_sorted, pos, x1, T, mod, ln_g, ln_b):
    n = x1.shape[0]
    tm = ROW_TILE
    per_seq = mod.shape[0] > 1
    return pl.pallas_call(
        _ln2_kernel,
        grid_spec=pltpu.PrefetchScalarGridSpec(
            num_scalar_prefetch=1,
            grid=(n // tm,),
            in_specs=[pl.BlockSpec(memory_space=pl.ANY),
                      pl.BlockSpec((tm, D_MODEL), lambda i, ps: (i, 0)),
                      pl.BlockSpec((None, 6, D_MODEL), lambda i, ps: ((i * tm) // T if per_seq else 0, 0, 0)),
                      pl.BlockSpec((1, D_MODEL), lambda i, ps: (0, 0)),
                      pl.BlockSpec((1, D_MODEL), lambda i, ps: (0, 0))],
            out_specs=pl.BlockSpec((tm, D_MODEL), lambda i, ps: (i, 0)),
            scratch_shapes=[pltpu.VMEM((2, tm, D_MODEL), F32), pltpu.SemaphoreType.DMA((2,))]),
        out_shape=jax.ShapeDtypeStruct((n, D_MODEL), F32),
        compiler_params=_cparams(1),
        name="moe_combine_ln",
    )(pos, f_sorted, x1, mod, ln_g, ln_b)


def _dispatch_plan(group, tm):
    n = group.shape[0]
    n_pad = n + N_GROUPS * tm
    onehot = (group[:, None] == jnp.arange(N_GROUPS, dtype=jnp.int32)[None, :]).astype(jnp.int32)
    csum = jnp.cumsum(onehot, axis=0)
    rank = jnp.take_along_axis(csum, group[:, None], axis=1)[:, 0] - 1
    padded = ((csum[-1] + tm - 1) // tm) * tm
    ends = jnp.cumsum(padded)
    pos = (ends - padded)[group] + rank
    src = jnp.zeros((n_pad,), jnp.int32).at[pos].set(jnp.arange(n, dtype=jnp.int32))
    tile_start = jnp.arange(n_pad // tm, dtype=jnp.int32) * tm
    tile_group = jnp.sum((tile_start[:, None] >= ends[None, :]).astype(jnp.int32), axis=1)
    tile_group = jnp.minimum(tile_group, N_GROUPS - 1)
    return pos, src, tile_group


def _grid_pos_embed(n_tokens):
    rows = n_tokens // GRID_W
    r = jnp.repeat(jnp.arange(rows, dtype=F32), GRID_W)
    c = jnp.tile(jnp.arange(GRID_W, dtype=F32), rows)
    q = D_MODEL // 4
    freq = jnp.exp(-jnp.log(10000.0) * jnp.arange(q, dtype=F32) / q)
    ar = r[:, None] * freq
    ac = c[:, None] * freq
    return jnp.concatenate([jnp.sin(ar), jnp.cos(ar), jnp.sin(ac), jnp.cos(ac)], axis=-1)


def _lru_gate_params(wa, ba, wx, bx, lam):
    ncb = D_GROUP // LANES

    def dense(w):
        z = jnp.zeros((ncb, LANES, LANES), F32)
        z = z.at[:, :LRU_BLOCK, :LRU_BLOCK].set(w[0::2])
        return z.at[:, LRU_BLOCK:, LRU_BLOCK:].set(w[1::2])

    wg = jnp.concatenate([dense(wa[0]), dense(wx[0]), dense(wa[1]), dense(wx[1])], axis=-1).astype(BF16)
    per_blk = lambda v: v.reshape(ncb, 1, LANES)
    bg = jnp.concatenate([per_blk(ba[0]), per_blk(bx[0]), per_blk(ba[1]), per_blk(bx[1])], axis=-1)
    lm = jnp.concatenate([per_blk(lam[0]), per_blk(lam[1])], axis=-1)
    return wg, bg, lm


def _mixer_states(B, l, states):
    if states is None:
        return (jnp.zeros((2, B, D_GROUP), F32),
                jnp.zeros((B, 2, N_HEADS, D_HEAD, D_HEAD), F32),
                jnp.zeros((B, 2, N_HEADS, D_HEAD, D_HEAD), F32),
                jnp.zeros((B, 2, N_HEADS, D_HEAD), F32),
                jnp.zeros((B, 2, N_HEADS, D_HEAD), F32))
    h, s, c, n, m = states
    return (jnp.swapaxes(h[:, l], 0, 1), s[:, l], c[:, l], n[:, l],
            jnp.broadcast_to(m[:, l][..., None], (B, 2, N_HEADS, D_HEAD)))


def kernel(x_prompt, x_sample, state_lru_h, state_hgrn_S, state_mlstm_C, state_mlstm_n, state_mlstm_m,
           c, c_ctx, w_ada, b_ada, w_in, b_in, conv_w, conv_b, lru_wa, lru_ba, lru_wx, lru_bx, lru_lam,
           hg_lb, hg_norm_g, ml_norm_g, w_out, ln1_g, ln1_b, ln2_g, ln2_b,
           w_router, b_router, w_gate, w_up, w_down):
    Bp, Tp, _ = x_prompt.shape
    Bs, Ts, _ = x_sample.shape
    assert Bs % SUBLANES == 0 and Bp % SUBLANES == 0 and Bs + 1 <= 16
    assert Tp % ROW_TILE == 0 and Ts % ROW_TILE == 0

    lb_soft = jax.nn.softmax(hg_lb.astype(F32), axis=0)
    hg_lower = jnp.cumsum(lb_soft, axis=0) - lb_soft[0:1]

    c16 = jnp.concatenate([c, c_ctx[None], jnp.zeros((16 - Bs - 1, D_MODEL), F32)], axis=0)
    mod = _ada_call(c16, w_ada, b_ada)

    w_in_p = jnp.pad(w_in, ((0, 0), (0, 0), (0, D_IN_PAD - D_IN))).astype(BF16)
    b_in_p = jnp.pad(b_in, ((0, 0), (0, D_IN_PAD - D_IN))).reshape(DEPTH, 1, D_IN_PAD)
    w_out_b = w_out.astype(BF16)
    w_gate_b, w_up_b, w_down_b = w_gate.astype(BF16), w_up.astype(BF16), w_down.astype(BF16)
    w_router_t = w_router.T
    b_router_c = b_router.reshape(N_EXPERTS, 1)

    xs = x_sample + _grid_pos_embed(Ts).astype(x_sample.dtype)
    streams = [
        dict(x=x_prompt.reshape(Bp * Tp, D_MODEL), T=Tp, B=Bp, states=None),
        dict(x=xs.reshape(Bs * Ts, D_MODEL), T=Ts, B=Bs,
             states=(state_lru_h, state_hgrn_S, state_mlstm_C, state_mlstm_n, state_mlstm_m)),
    ]
    finals = []
    for l in range(DEPTH):
        wg, bg, lm = _lru_gate_params(lru_wa[l], lru_ba[l], lru_wx[l], lru_bx[l], lru_lam[l])
        row = lambda v: v[l].reshape(1, -1)
        for si, st in enumerate(streams):
            T, B = st["T"], st["B"]
            mod_s = (mod[l, :Bs] if si == 1 else mod[l, Bs:Bs + 1]).reshape(-1, 6, D_MODEL)
            h0, s0, c0, n0, m0 = _mixer_states(B, l, st["states"])
            z = _inproj_call(st["x"], T, mod_s, w_in_p[l], b_in_p[l])
            y_r, h_T = _rglru_call(z.reshape(B, T, D_IN_PAD), h0, conv_w[l], row(conv_b), wg, bg, lm)
            y_h, s_T = _hgrn_call(z, B, s0, hg_lower[l], row(hg_norm_g))
            y_m, c_T, n_T, m_T = _mlstm_call(z, B, c0, n0, m0, row(ml_norm_g))
            x1, hx, group = _outproj_call(y_r.reshape(B * T, D_GROUP), y_h, y_m, st["x"], T, mod_s, w_out_b[l],
                                          row(ln1_g), row(ln1_b), w_router_t, b_router_c)
            pos, src, tile_group = _dispatch_plan(group[0], MOE_TILE)
            f_sorted = _moe_call(hx, tile_group, src, w_gate_b[l], w_up_b[l], w_down_b[l])
            st["x"] = _ln2_call(f_sorted, pos, x1, T, mod_s, row(ln2_g), row(ln2_b))
            if si == 0:
                finals.append((jnp.swapaxes(h_T, 0, 1), s_T, c_T, n_T, m_T[..., 0]))
    outs = [st["x"].reshape(st["B"], st["T"], D_MODEL) for st in streams]
    stack = lambda i: jnp.stack([f[i] for f in finals], axis=1)
    return (outs[0], outs[1], stack(0), stack(1), stack(2), stack(3), stack(4))
```

```python
import functools

import jax
import jax.numpy as jnp
from jax import lax
from jax.experimental import pallas as pl
from jax.experimental.pallas import tpu as pltpu

F32 = jnp.float32
BF16 = jnp.bfloat16
HIGHEST = lax.Precision.HIGHEST

D_MODEL = 1024
DEPTH = 2
GRID_W = 64
D_GROUP = 512
D_MIX = 3 * D_GROUP
LRU_BLOCKS = 8
LRU_BLOCK = D_GROUP // LRU_BLOCKS
LRU_C = 8.0
N_HEADS = 4
D_HEAD = D_GROUP // N_HEADS
N_EXPERTS = 16
N_GROUPS = 4
EXPERTS_PER_GROUP = N_EXPERTS // N_GROUPS
D_FF = 512
ALPHA = (2.0 * DEPTH) ** 0.25
EPS = 1e-5
NEG = -1e30
TINY = 1e-30
N_GATE = 4 * N_HEADS
D_IN = 11 * D_GROUP + N_GATE
D_IN_PAD = 45 * 128
GATE_BLK = (11 * D_GROUP) // 128

SUBLANES = 8
LANES = 128
VMEM_LIMIT = 56 * 1024 * 1024

LRU_TC = 32
LRU_PAD = SUBLANES
HG_L = 64
HG_SUB = 16
ML_L = 128
ML_SUB = 32


def _cparams(n_axes):
    return pltpu.CompilerParams(dimension_semantics=("arbitrary",) * n_axes,
                                vmem_limit_bytes=VMEM_LIMIT)


def _dot(a, b):
    return jnp.dot(a.astype(BF16), b.astype(BF16), preferred_element_type=F32)


def _dot_nt(a, b):
    return lax.dot_general(a.astype(BF16), b.astype(BF16), (((1,), (1,)), ((), ())),
                           preferred_element_type=F32)


def _dot_tn(a, b):
    return lax.dot_general(a.astype(BF16), b.astype(BF16), (((0,), (0,)), ((), ())),
                           preferred_element_type=F32)


def _dot_f32(a, b):
    return jnp.dot(a, b, precision=HIGHEST, preferred_element_type=F32)


def _dot_split3(mask, x, mask_on_left):
    m = jnp.where(mask, 1.0, 0.0).astype(BF16)
    hi = x.astype(BF16)
    r1 = x - hi.astype(F32)
    mid = r1.astype(BF16)
    lo = (r1 - mid.astype(F32)).astype(BF16)
    mm = (lambda p: jnp.dot(m, p, preferred_element_type=F32)) if mask_on_left else (
        lambda p: jnp.dot(p, m, preferred_element_type=F32))
    return mm(hi) + mm(mid) + mm(lo)


def _softplus(x):
    return jnp.maximum(x, 0.0) + jnp.log1p(jnp.exp(-jnp.abs(x)))


def _silu(x):
    return x * jax.nn.sigmoid(x)


def _tri_masks(n):
    r = lax.broadcasted_iota(jnp.int32, (n, n), 0)
    c = lax.broadcasted_iota(jnp.int32, (n, n), 1)
    return r >= c, r <= c


def _ada_kernel(c_ref, w_ref, b_ref, o_ref):
    o_ref[0] = _dot_f32(_silu(c_ref[...]), w_ref[0]) + b_ref[0]


def _ada_call(c16, w_ada, b_ada):
    tn = 1536
    return pl.pallas_call(
        _ada_kernel,
        grid=(DEPTH, 6 * D_MODEL // tn),
        in_specs=[pl.BlockSpec((16, D_MODEL), lambda l, j: (0, 0)),
                  pl.BlockSpec((1, D_MODEL, tn), lambda l, j: (l, 0, j)),
                  pl.BlockSpec((1, 1, tn), lambda l, j: (l, 0, j))],
        out_specs=pl.BlockSpec((1, 16, tn), lambda l, j: (l, 0, j)),
        out_shape=jax.ShapeDtypeStruct((DEPTH, 16, 6 * D_MODEL), F32),
        compiler_params=_cparams(2),
        name="adaln",
    )(c16, w_ada, b_ada.reshape(DEPTH, 1, 6 * D_MODEL))


INPROJ_CW = 640
ROW_TILE = 256
OUTPROJ_TILE = 512

SHIFT1, SCALE1, GATE1, SHIFT2, SCALE2, GATE2 = range(6)


def _mod_spec(mod, T, tm, n_grid_axes=1):
    per_seq = mod.shape[0] > 1
    if n_grid_axes == 1:
        return pl.BlockSpec((None, 6, D_MODEL), lambda i: ((i * tm) // T if per_seq else 0, 0, 0))
    return pl.BlockSpec((None, 6, D_MODEL), lambda i, e: ((i * tm) // T if per_seq else 0, 0, 0))


def _inproj_kernel(x_ref, mod_ref, w_ref, b_ref, z_ref):
    m = mod_ref[...]
    hb = (x_ref[...] * (1.0 + m[SCALE1:SCALE1 + 1]) + m[SHIFT1:SHIFT1 + 1]).astype(BF16)
    for j in range(D_IN_PAD // INPROJ_CW):
        cs = slice(j * INPROJ_CW, (j + 1) * INPROJ_CW)
        z_ref[:, cs] = jnp.dot(hb, w_ref[:, cs], preferred_element_type=F32) + b_ref[:, cs]


def _inproj_call(x, T, mod, w, b, l):
    n = x.shape[0]
    tm = ROW_TILE
    return pl.pallas_call(
        _inproj_kernel,
        grid=(n // tm,),
        in_specs=[pl.BlockSpec((tm, D_MODEL), lambda i: (i, 0)),
                  _mod_spec(mod, T, tm),
                  pl.BlockSpec((None, D_MODEL, D_IN_PAD), lambda i: (l, 0, 0)),
                  pl.BlockSpec((None, 1, D_IN_PAD), lambda i: (l, 0, 0))],
        out_specs=pl.BlockSpec((tm, D_IN_PAD), lambda i: (i, 0)),
        out_shape=jax.ShapeDtypeStruct((n, D_IN_PAD), F32),
        compiler_params=_cparams(1),
        name="inproj",
    )(x, mod, w, b)


def _rglru_kernel(xr_ref, gr_ref, cw_ref, cb_ref, wg_ref, bg_ref, lam_ref, h0_ref,
                  y_ref, hT_ref, xs_ref, os_ref):
    T = xr_ref.shape[1]
    P = T + LRU_PAD
    TC = LRU_TC
    nch = T // TC
    CP = 256
    zeros = jnp.zeros((LRU_PAD, LANES), F32)
    for s in range(SUBLANES + 1):
        xs_ref[s * P:s * P + LRU_PAD] = zeros
    for s in range(SUBLANES):
        def copy_in(c, carry, s=s):
            r0 = pl.multiple_of(c * CP, CP)
            xs_ref[pl.ds(s * P + LRU_PAD + r0, CP)] = xr_ref[s, pl.ds(r0, CP)]
            return carry
        lax.fori_loop(0, T // CP, copy_in, 0)

    cw = cw_ref[...]
    cb = cb_ref[...]
    sp = _softplus(-lam_ref[0])

    def gates(t0, d):
        win = [xs_ref[pl.ds(t0 + k + LRU_PAD - 2, SUBLANES, stride=P)] for k in range(TC + 3)]
        xc = jnp.stack([cw[0:1] * win[i] + cw[1:2] * win[i + 1] + cw[2:3] * win[i + 2] + cw[3:4] * win[i + 3] + cb
                        for i in range(TC)])
        x2 = xc.reshape(TC * SUBLANES, LANES)
        g = (jnp.dot(x2.astype(BF16), wg_ref[0, :, d * 256:(d + 1) * 256], preferred_element_type=F32)
             + bg_ref[0, :, d * 256:(d + 1) * 256])
        r = jax.nn.sigmoid(g[:, :LANES])
        i = jax.nn.sigmoid(g[:, LANES:])
        log_a = -LRU_C * r * sp[:, d * LANES:(d + 1) * LANES]
        a = jnp.exp(log_a)
        u = jnp.sqrt(jnp.maximum(1.0 - jnp.exp(2.0 * log_a), 0.0)) * (i * x2)
        return a.reshape(TC, SUBLANES, LANES), u.reshape(TC, SUBLANES, LANES)

    def fwd(c, h):
        t0 = pl.multiple_of(c * TC, TC)
        a, u = gates(t0, 0)
        for i in range(TC):
            h = a[i] * h + u[i]
            os_ref[pl.ds(t0 + i, SUBLANES, stride=P)] = h
        return h

    hT_ref[0] = lax.fori_loop(0, nch, fwd, h0_ref[0])

    def bwd(c, h):
        t0 = pl.multiple_of((nch - 1 - c) * TC, TC)
        a, u = gates(t0, 1)
        for i in range(TC - 1, -1, -1):
            h = a[i] * h + u[i]
            os_ref[pl.ds(t0 + i, SUBLANES, stride=P)] = os_ref[pl.ds(t0 + i, SUBLANES, stride=P)] + h
        return h

    hT_ref[1] = lax.fori_loop(0, nch, bwd, h0_ref[1])

    for s in range(SUBLANES):
        def copy_out(c, carry, s=s):
            r0 = pl.multiple_of(c * CP, CP)
            y_ref[s, pl.ds(r0, CP)] = jax.nn.gelu(gr_ref[s, pl.ds(r0, CP)]) * os_ref[pl.ds(s * P + r0, CP)]
            return carry
        lax.fori_loop(0, T // CP, copy_out, 0)


def _rglru_call(z3, h0, cw, cb, wg, bg, lam):
    B, T, _ = z3.shape
    nb = B // SUBLANES
    ncb = D_GROUP // LANES
    blk = (SUBLANES, T, LANES)
    pitch = T + LRU_PAD
    return pl.pallas_call(
        _rglru_kernel,
        grid=(nb, ncb),
        in_specs=[pl.BlockSpec(blk, lambda b, c: (b, 0, c)),
                  pl.BlockSpec(blk, lambda b, c: (b, 0, ncb + c)),
                  pl.BlockSpec((4, LANES), lambda b, c: (0, c)),
                  pl.BlockSpec((1, LANES), lambda b, c: (0, c)),
                  pl.BlockSpec((1, LANES, 4 * LANES), lambda b, c: (c, 0, 0)),
                  pl.BlockSpec((1, 1, 4 * LANES), lambda b, c: (c, 0, 0)),
                  pl.BlockSpec((1, 1, 2 * LANES), lambda b, c: (c, 0, 0)),
                  pl.BlockSpec((2, SUBLANES, LANES), lambda b, c: (0, b, c))],
        out_specs=[pl.BlockSpec(blk, lambda b, c: (b, 0, c)),
                   pl.BlockSpec((2, SUBLANES, LANES), lambda b, c: (0, b, c))],
        out_shape=[jax.ShapeDtypeStruct((B, T, D_GROUP), F32),
                   jax.ShapeDtypeStruct((2, B, D_GROUP), F32)],
        scratch_shapes=[pltpu.VMEM((SUBLANES * pitch + LRU_PAD, LANES), F32),
                        pltpu.VMEM((SUBLANES * pitch, LANES), F32)],
        compiler_params=_cparams(2),
        name="rglru",
    )(z3, z3, cw, cb, wg, bg, lam, h0)


def _hgrn_kernel(q_ref, ff_ref, fb_ref, v_ref, og_ref, lb_ref, ng_ref, s0_ref,
                 y_ref, sT_ref, st_ref, of_ref, ob_ref):
    T = q_ref.shape[0]
    L = HG_L
    SUB = HG_SUB
    nch = T // L
    nsub = L // SUB
    tril, triu = _tri_masks(L)
    heads = [slice(hd * D_HEAD, (hd + 1) * D_HEAD) for hd in range(N_HEADS)]

    for d in range(2):
        for hd in range(N_HEADS):
            st_ref[d, hd] = s0_ref[d, hd].T

    def step(c, carry):
        jobs = ((0, pl.multiple_of(c * L, L), of_ref), (1, pl.multiple_of((nch - 1 - c) * L, L), ob_ref))
        gated = []
        for d, t0, _ in jobs:
            q = _silu(q_ref[pl.ds(t0, L), :])
            v = v_ref[pl.ds(t0, L), :].astype(BF16)
            lb = lb_ref[d:d + 1, :]
            sig = jax.nn.sigmoid((ff_ref if d == 0 else fb_ref)[pl.ds(t0, L), :])
            log_f = jnp.log(jnp.maximum(lb + (1.0 - lb) * sig, TINY))
            kk = (1.0 - lb) * (1.0 - sig)
            mask = tril if d == 0 else triu
            b = _dot_split3(mask, log_f, mask_on_left=True)
            gated.append((q, v, kk, b, mask))
        factored = []
        for (d, _, _), (q, v, kk, b, mask) in zip(jobs, gated):
            b_tot = b[L - 1:L] if d == 0 else b[0:1]
            subs = []
            for i in range(nsub):
                rows = slice(i * SUB, (i + 1) * SUB)
                if d == 0:
                    cols = slice(0, (i + 1) * SUB)
                    edge = b[i * SUB - 1:i * SUB] if i > 0 else 0.0
                else:
                    cols = slice(i * SUB, L)
                    edge = b[(i + 1) * SUB:(i + 1) * SUB + 1] if i < nsub - 1 else 0.0
                subs.append((rows, cols, (q[rows] * jnp.exp(b[rows] - edge)).astype(BF16),
                             (kk[cols] * jnp.exp(edge - b[cols])).astype(BF16)))
            factored.append(((q * jnp.exp(b)).astype(BF16), (kk * jnp.exp(b_tot - b)).astype(BF16),
                             jnp.exp(b_tot), subs))
        scores = [[[_dot_nt(qs[:, hs], ks[:, hs]) for (_, _, qs, ks) in subs] for hs in heads]
                  for (_, _, _, subs) in factored]
        states = [[st_ref[d, hd] for hd in range(N_HEADS)] for d, _, _ in jobs]
        inter = [[_dot_nt(qdec[:, hs], states[j][hd]) for hd, hs in enumerate(heads)]
                 for j, (qdec, _, _, _) in enumerate(factored)]
        update = [[_dot_tn(gated[j][1][:, hs], kdec[:, hs]) for hs in heads]
                  for j, (_, kdec, _, _) in enumerate(factored)]
        for j, (d, t0, o_ref) in enumerate(jobs):
            v, mask = gated[j][1], gated[j][4]
            dec, subs = factored[j][2], factored[j][3]
            for hd, hs in enumerate(heads):
                intra = [_dot(jnp.where(mask[rows, cols], scores[j][hd][i], 0.0), v[cols, hs])
                         for i, (rows, cols, _, _) in enumerate(subs)]
                o_ref[pl.ds(t0, L), hs] = jnp.concatenate(intra, axis=0) + inter[j][hd]
                st_ref[d, hd] = states[j][hd] * dec[:, hs] + update[j][hd]
        return carry

    lax.fori_loop(0, nch, step, 0)

    def combine(c, carry):
        t0 = pl.multiple_of(c * L, L)
        for hs in heads:
            o = of_ref[pl.ds(t0, L), hs] + ob_ref[pl.ds(t0, L), hs]
            o = o * lax.rsqrt(jnp.mean(jnp.square(o), -1, keepdims=True) + EPS)
            y_ref[pl.ds(t0, L), hs] = o * ng_ref[:, hs] * _silu(og_ref[pl.ds(t0, L), hs])
        return carry

    lax.fori_loop(0, nch, combine, 0)

    for d in range(2):
        for hd in range(N_HEADS):
            sT_ref[d, hd] = st_ref[d, hd].T


def _seq_col_spec(T, j):
    return pl.BlockSpec((T, D_GROUP), lambda b, j=j: (b, j))


def _hgrn_call(z2, B, s0, lower, norm_g):
    T = z2.shape[0] // B
    col = functools.partial(_seq_col_spec, T)
    st_spec = pl.BlockSpec((None, 2, N_HEADS, D_HEAD, D_HEAD), lambda b: (b, 0, 0, 0, 0))
    return pl.pallas_call(
        _hgrn_kernel,
        grid=(B,),
        in_specs=[col(2), col(3), col(4), col(5), col(6),
                  pl.BlockSpec((2, D_GROUP), lambda b: (0, 0)),
                  pl.BlockSpec((1, D_GROUP), lambda b: (0, 0)),
                  st_spec],
        out_specs=[pl.BlockSpec((T, D_GROUP), lambda b: (b, 0)), st_spec],
        out_shape=[jax.ShapeDtypeStruct((B * T, D_GROUP), F32),
                   jax.ShapeDtypeStruct((B, 2, N_HEADS, D_HEAD, D_HEAD), F32)],
        scratch_shapes=[pltpu.VMEM((2, N_HEADS, D_HEAD, D_HEAD), F32),
                        pltpu.VMEM((T, D_GROUP), F32),
                        pltpu.VMEM((T, D_GROUP), F32)],
        compiler_params=_cparams(1),
        name="hgrn2",
    )(z2, z2, z2, z2, z2, lower, norm_g, s0)


def _mlstm_kernel(q_ref, k_ref, v_ref, og_ref, g_ref, ng_ref, c0_ref, n0_ref, m0_ref,
                  y_ref, cT_ref, nT_ref, mT_ref, cn_ref, of_ref, ob_ref):
    T = q_ref.shape[0]
    L = ML_L
    SUB = ML_SUB
    nch = T // L
    nsub = L // SUB
    tril, triu = _tri_masks(L)
    gcol = lax.broadcasted_iota(jnp.int32, (L, N_GATE), 1)
    heads = [slice(hd * D_HEAD, (hd + 1) * D_HEAD) for hd in range(N_HEADS)]
    ones = jnp.ones((L, D_HEAD), BF16)

    for d in range(2):
        for hd in range(N_HEADS):
            cn_ref[d, hd, :, 0:D_HEAD] = c0_ref[d, hd]
            cn_ref[d, hd, :, D_HEAD:] = jnp.broadcast_to(n0_ref[d, hd:hd + 1, :], (D_HEAD, D_HEAD)).T
    mT_ref[...] = m0_ref[...]

    def step(c, carry):
        jobs = ((0, pl.multiple_of(c * L, L), of_ref), (1, pl.multiple_of((nch - 1 - c) * L, L), ob_ref))
        loaded, q_state = [], []
        for d, t0, _ in jobs:
            g = g_ref[pl.ds(t0, L), 0:N_GATE]
            g = jnp.where(gcol % 8 >= N_HEADS, jax.nn.log_sigmoid(g), g)
            bc = _dot_split3(tril if d == 0 else triu, g, mask_on_left=True)
            qb = q_ref[pl.ds(t0, L), :].astype(BF16)
            kf = k_ref[pl.ds(t0, L), :] * (D_HEAD ** -0.5)
            vb = v_ref[pl.ds(t0, L), :].astype(BF16)
            v1 = [jnp.concatenate([vb[:, hs], ones], axis=1) for hs in heads]
            loaded.append((g, bc, qb, kf, v1))
            q_state.append([_dot(qb[:, hs], cn_ref[d, hd]) for hd, hs in enumerate(heads)])
        scaled = []
        for j, (d, _, _) in enumerate(jobs):
            g, bc, qb, kf, v1 = loaded[j]
            per_head = []
            for hd, hs in enumerate(heads):
                ic, fc = d * 8 + hd, d * 8 + N_HEADS + hd
                b_col = bc[:, fc:fc + 1]
                r_col = g[:, ic:ic + 1] - b_col
                b_tot = b_col[L - 1:L] if d == 0 else b_col[0:1]
                m_st = mT_ref[d, hd:hd + 1, 0:1]
                blk_max = [jnp.max(r_col[i * SUB:(i + 1) * SUB], axis=0, keepdims=True) for i in range(nsub)]
                r_rep = jnp.broadcast_to(r_col, (L, D_HEAD))
                subs = []
                for i in range(nsub):
                    rows = slice(i * SUB, (i + 1) * SUB)
                    cols = slice(0, (i + 1) * SUB) if d == 0 else slice(i * SUB, L)
                    seen = blk_max[:i + 1] if d == 0 else blk_max[i:]
                    mu = jnp.maximum(m_st, functools.reduce(jnp.maximum, seen))
                    kp = (kf[cols, hs] * jnp.exp(r_rep[cols] - mu)).astype(BF16)
                    floor = jnp.exp(-b_col[rows] - mu)
                    subs.append((rows, cols, mu, kp, floor))
                per_head.append((b_tot, m_st, subs))
            scaled.append(per_head)
        scores = [[[_dot_nt(loaded[j][2][rows, hs], kp) for (rows, _, _, kp, _) in scaled[j][hd][2]]
                   for hd, hs in enumerate(heads)] for j in range(2)]
        full = [nsub - 1, 0]
        update = [[_dot_tn(scaled[j][hd][2][full[j]][3], loaded[j][4][hd]) for hd in range(N_HEADS)]
                  for j in range(2)]
        for j, (d, t0, o_ref) in enumerate(jobs):
            mask = tril if d == 0 else triu
            for hd, hs in enumerate(heads):
                b_tot, m_st, subs = scaled[j][hd]
                outs = []
                for i, (rows, cols, mu, _, floor) in enumerate(subs):
                    w = jnp.where(mask[rows, cols], scores[j][hd][i], 0.0)
                    nd = _dot(w, loaded[j][4][hd][cols]) + jnp.exp(m_st - mu) * q_state[j][hd][rows]
                    outs.append(nd[:, 0:D_HEAD] / jnp.maximum(jnp.abs(nd[:, D_HEAD:]), floor))
                o_ref[pl.ds(t0, L), hs] = jnp.concatenate(outs, axis=0)
                mu_full = subs[full[j]][2]
                cn_ref[d, hd] = jnp.exp(m_st - mu_full) * cn_ref[d, hd] + update[j][hd]
                mT_ref[d, hd:hd + 1, :] = jnp.broadcast_to(b_tot + mu_full, (1, LANES))
        return carry

    lax.fori_loop(0, nch, step, 0)

    for d in range(2):
        for hd in range(N_HEADS):
            cT_ref[d, hd] = cn_ref[d, hd, :, 0:D_HEAD]
            nT_ref[d, hd:hd + 1, :] = cn_ref[d, hd, :, D_HEAD:].T[0:1, :]

    def combine(c, carry):
        t0 = pl.multiple_of(c * L, L)
        for hs in heads:
            o = of_ref[pl.ds(t0, L), hs] + ob_ref[pl.ds(t0, L), hs]
            mu = jnp.mean(o, -1, keepdims=True)
            var = jnp.mean(jnp.square(o - mu), -1, keepdims=True)
            o = (o - mu) * lax.rsqrt(var + EPS)
            y_ref[pl.ds(t0, L), hs] = jax.nn.sigmoid(og_ref[pl.ds(t0, L), hs]) * (o * ng_ref[:, hs])
        return carry

    lax.fori_loop(0, nch, combine, 0)


def _mlstm_call(z2, B, c0, n0, m0, norm_g):
    T = z2.shape[0] // B
    col = functools.partial(_seq_col_spec, T)
    c_spec = pl.BlockSpec((None, 2, N_HEADS, D_HEAD, D_HEAD), lambda b: (b, 0, 0, 0, 0))
    v_spec = pl.BlockSpec((None, 2, N_HEADS, D_HEAD), lambda b: (b, 0, 0, 0))
    return pl.pallas_call(
        _mlstm_kernel,
        grid=(B,),
        in_specs=[col(7), col(8), col(9), col(10),
                  pl.BlockSpec((T, LANES), lambda b: (b, GATE_BLK)),
                  pl.BlockSpec((1, D_GROUP), lambda b: (0, 0)),
                  c_spec, v_spec, v_spec],
        out_specs=[pl.BlockSpec((T, D_GROUP), lambda b: (b, 0)), c_spec, v_spec, v_spec],
        out_shape=[jax.ShapeDtypeStruct((B * T, D_GROUP), F32),
                   jax.ShapeDtypeStruct((B, 2, N_HEADS, D_HEAD, D_HEAD), F32),
                   jax.ShapeDtypeStruct((B, 2, N_HEADS, D_HEAD), F32),
                   jax.ShapeDtypeStruct((B, 2, N_HEADS, D_HEAD), F32)],
        scratch_shapes=[pltpu.VMEM((2, N_HEADS, D_HEAD, 2 * D_HEAD), F32),
                        pltpu.VMEM((T, D_GROUP), F32), pltpu.VMEM((T, D_GROUP), F32)],
        compiler_params=_cparams(1),
        name="mlstm",
    )(z2, z2, z2, z2, z2, norm_g, c0, n0, m0)


def _layer_norm(v, g, b):
    mu = jnp.mean(v, -1, keepdims=True)
    var = jnp.mean(jnp.square(v - mu), -1, keepdims=True)
    return (v - mu) * lax.rsqrt(var + EPS) * g + b


def _route(logits_t, b_router):
    mx = jnp.max(logits_t, axis=0, keepdims=True)
    ex = jnp.exp(logits_t - mx)
    scores = ex / jnp.sum(ex, axis=0, keepdims=True)
    sel = scores + b_router
    rows = [sel[e:e + 1] for e in range(N_EXPERTS)]
    first, second, grp = [], [], []
    for gi in range(N_GROUPS):
        r = rows[gi * EXPERTS_PER_GROUP:(gi + 1) * EXPERTS_PER_GROUP]
        m1 = functools.reduce(jnp.maximum, r)
        taken = jnp.zeros_like(m1, dtype=jnp.bool_)
        f = []
        for x in r:
            hit = jnp.logical_and(x == m1, jnp.logical_not(taken))
            f.append(hit)
            taken = jnp.logical_or(taken, hit)
        rest = [jnp.where(fi, -jnp.inf, x) for fi, x in zip(f, r)]
        m2 = functools.reduce(jnp.maximum, rest)
        taken = jnp.zeros_like(m1, dtype=jnp.bool_)
        s = []
        for x in rest:
            hit = jnp.logical_and(x == m2, jnp.logical_not(taken))
            s.append(hit)
            taken = jnp.logical_or(taken, hit)
        first.append(f)
        second.append(s)
        grp.append(m1 + m2)
    gmax = functools.reduce(jnp.maximum, grp)
    taken = jnp.zeros_like(gmax, dtype=jnp.bool_)
    chosen = []
    group = jnp.zeros_like(gmax, dtype=jnp.int32)
    for gi in range(N_GROUPS):
        best = jnp.logical_and(grp[gi] == gmax, jnp.logical_not(taken))
        taken = jnp.logical_or(taken, best)
        group = jnp.where(best, gi, group)
        for j in range(EXPERTS_PER_GROUP):
            chosen.append(jnp.logical_and(best, jnp.logical_or(first[gi][j], second[gi][j])))
    picked = [jnp.where(ch, scores[e:e + 1], 0.0) for e, ch in enumerate(chosen)]
    denom = functools.reduce(lambda a, b: a + b, picked)
    return jnp.concatenate([p / denom for p in picked], axis=0), group


D_DISPATCH = D_MODEL + LANES


def _outproj_kernel(yr_ref, yh_ref, ym_ref, x_ref, mod_ref, w_ref, lg_ref, lb_ref,
                    wr_ref, br_ref, x1_ref, hx_ref, grp_ref):
    tm = x_ref.shape[0]
    m = mod_ref[...]
    y = (jnp.dot(yr_ref[...].astype(BF16), w_ref[0:D_GROUP], preferred_element_type=F32)
         + jnp.dot(yh_ref[...].astype(BF16), w_ref[D_GROUP:2 * D_GROUP], preferred_element_type=F32)
         + jnp.dot(ym_ref[...].astype(BF16), w_ref[2 * D_GROUP:], preferred_element_type=F32))
    x1 = _layer_norm(ALPHA * x_ref[...] + m[GATE1:GATE1 + 1] * y, lg_ref[...], lb_ref[...])
    x1_ref[...] = x1
    hm = x1 * (1.0 + m[SCALE2:SCALE2 + 1]) + m[SHIFT2:SHIFT2 + 1]
    hx_ref[:, 0:D_MODEL] = hm
    logits_t = lax.dot_general(wr_ref[...], hm, (((1,), (1,)), ((), ())), precision=HIGHEST,
                               preferred_element_type=F32)
    cmb_t, group = _route(logits_t, br_ref[...])
    cmb_rows = jnp.concatenate([cmb_t, jnp.zeros((LANES - N_EXPERTS, tm), F32)], axis=0)
    hx_ref[:, D_MODEL:] = cmb_rows.T
    grp_ref[...] = group


def _outproj_call(yr, yh, ym, x, T, mod, w_out, l, ln_g, ln_b, w_router_t, b_router):
    n = x.shape[0]
    tm = OUTPROJ_TILE if (mod.shape[0] == 1 or T % OUTPROJ_TILE == 0) else ROW_TILE
    row = lambda w: pl.BlockSpec((tm, w), lambda i: (i, 0))
    full = lambda a: pl.BlockSpec(a.shape, lambda i: (0,) * a.ndim)
    return pl.pallas_call(
        _outproj_kernel,
        grid=(n // tm,),
        in_specs=[row(D_GROUP), row(D_GROUP), row(D_GROUP), row(D_MODEL), _mod_spec(mod, T, tm),
                  pl.BlockSpec((None, D_MIX, D_MODEL), lambda i: (l, 0, 0)),
                  full(ln_g), full(ln_b), full(w_router_t), full(b_router)],
        out_specs=[row(D_MODEL), row(D_DISPATCH), pl.BlockSpec((1, tm), lambda i: (0, i))],
        out_shape=[jax.ShapeDtypeStruct((n, D_MODEL), F32),
                   jax.ShapeDtypeStruct((n, D_DISPATCH), F32),
                   jax.ShapeDtypeStruct((1, n), jnp.int32)],
        compiler_params=_cparams(1),
        name="outproj_ln_router",
    )(yr, yh, ym, x, mod, w_out, ln_g, ln_b, w_router_t, b_router)


MOE_TILE = 256


def _row_gather(idx_ref, first, n_rows, src_hbm, dst_ref, sem):
    def body(r, carry):
        pltpu.make_async_copy(src_hbm.at[pl.ds(idx_ref[first + r], 1)], dst_ref.at[pl.ds(r, 1)], sem).start()
        return carry
    lax.fori_loop(0, n_rows, body, 0, unroll=8)


def _row_gather_wait(n_rows, src_hbm, dst_ref, sem):
    def body(r, carry):
        pltpu.make_async_copy(src_hbm.at[pl.ds(0, 1)], dst_ref.at[pl.ds(0, 1)], sem).wait()
        return carry
    lax.fori_loop(0, n_rows, body, 0, unroll=8)


def _gathered_tile(idx_ref, src_hbm, buf_ref, sem_ref, tm):
    i = pl.program_id(0)
    slot = lax.rem(i, 2)

    @pl.when(i == 0)
    def _():
        _row_gather(idx_ref, 0, tm, src_hbm, buf_ref.at[0], sem_ref.at[0])

    @pl.when(i + 1 < pl.num_programs(0))
    def _():
        _row_gather(idx_ref, (i + 1) * tm, tm, src_hbm, buf_ref.at[1 - slot], sem_ref.at[1 - slot])

    _row_gather_wait(tm, src_hbm, buf_ref.at[slot], sem_ref.at[slot])
    return buf_ref[slot]


def _moe_kernel(tile_group_ref, src_ref, hx_hbm, wg_ref, wu_ref, wd_ref, o_ref, buf_ref, sem_ref):
    tm = o_ref.shape[0]
    i = pl.program_id(0)
    n_tiles = pl.num_programs(0)
    slot = lax.rem(i, 2)

    @pl.when(i == 0)
    def _():
        _row_gather(src_ref, 0, tm, hx_hbm, buf_ref.at[0], sem_ref.at[0])

    _row_gather_wait(tm, hx_hbm, buf_ref.at[slot], sem_ref.at[slot])
    rows = buf_ref[slot]
    hm = rows[:, 0:D_MODEL].astype(BF16)
    cmb = rows[:, D_MODEL:]
    lane = lax.broadcasted_iota(jnp.int32, cmb.shape, 1)
    first_expert = tile_group_ref[i] * EXPERTS_PER_GROUP
    nxt = lax.rem(i + 1, n_tiles) * tm
    part_rows = tm // EXPERTS_PER_GROUP
    acc = None
    for k in range(EXPERTS_PER_GROUP):
        for r in range(k * part_rows, (k + 1) * part_rows):
            pltpu.make_async_copy(hx_hbm.at[pl.ds(src_ref[nxt + r], 1)], buf_ref.at[1 - slot, pl.ds(r, 1)],
                                  sem_ref.at[1 - slot]).start()
        ce = jnp.sum(jnp.where(lane == first_expert + k, cmb, 0.0), axis=1, keepdims=True)
        hg = jnp.dot(hm, wg_ref[k], preferred_element_type=F32)
        hu = jnp.dot(hm, wu_ref[k], preferred_element_type=F32)
        part = jnp.dot((_silu(hg) * hu * ce).astype(BF16), wd_ref[k], preferred_element_type=F32)
        acc = part if acc is None else acc + part
    o_ref[...] = acc

    @pl.when(i == n_tiles - 1)
    def _():
        _row_gather_wait(tm, hx_hbm, buf_ref.at[1 - slot], sem_ref.at[1 - slot])


def _moe_call(hx, tile_group, src, w_gate, w_up, w_down, l):
    n_pad = src.shape[0]
    tm = MOE_TILE
    grp_w = lambda shape: pl.BlockSpec((None,) + shape, lambda i, tg, sr: (l, tg[i], 0, 0))
    return pl.pallas_call(
        _moe_kernel,
        grid_spec=pltpu.PrefetchScalarGridSpec(
            num_scalar_prefetch=2,
            grid=(n_pad // tm,),
            in_specs=[pl.BlockSpec(memory_space=pl.ANY),
                      grp_w((EXPERTS_PER_GROUP, D_MODEL, D_FF)), grp_w((EXPERTS_PER_GROUP, D_MODEL, D_FF)),
                      grp_w((EXPERTS_PER_GROUP, D_FF, D_MODEL))],
            out_specs=pl.BlockSpec((tm, D_MODEL), lambda i, tg, sr: (i, 0)),
            scratch_shapes=[pltpu.VMEM((2, tm, D_DISPATCH), F32), pltpu.SemaphoreType.DMA((2,))]),
        out_shape=jax.ShapeDtypeStruct((n_pad, D_MODEL), F32),
        compiler_params=_cparams(1),
        name="moe_sorted",
    )(tile_group, src, hx, w_gate, w_up, w_down)


def _ln2_kernel(pos_ref, f_hbm, x1_ref, mod_ref, lg_ref, lb_ref, o_ref, buf_ref, sem_ref):
    tm = o_ref.shape[0]
    f = _gathered_tile(pos_ref, f_hbm, buf_ref, sem_ref, tm)
    v = ALPHA * x1_ref[...] + mod_ref[GATE2:GATE2 + 1, :] * f
    o_ref[...] = _layer_norm(v, lg_ref[...], lb_ref[...])


def _ln2_call(f_sorted, pos, x1, T, mod, ln_g, ln_b):
    n = x1.shape[0]
    tm = ROW_TILE
    per_seq = mod.shape[0] > 1
    return pl.pallas_call(
        _ln2_kernel,
        grid_spec=pltpu.PrefetchScalarGridSpec(
            num_scalar_prefetch=1,
            grid=(n // tm,),
            in_specs=[pl.BlockSpec(memory_space=pl.ANY),
                      pl.BlockSpec((tm, D_MODEL), lambda i, ps: (i, 0)),
                      pl.BlockSpec((None, 6, D_MODEL), lambda i, ps: ((i * tm) // T if per_seq else 0, 0, 0)),
                      pl.BlockSpec((1, D_MODEL), lambda i, ps: (0, 0)),
                      pl.BlockSpec((1, D_MODEL), lambda i, ps: (0, 0))],
            out_specs=pl.BlockSpec((tm, D_MODEL), lambda i, ps: (i, 0)),
            scratch_shapes=[pltpu.VMEM((2, tm, D_MODEL), F32), pltpu.SemaphoreType.DMA((2,))]),
        out_shape=jax.ShapeDtypeStruct((n, D_MODEL), F32),
        compiler_params=_cparams(1),
        name="moe_combine_ln",
    )(pos, f_sorted, x1, mod, ln_g, ln_b)


def _dispatch_plan(group, tm):
    n = group.shape[0]
    n_pad = n + N_GROUPS * tm
    onehot = (group[:, None] == jnp.arange(N_GROUPS, dtype=jnp.int32)[None, :]).astype(jnp.int32)
    csum = jnp.cumsum(onehot, axis=0)
    rank = jnp.take_along_axis(csum, group[:, None], axis=1)[:, 0] - 1
    padded = ((csum[-1] + tm - 1) // tm) * tm
    ends = jnp.cumsum(padded)
    pos = (ends - padded)[group] + rank
    src = jnp.zeros((n_pad,), jnp.int32).at[pos].set(jnp.arange(n, dtype=jnp.int32))
    tile_start = jnp.arange(n_pad // tm, dtype=jnp.int32) * tm
    tile_group = jnp.sum((tile_start[:, None] >= ends[None, :]).astype(jnp.int32), axis=1)
    tile_group = jnp.minimum(tile_group, N_GROUPS - 1)
    return pos, src, tile_group


def _grid_pos_embed(n_tokens):
    rows = n_tokens // GRID_W
    r = jnp.repeat(jnp.arange(rows, dtype=F32), GRID_W)
    c = jnp.tile(jnp.arange(GRID_W, dtype=F32), rows)
    q = D_MODEL // 4
    freq = jnp.exp(-jnp.log(10000.0) * jnp.arange(q, dtype=F32) / q)
    ar = r[:, None] * freq
    ac = c[:, None] * freq
    return jnp.concatenate([jnp.sin(ar), jnp.cos(ar), jnp.sin(ac), jnp.cos(ac)], axis=-1)


def _lru_gate_params(wa, ba, wx, bx, lam):
    ncb = D_GROUP // LANES

    def dense(w):
        z = jnp.zeros((ncb, LANES, LANES), F32)
        z = z.at[:, :LRU_BLOCK, :LRU_BLOCK].set(w[0::2])
        return z.at[:, LRU_BLOCK:, LRU_BLOCK:].set(w[1::2])

    wg = jnp.concatenate([dense(wa[0]), dense(wx[0]), dense(wa[1]), dense(wx[1])], axis=-1).astype(BF16)
    per_blk = lambda v: v.reshape(ncb, 1, LANES)
    bg = jnp.concatenate([per_blk(ba[0]), per_blk(bx[0]), per_blk(ba[1]), per_blk(bx[1])], axis=-1)
    lm = jnp.concatenate([per_blk(lam[0]), per_blk(lam[1])], axis=-1)
    return wg, bg, lm


def _mixer_states(B, l, states):
    if states is None:
        return (jnp.zeros((2, B, D_GROUP), F32),
                jnp.zeros((B, 2, N_HEADS, D_HEAD, D_HEAD), F32),
                jnp.zeros((B, 2, N_HEADS, D_HEAD, D_HEAD), F32),
                jnp.zeros((B, 2, N_HEADS, D_HEAD), F32),
                jnp.zeros((B, 2, N_HEADS, D_HEAD), F32))
    h, s, c, n, m = states
    return (jnp.swapaxes(h[:, l], 0, 1), s[:, l], c[:, l], n[:, l],
            jnp.broadcast_to(m[:, l][..., None], (B, 2, N_HEADS, D_HEAD)))


def kernel(x_prompt, x_sample, state_lru_h, state_hgrn_S, state_mlstm_C, state_mlstm_n, state_mlstm_m,
           c, c_ctx, w_ada, b_ada, w_in, b_in, conv_w, conv_b, lru_wa, lru_ba, lru_wx, lru_bx, lru_lam,
           hg_lb, hg_norm_g, ml_norm_g, w_out, ln1_g, ln1_b, ln2_g, ln2_b,
           w_router, b_router, w_gate, w_up, w_down):
    Bp, Tp, _ = x_prompt.shape
    Bs, Ts, _ = x_sample.shape
    assert Bs % SUBLANES == 0 and Bp % SUBLANES == 0 and Bs + 1 <= 16
    assert Tp % ROW_TILE == 0 and Ts % ROW_TILE == 0

    lb_soft = jax.nn.softmax(hg_lb.astype(F32), axis=0)
    hg_lower = jnp.cumsum(lb_soft, axis=0) - lb_soft[0:1]

    c16 = jnp.concatenate([c, c_ctx[None], jnp.zeros((16 - Bs - 1, D_MODEL), F32)], axis=0)
    mod = _ada_call(c16, w_ada, b_ada)

    w_in_p = jnp.pad(w_in, ((0, 0), (0, 0), (0, D_IN_PAD - D_IN))).astype(BF16)
    b_in_p = jnp.pad(b_in, ((0, 0), (0, D_IN_PAD - D_IN))).reshape(DEPTH, 1, D_IN_PAD)
    w_out_b = w_out.astype(BF16)
    w_gate_b, w_up_b, w_down_b = w_gate.astype(BF16), w_up.astype(BF16), w_down.astype(BF16)
    w_router_t = w_router.T
    b_router_c = b_router.reshape(N_EXPERTS, 1)

    xs = x_sample + _grid_pos_embed(Ts).astype(x_sample.dtype)
    streams = [
        dict(x=x_prompt.reshape(Bp * Tp, D_MODEL), T=Tp, B=Bp, states=None),
        dict(x=xs.reshape(Bs * Ts, D_MODEL), T=Ts, B=Bs,
             states=(state_lru_h, state_hgrn_S, state_mlstm_C, state_mlstm_n, state_mlstm_m)),
    ]
    finals = []
    for l in range(DEPTH):
        wg, bg, lm = _lru_gate_params(lru_wa[l], lru_ba[l], lru_wx[l], lru_bx[l], lru_lam[l])
        row = lambda v: v[l].reshape(1, -1)
        for si, st in enumerate(streams):
            T, B = st["T"], st["B"]
            mod_s = (mod[l, :Bs] if si == 1 else mod[l, Bs:Bs + 1]).reshape(-1, 6, D_MODEL)
            h0, s0, c0, n0, m0 = _mixer_states(B, l, st["states"])
            z = _inproj_call(st["x"], T, mod_s, w_in_p, b_in_p, l)
            y_r, h_T = _rglru_call(z.reshape(B, T, D_IN_PAD), h0, conv_w[l], row(conv_b), wg, bg, lm)
            y_h, s_T = _hgrn_call(z, B, s0, hg_lower[l], row(hg_norm_g))
            y_m, c_T, n_T, m_T = _mlstm_call(z, B, c0, n0, m0, row(ml_norm_g))
            x1, hx, group = _outproj_call(y_r.reshape(B * T, D_GROUP), y_h, y_m, st["x"], T, mod_s, w_out_b, l,
                                          row(ln1_g), row(ln1_b), w_router_t, b_router_c)
            pos, src, tile_group = _dispatch_plan(group[0], MOE_TILE)
            f_sorted = _moe_call(hx, tile_group, src, w_gate_b, w_up_b, w_down_b, l)
            st["x"] = _ln2_call(f_sorted, pos, x1, T, mod_s, row(ln2_g), row(ln2_b))
            if si == 0:
                finals.append((jnp.swapaxes(h_T, 0, 1), s_T, c_T, n_T, m_T[..., 0]))
    outs = [st["x"].reshape(st["B"], st["T"], D_MODEL) for st in streams]
    stack = lambda i: jnp.stack([f[i] for f in finals], axis=1)
    return (outs[0], outs[1], stack(0), stack(1), stack(2), stack(3), stack(4))
```

```python
import functools

import jax
import jax.numpy as jnp
from jax import lax
from jax.experimental import pallas as pl
from jax.experimental.pallas import tpu as pltpu

F32 = jnp.float32
BF16 = jnp.bfloat16
HIGHEST = lax.Precision.HIGHEST

D_MODEL = 1024
DEPTH = 2
GRID_W = 64
D_GROUP = 512
D_MIX = 3 * D_GROUP
LRU_BLOCKS = 8
LRU_BLOCK = D_GROUP // LRU_BLOCKS
LRU_C = 8.0
N_HEADS = 4
D_HEAD = D_GROUP // N_HEADS
N_EXPERTS = 16
N_GROUPS = 4
EXPERTS_PER_GROUP = N_EXPERTS // N_GROUPS
D_FF = 512
ALPHA = (2.0 * DEPTH) ** 0.25
EPS = 1e-5
NEG = -1e30
TINY = 1e-30
N_GATE = 4 * N_HEADS
D_IN = 11 * D_GROUP + N_GATE
D_IN_PAD = 45 * 128
GATE_BLK = (11 * D_GROUP) // 128

SUBLANES = 8
LANES = 128
VMEM_LIMIT = 56 * 1024 * 1024

LRU_TC = 32
LRU_PAD = SUBLANES
HG_L = 64
HG_SUB = 16
ML_L = 128
ML_SUB = 32


def _cparams(n_axes):
    return pltpu.CompilerParams(dimension_semantics=("arbitrary",) * n_axes,
                                vmem_limit_bytes=VMEM_LIMIT)


def _dot(a, b):
    return jnp.dot(a.astype(BF16), b.astype(BF16), preferred_element_type=F32)


def _dot_nt(a, b):
    return lax.dot_general(a.astype(BF16), b.astype(BF16), (((1,), (1,)), ((), ())),
                           preferred_element_type=F32)


def _dot_tn(a, b):
    return lax.dot_general(a.astype(BF16), b.astype(BF16), (((0,), (0,)), ((), ())),
                           preferred_element_type=F32)


def _dot_f32(a, b):
    return jnp.dot(a, b, precision=HIGHEST, preferred_element_type=F32)


def _dot_split3(mask, x, mask_on_left):
    m = jnp.where(mask, 1.0, 0.0).astype(BF16)
    hi = x.astype(BF16)
    r1 = x - hi.astype(F32)
    mid = r1.astype(BF16)
    lo = (r1 - mid.astype(F32)).astype(BF16)
    mm = (lambda p: jnp.dot(m, p, preferred_element_type=F32)) if mask_on_left else (
        lambda p: jnp.dot(p, m, preferred_element_type=F32))
    return mm(hi) + mm(mid) + mm(lo)


def _softplus(x):
    return jnp.maximum(x, 0.0) + jnp.log1p(jnp.exp(-jnp.abs(x)))


def _silu(x):
    return x * jax.nn.sigmoid(x)


def _tri_masks(n):
    r = lax.broadcasted_iota(jnp.int32, (n, n), 0)
    c = lax.broadcasted_iota(jnp.int32, (n, n), 1)
    return r >= c, r <= c


def _ada_kernel(c_ref, w_ref, b_ref, o_ref):
    o_ref[0] = _dot_f32(_silu(c_ref[...]), w_ref[0]) + b_ref[0]


def _ada_call(c16, w_ada, b_ada):
    tn = 1536
    return pl.pallas_call(
        _ada_kernel,
        grid=(DEPTH, 6 * D_MODEL // tn),
        in_specs=[pl.BlockSpec((16, D_MODEL), lambda l, j: (0, 0)),
                  pl.BlockSpec((1, D_MODEL, tn), lambda l, j: (l, 0, j)),
                  pl.BlockSpec((1, 1, tn), lambda l, j: (l, 0, j))],
        out_specs=pl.BlockSpec((1, 16, tn), lambda l, j: (l, 0, j)),
        out_shape=jax.ShapeDtypeStruct((DEPTH, 16, 6 * D_MODEL), F32),
        compiler_params=_cparams(2),
        name="adaln",
    )(c16, w_ada, b_ada.reshape(DEPTH, 1, 6 * D_MODEL))


INPROJ_CW = 640
ROW_TILE = 256
OUTPROJ_TILE = 512

SHIFT1, SCALE1, GATE1, SHIFT2, SCALE2, GATE2 = range(6)


def _mod_spec(mod, T, tm, n_grid_axes=1):
    per_seq = mod.shape[0] > 1
    if n_grid_axes == 1:
        return pl.BlockSpec((None, 6, D_MODEL), lambda i: ((i * tm) // T if per_seq else 0, 0, 0))
    return pl.BlockSpec((None, 6, D_MODEL), lambda i, e: ((i * tm) // T if per_seq else 0, 0, 0))


def _inproj_kernel(x_ref, mod_ref, w_ref, b_ref, z_ref):
    m = mod_ref[...]
    hb = (x_ref[...] * (1.0 + m[SCALE1:SCALE1 + 1]) + m[SHIFT1:SHIFT1 + 1]).astype(BF16)
    for j in range(D_IN_PAD // INPROJ_CW):
        cs = slice(j * INPROJ_CW, (j + 1) * INPROJ_CW)
        z_ref[:, cs] = jnp.dot(hb, w_ref[:, cs], preferred_element_type=F32) + b_ref[:, cs]


def _inproj_call(x, T, mod, w, b, l):
    n = x.shape[0]
    tm = ROW_TILE
    return pl.pallas_call(
        _inproj_kernel,
        grid=(n // tm,),
        in_specs=[pl.BlockSpec((tm, D_MODEL), lambda i: (i, 0)),
                  _mod_spec(mod, T, tm),
                  pl.BlockSpec((None, D_MODEL, D_IN_PAD), lambda i: (l, 0, 0)),
                  pl.BlockSpec((None, 1, D_IN_PAD), lambda i: (l, 0, 0))],
        out_specs=pl.BlockSpec((tm, D_IN_PAD), lambda i: (i, 0)),
        out_shape=jax.ShapeDtypeStruct((n, D_IN_PAD), F32),
        compiler_params=_cparams(1),
        name="inproj",
    )(x, mod, w, b)


def _rglru_kernel(xr_ref, gr_ref, cw_ref, cb_ref, wg_ref, bg_ref, lam_ref, h0_ref,
                  y_ref, hT_ref, xs_ref, os_ref):
    T = xr_ref.shape[1]
    P = T + LRU_PAD
    TC = LRU_TC
    nch = T // TC
    CP = 256
    zeros = jnp.zeros((LRU_PAD, LANES), F32)
    for s in range(SUBLANES + 1):
        xs_ref[s * P:s * P + LRU_PAD] = zeros
    for s in range(SUBLANES):
        def copy_in(c, carry, s=s):
            r0 = pl.multiple_of(c * CP, CP)
            xs_ref[pl.ds(s * P + LRU_PAD + r0, CP)] = xr_ref[s, pl.ds(r0, CP)]
            return carry
        lax.fori_loop(0, T // CP, copy_in, 0)

    cw = cw_ref[...]
    cb = cb_ref[...]
    sp = _softplus(-lam_ref[0])

    def gates(t0, d):
        win = [xs_ref[pl.ds(t0 + k + LRU_PAD - 2, SUBLANES, stride=P)] for k in range(TC + 3)]
        xc = jnp.stack([cw[0:1] * win[i] + cw[1:2] * win[i + 1] + cw[2:3] * win[i + 2] + cw[3:4] * win[i + 3] + cb
                        for i in range(TC)])
        x2 = xc.reshape(TC * SUBLANES, LANES)
        g = (jnp.dot(x2.astype(BF16), wg_ref[0, :, d * 256:(d + 1) * 256], preferred_element_type=F32)
             + bg_ref[0, :, d * 256:(d + 1) * 256])
        r = jax.nn.sigmoid(g[:, :LANES])
        i = jax.nn.sigmoid(g[:, LANES:])
        log_a = -LRU_C * r * sp[:, d * LANES:(d + 1) * LANES]
        a = jnp.exp(log_a)
        u = jnp.sqrt(jnp.maximum(1.0 - jnp.exp(2.0 * log_a), 0.0)) * (i * x2)
        return a.reshape(TC, SUBLANES, LANES), u.reshape(TC, SUBLANES, LANES)

    def fwd(c, h):
        t0 = pl.multiple_of(c * TC, TC)
        a, u = gates(t0, 0)
        for i in range(TC):
            h = a[i] * h + u[i]
            os_ref[pl.ds(t0 + i, SUBLANES, stride=P)] = h
        return h

    hT_ref[0] = lax.fori_loop(0, nch, fwd, h0_ref[0])

    def bwd(c, h):
        t0 = pl.multiple_of((nch - 1 - c) * TC, TC)
        a, u = gates(t0, 1)
        for i in range(TC - 1, -1, -1):
            h = a[i] * h + u[i]
            os_ref[pl.ds(t0 + i, SUBLANES, stride=P)] = os_ref[pl.ds(t0 + i, SUBLANES, stride=P)] + h
        return h

    hT_ref[1] = lax.fori_loop(0, nch, bwd, h0_ref[1])

    for s in range(SUBLANES):
        def copy_out(c, carry, s=s):
            r0 = pl.multiple_of(c * CP, CP)
            y_ref[s, pl.ds(r0, CP)] = jax.nn.gelu(gr_ref[s, pl.ds(r0, CP)]) * os_ref[pl.ds(s * P + r0, CP)]
            return carry
        lax.fori_loop(0, T // CP, copy_out, 0)


def _rglru_call(z3, h0, cw, cb, wg, bg, lam):
    B, T, _ = z3.shape
    nb = B // SUBLANES
    ncb = D_GROUP // LANES
    blk = (SUBLANES, T, LANES)
    pitch = T + LRU_PAD
    return pl.pallas_call(
        _rglru_kernel,
        grid=(nb, ncb),
        in_specs=[pl.BlockSpec(blk, lambda b, c: (b, 0, c)),
                  pl.BlockSpec(blk, lambda b, c: (b, 0, ncb + c)),
                  pl.BlockSpec((4, LANES), lambda b, c: (0, c)),
                  pl.BlockSpec((1, LANES), lambda b, c: (0, c)),
                  pl.BlockSpec((1, LANES, 4 * LANES), lambda b, c: (c, 0, 0)),
                  pl.BlockSpec((1, 1, 4 * LANES), lambda b, c: (c, 0, 0)),
                  pl.BlockSpec((1, 1, 2 * LANES), lambda b, c: (c, 0, 0)),
                  pl.BlockSpec((2, SUBLANES, LANES), lambda b, c: (0, b, c))],
        out_specs=[pl.BlockSpec(blk, lambda b, c: (b, 0, c)),
                   pl.BlockSpec((2, SUBLANES, LANES), lambda b, c: (0, b, c))],
        out_shape=[jax.ShapeDtypeStruct((B, T, D_GROUP), F32),
                   jax.ShapeDtypeStruct((2, B, D_GROUP), F32)],
        scratch_shapes=[pltpu.VMEM((SUBLANES * pitch + LRU_PAD, LANES), F32),
                        pltpu.VMEM((SUBLANES * pitch, LANES), F32)],
        compiler_params=_cparams(2),
        name="rglru",
    )(z3, z3, cw, cb, wg, bg, lam, h0)


def _hgrn_kernel(q_ref, ff_ref, fb_ref, v_ref, og_ref, lb_ref, ng_ref, s0_ref,
                 y_ref, sT_ref, st_ref, of_ref, ob_ref):
    T = q_ref.shape[0]
    L = HG_L
    SUB = HG_SUB
    nch = T // L
    nsub = L // SUB
    tril, triu = _tri_masks(L)
    heads = [slice(hd * D_HEAD, (hd + 1) * D_HEAD) for hd in range(N_HEADS)]

    for d in range(2):
        for hd in range(N_HEADS):
            st_ref[d, hd] = s0_ref[d, hd].T

    def step(c, carry):
        jobs = ((0, pl.multiple_of(c * L, L), of_ref), (1, pl.multiple_of((nch - 1 - c) * L, L), ob_ref))
        gated = []
        for d, t0, _ in jobs:
            q = _silu(q_ref[pl.ds(t0, L), :])
            v = v_ref[pl.ds(t0, L), :].astype(BF16)
            lb = lb_ref[d:d + 1, :]
            sig = jax.nn.sigmoid((ff_ref if d == 0 else fb_ref)[pl.ds(t0, L), :])
            log_f = jnp.log(jnp.maximum(lb + (1.0 - lb) * sig, TINY))
            kk = (1.0 - lb) * (1.0 - sig)
            mask = tril if d == 0 else triu
            b = _dot_split3(mask, log_f, mask_on_left=True)
            gated.append((q, v, kk, b, mask))
        factored = []
        for (d, _, _), (q, v, kk, b, mask) in zip(jobs, gated):
            b_tot = b[L - 1:L] if d == 0 else b[0:1]
            subs = []
            for i in range(nsub):
                rows = slice(i * SUB, (i + 1) * SUB)
                if d == 0:
                    cols = slice(0, (i + 1) * SUB)
                    edge = b[i * SUB - 1:i * SUB] if i > 0 else 0.0
                else:
                    cols = slice(i * SUB, L)
                    edge = b[(i + 1) * SUB:(i + 1) * SUB + 1] if i < nsub - 1 else 0.0
                subs.append((rows, cols, (q[rows] * jnp.exp(b[rows] - edge)).astype(BF16),
                             (kk[cols] * jnp.exp(edge - b[cols])).astype(BF16)))
            factored.append(((q * jnp.exp(b)).astype(BF16), (kk * jnp.exp(b_tot - b)).astype(BF16),
                             jnp.exp(b_tot), subs))
        scores = [[[_dot_nt(qs[:, hs], ks[:, hs]) for (_, _, qs, ks) in subs] for hs in heads]
                  for (_, _, _, subs) in factored]
        states = [[st_ref[d, hd] for hd in range(N_HEADS)] for d, _, _ in jobs]
        inter = [[_dot_nt(qdec[:, hs], states[j][hd]) for hd, hs in enumerate(heads)]
                 for j, (qdec, _, _, _) in enumerate(factored)]
        update = [[_dot_tn(gated[j][1][:, hs], kdec[:, hs]) for hs in heads]
                  for j, (_, kdec, _, _) in enumerate(factored)]
        for j, (d, t0, o_ref) in enumerate(jobs):
            v, mask = gated[j][1], gated[j][4]
            dec, subs = factored[j][2], factored[j][3]
            for hd, hs in enumerate(heads):
                intra = [_dot(jnp.where(mask[rows, cols], scores[j][hd][i], 0.0), v[cols, hs])
                         for i, (rows, cols, _, _) in enumerate(subs)]
                o_ref[pl.ds(t0, L), hs] = jnp.concatenate(intra, axis=0) + inter[j][hd]
                st_ref[d, hd] = states[j][hd] * dec[:, hs] + update[j][hd]
        return carry

    lax.fori_loop(0, nch, step, 0)

    def combine(c, carry):
        t0 = pl.multiple_of(c * L, L)
        for hs in heads:
            o = of_ref[pl.ds(t0, L), hs] + ob_ref[pl.ds(t0, L), hs]
            o = o * lax.rsqrt(jnp.mean(jnp.square(o), -1, keepdims=True) + EPS)
            y_ref[pl.ds(t0, L), hs] = o * ng_ref[:, hs] * _silu(og_ref[pl.ds(t0, L), hs])
        return carry

    lax.fori_loop(0, nch, combine, 0)

    for d in range(2):
        for hd in range(N_HEADS):
            sT_ref[d, hd] = st_ref[d, hd].T


def _seq_col_spec(T, j):
    return pl.BlockSpec((T, D_GROUP), lambda b, j=j: (b, j))


def _hgrn_call(z2, B, s0, lower, norm_g):
    T = z2.shape[0] // B
    col = functools.partial(_seq_col_spec, T)
    st_spec = pl.BlockSpec((None, 2, N_HEADS, D_HEAD, D_HEAD), lambda b: (b, 0, 0, 0, 0))
    return pl.pallas_call(
        _hgrn_kernel,
        grid=(B,),
        in_specs=[col(2), col(3), col(4), col(5), col(6),
                  pl.BlockSpec((2, D_GROUP), lambda b: (0, 0)),
                  pl.BlockSpec((1, D_GROUP), lambda b: (0, 0)),
                  st_spec],
        out_specs=[pl.BlockSpec((T, D_GROUP), lambda b: (b, 0)), st_spec],
        out_shape=[jax.ShapeDtypeStruct((B * T, D_GROUP), F32),
                   jax.ShapeDtypeStruct((B, 2, N_HEADS, D_HEAD, D_HEAD), F32)],
        scratch_shapes=[pltpu.VMEM((2, N_HEADS, D_HEAD, D_HEAD), F32),
                        pltpu.VMEM((T, D_GROUP), F32),
                        pltpu.VMEM((T, D_GROUP), F32)],
        compiler_params=_cparams(1),
        name="hgrn2",
    )(z2, z2, z2, z2, z2, lower, norm_g, s0)


def _mlstm_kernel(q_ref, k_ref, v_ref, og_ref, g_ref, ng_ref, c0_ref, n0_ref, m0_ref,
                  y_ref, cT_ref, nT_ref, mT_ref, cn_ref, of_ref, ob_ref):
    T = q_ref.shape[0]
    L = ML_L
    SUB = ML_SUB
    nch = T // L
    nsub = L // SUB
    tril, triu = _tri_masks(L)
    gcol = lax.broadcasted_iota(jnp.int32, (L, N_GATE), 1)
    heads = [slice(hd * D_HEAD, (hd + 1) * D_HEAD) for hd in range(N_HEADS)]
    ones = jnp.ones((L, D_HEAD), BF16)

    for d in range(2):
        for hd in range(N_HEADS):
            cn_ref[d, hd, :, 0:D_HEAD] = c0_ref[d, hd]
            cn_ref[d, hd, :, D_HEAD:] = jnp.broadcast_to(n0_ref[d, hd:hd + 1, :], (D_HEAD, D_HEAD)).T
    mT_ref[...] = m0_ref[...]

    def step(c, carry):
        jobs = ((0, pl.multiple_of(c * L, L), of_ref), (1, pl.multiple_of((nch - 1 - c) * L, L), ob_ref))
        loaded, q_state = [], []
        for d, t0, _ in jobs:
            g = g_ref[pl.ds(t0, L), 0:N_GATE]
            g = jnp.where(gcol % 8 >= N_HEADS, jax.nn.log_sigmoid(g), g)
            bc = _dot_split3(tril if d == 0 else triu, g, mask_on_left=True)
            qb = q_ref[pl.ds(t0, L), :].astype(BF16)
            kf = k_ref[pl.ds(t0, L), :] * (D_HEAD ** -0.5)
            vb = v_ref[pl.ds(t0, L), :].astype(BF16)
            v1 = [jnp.concatenate([vb[:, hs], ones], axis=1) for hs in heads]
            loaded.append((g, bc, qb, kf, v1))
            q_state.append([_dot(qb[:, hs], cn_ref[d, hd]) for hd, hs in enumerate(heads)])
        scaled = []
        for j, (d, _, _) in enumerate(jobs):
            g, bc, qb, kf, v1 = loaded[j]
            per_head = []
            for hd, hs in enumerate(heads):
                ic, fc = d * 8 + hd, d * 8 + N_HEADS + hd
                b_col = bc[:, fc:fc + 1]
                r_col = g[:, ic:ic + 1] - b_col
                b_tot = b_col[L - 1:L] if d == 0 else b_col[0:1]
                m_st = mT_ref[d, hd:hd + 1, 0:1]
                blk_max = [jnp.max(r_col[i * SUB:(i + 1) * SUB], axis=0, keepdims=True) for i in range(nsub)]
                r_rep = jnp.broadcast_to(r_col, (L, D_HEAD))
                subs = []
                for i in range(nsub):
                    rows = slice(i * SUB, (i + 1) * SUB)
                    cols = slice(0, (i + 1) * SUB) if d == 0 else slice(i * SUB, L)
                    seen = blk_max[:i + 1] if d == 0 else blk_max[i:]
                    mu = jnp.maximum(m_st, functools.reduce(jnp.maximum, seen))
                    kp = (kf[cols, hs] * jnp.exp(r_rep[cols] - mu)).astype(BF16)
                    floor = jnp.exp(-b_col[rows] - mu)
                    subs.append((rows, cols, mu, kp, floor))
                per_head.append((b_tot, m_st, subs))
            scaled.append(per_head)
        scores = [[[_dot_nt(loaded[j][2][rows, hs], kp) for (rows, _, _, kp, _) in scaled[j][hd][2]]
                   for hd, hs in enumerate(heads)] for j in range(2)]
        full = [nsub - 1, 0]
        update = [[_dot_tn(scaled[j][hd][2][full[j]][3], loaded[j][4][hd]) for hd in range(N_HEADS)]
                  for j in range(2)]
        for j, (d, t0, o_ref) in enumerate(jobs):
            mask = tril if d == 0 else triu
            for hd, hs in enumerate(heads):
                b_tot, m_st, subs = scaled[j][hd]
                outs = []
                for i, (rows, cols, mu, _, floor) in enumerate(subs):
                    w = jnp.where(mask[rows, cols], scores[j][hd][i], 0.0)
                    nd = _dot(w, loaded[j][4][hd][cols]) + jnp.exp(m_st - mu) * q_state[j][hd][rows]
                    outs.append(nd[:, 0:D_HEAD] / jnp.maximum(jnp.abs(nd[:, D_HEAD:]), floor))
                o_ref[pl.ds(t0, L), hs] = jnp.concatenate(outs, axis=0)
                mu_full = subs[full[j]][2]
                cn_ref[d, hd] = jnp.exp(m_st - mu_full) * cn_ref[d, hd] + update[j][hd]
                mT_ref[d, hd:hd + 1, :] = jnp.broadcast_to(b_tot + mu_full, (1, LANES))
        return carry

    lax.fori_loop(0, nch, step, 0)

    for d in range(2):
        for hd in range(N_HEADS):
            cT_ref[d, hd] = cn_ref[d, hd, :, 0:D_HEAD]
            nT_ref[d, hd:hd + 1, :] = cn_ref[d, hd, :, D_HEAD:].T[0:1, :]

    def combine(c, carry):
        t0 = pl.multiple_of(c * L, L)
        for hs in heads:
            o = of_ref[pl.ds(t0, L), hs] + ob_ref[pl.ds(t0, L), hs]
            mu = jnp.mean(o, -1, keepdims=True)
            var = jnp.mean(jnp.square(o - mu), -1, keepdims=True)
            o = (o - mu) * lax.rsqrt(var + EPS)
            y_ref[pl.ds(t0, L), hs] = jax.nn.sigmoid(og_ref[pl.ds(t0, L), hs]) * (o * ng_ref[:, hs])
        return carry

    lax.fori_loop(0, nch, combine, 0)


def _mlstm_call(z2, B, c0, n0, m0, norm_g):
    T = z2.shape[0] // B
    col = functools.partial(_seq_col_spec, T)
    c_spec = pl.BlockSpec((None, 2, N_HEADS, D_HEAD, D_HEAD), lambda b: (b, 0, 0, 0, 0))
    v_spec = pl.BlockSpec((None, 2, N_HEADS, D_HEAD), lambda b: (b, 0, 0, 0))
    return pl.pallas_call(
        _mlstm_kernel,
        grid=(B,),
        in_specs=[col(7), col(8), col(9), col(10),
                  pl.BlockSpec((T, LANES), lambda b: (b, GATE_BLK)),
                  pl.BlockSpec((1, D_GROUP), lambda b: (0, 0)),
                  c_spec, v_spec, v_spec],
        out_specs=[pl.BlockSpec((T, D_GROUP), lambda b: (b, 0)), c_spec, v_spec, v_spec],
        out_shape=[jax.ShapeDtypeStruct((B * T, D_GROUP), F32),
                   jax.ShapeDtypeStruct((B, 2, N_HEADS, D_HEAD, D_HEAD), F32),
                   jax.ShapeDtypeStruct((B, 2, N_HEADS, D_HEAD), F32),
                   jax.ShapeDtypeStruct((B, 2, N_HEADS, D_HEAD), F32)],
        scratch_shapes=[pltpu.VMEM((2, N_HEADS, D_HEAD, 2 * D_HEAD), F32),
                        pltpu.VMEM((T, D_GROUP), F32), pltpu.VMEM((T, D_GROUP), F32)],
        compiler_params=_cparams(1),
        name="mlstm",
    )(z2, z2, z2, z2, z2, norm_g, c0, n0, m0)


def _layer_norm(v, g, b):
    mu = jnp.mean(v, -1, keepdims=True)
    var = jnp.mean(jnp.square(v - mu), -1, keepdims=True)
    return (v - mu) * lax.rsqrt(var + EPS) * g + b


def _route(logits_t, b_router):
    mx = jnp.max(logits_t, axis=0, keepdims=True)
    ex = jnp.exp(logits_t - mx)
    scores = ex / jnp.sum(ex, axis=0, keepdims=True)
    sel = scores + b_router
    rows = [sel[e:e + 1] for e in range(N_EXPERTS)]
    first, second, grp = [], [], []
    for gi in range(N_GROUPS):
        r = rows[gi * EXPERTS_PER_GROUP:(gi + 1) * EXPERTS_PER_GROUP]
        m1 = functools.reduce(jnp.maximum, r)
        taken = jnp.zeros_like(m1, dtype=jnp.bool_)
        f = []
        for x in r:
            hit = jnp.logical_and(x == m1, jnp.logical_not(taken))
            f.append(hit)
            taken = jnp.logical_or(taken, hit)
        rest = [jnp.where(fi, -jnp.inf, x) for fi, x in zip(f, r)]
        m2 = functools.reduce(jnp.maximum, rest)
        taken = jnp.zeros_like(m1, dtype=jnp.bool_)
        s = []
        for x in rest:
            hit = jnp.logical_and(x == m2, jnp.logical_not(taken))
            s.append(hit)
            taken = jnp.logical_or(taken, hit)
        first.append(f)
        second.append(s)
        grp.append(m1 + m2)
    gmax = functools.reduce(jnp.maximum, grp)
    taken = jnp.zeros_like(gmax, dtype=jnp.bool_)
    chosen = []
    group = jnp.zeros_like(gmax, dtype=jnp.int32)
    for gi in range(N_GROUPS):
        best = jnp.logical_and(grp[gi] == gmax, jnp.logical_not(taken))
        taken = jnp.logical_or(taken, best)
        group = jnp.where(best, gi, group)
        for j in range(EXPERTS_PER_GROUP):
            chosen.append(jnp.logical_and(best, jnp.logical_or(first[gi][j], second[gi][j])))
    picked = [jnp.where(ch, scores[e:e + 1], 0.0) for e, ch in enumerate(chosen)]
    denom = functools.reduce(lambda a, b: a + b, picked)
    return jnp.concatenate([p / denom for p in picked], axis=0), group


SLABS = D_MODEL // LANES


def _outproj_kernel(yr_ref, yh_ref, ym_ref, x_ref, mod_ref, w_ref, lg_ref, lb_ref,
                    wr_ref, br_ref, x1_ref, hx_ref, grp_ref):
    tm = x_ref.shape[0]
    m = mod_ref[...]
    y = (jnp.dot(yr_ref[...].astype(BF16), w_ref[0:D_GROUP], preferred_element_type=F32)
         + jnp.dot(yh_ref[...].astype(BF16), w_ref[D_GROUP:2 * D_GROUP], preferred_element_type=F32)
         + jnp.dot(ym_ref[...].astype(BF16), w_ref[2 * D_GROUP:], preferred_element_type=F32))
    x1 = _layer_norm(ALPHA * x_ref[...] + m[GATE1:GATE1 + 1] * y, lg_ref[...], lb_ref[...])
    x1_ref[...] = x1
    hm = x1 * (1.0 + m[SCALE2:SCALE2 + 1]) + m[SHIFT2:SHIFT2 + 1]
    for j in range(SLABS):
        hx_ref[:, j, :] = hm[:, j * LANES:(j + 1) * LANES]
    logits_t = lax.dot_general(wr_ref[...], hm, (((1,), (1,)), ((), ())), precision=HIGHEST,
                               preferred_element_type=F32)
    cmb_t, group = _route(logits_t, br_ref[...])
    cmb_rows = jnp.concatenate([cmb_t, jnp.zeros((LANES - N_EXPERTS, tm), F32)], axis=0)
    hx_ref[:, SLABS, :] = cmb_rows.T
    grp_ref[...] = group


def _outproj_call(yr, yh, ym, x, T, mod, w_out, l, ln_g, ln_b, w_router_t, b_router):
    n = x.shape[0]
    tm = OUTPROJ_TILE if (mod.shape[0] == 1 or T % OUTPROJ_TILE == 0) else ROW_TILE
    row = lambda w: pl.BlockSpec((tm, w), lambda i: (i, 0))
    full = lambda a: pl.BlockSpec(a.shape, lambda i: (0,) * a.ndim)
    return pl.pallas_call(
        _outproj_kernel,
        grid=(n // tm,),
        in_specs=[row(D_GROUP), row(D_GROUP), row(D_GROUP), row(D_MODEL), _mod_spec(mod, T, tm),
                  pl.BlockSpec((None, D_MIX, D_MODEL), lambda i: (l, 0, 0)),
                  full(ln_g), full(ln_b), full(w_router_t), full(b_router)],
        out_specs=[row(D_MODEL), pl.BlockSpec((tm, SLABS + 1, LANES), lambda i: (i, 0, 0)),
                   pl.BlockSpec((1, tm), lambda i: (0, i))],
        out_shape=[jax.ShapeDtypeStruct((n, D_MODEL), F32),
                   jax.ShapeDtypeStruct((n, SLABS + 1, LANES), F32),
                   jax.ShapeDtypeStruct((1, n), jnp.int32)],
        compiler_params=_cparams(1),
        name="outproj_ln_router",
    )(yr, yh, ym, x, mod, w_out, ln_g, ln_b, w_router_t, b_router)


MOE_TILE = 256


def _row_gather(idx_ref, first, n_rows, src_hbm, dst_ref, sem):
    def body(p, carry):
        for k in range(2):
            r = 2 * p + k
            pltpu.make_async_copy(src_hbm.at[idx_ref[first + r]], dst_ref.at[:, r, :], sem).start(priority=k)
        return carry
    lax.fori_loop(0, n_rows // 2, body, 0, unroll=4)


def _row_gather_wait(n_rows, src_hbm, dst_ref, sem):
    def body(r, carry):
        pltpu.make_async_copy(src_hbm.at[0], dst_ref.at[:, 0, :], sem).wait()
        return carry
    lax.fori_loop(0, n_rows, body, 0, unroll=8)


def _gathered_tile(idx_ref, src_hbm, buf_ref, sem_ref, tm):
    i = pl.program_id(0)
    slot = lax.rem(i, 2)

    @pl.when(i == 0)
    def _():
        _row_gather(idx_ref, 0, tm, src_hbm, buf_ref.at[0], sem_ref.at[0])

    @pl.when(i + 1 < pl.num_programs(0))
    def _():
        _row_gather(idx_ref, (i + 1) * tm, tm, src_hbm, buf_ref.at[1 - slot], sem_ref.at[1 - slot])

    _row_gather_wait(tm, src_hbm, buf_ref.at[slot], sem_ref.at[slot])
    return buf_ref.at[slot]


def _moe_kernel(tile_group_ref, src_ref, hx_hbm, wg_ref, wu_ref, wd_ref, o_ref, buf_ref, sem_ref):
    tm = o_ref.shape[0]
    rows_ref = _gathered_tile(src_ref, hx_hbm, buf_ref, sem_ref, tm)
    hm = jnp.concatenate([rows_ref[j] for j in range(SLABS)], axis=1).astype(BF16)
    cmb = rows_ref[SLABS]
    lane = lax.broadcasted_iota(jnp.int32, cmb.shape, 1)
    first_expert = tile_group_ref[pl.program_id(0)] * EXPERTS_PER_GROUP
    acc = None
    for k in range(EXPERTS_PER_GROUP):
        ce = jnp.sum(jnp.where(lane == first_expert + k, cmb, 0.0), axis=1, keepdims=True)
        hg = jnp.dot(hm, wg_ref[k], preferred_element_type=F32)
        hu = jnp.dot(hm, wu_ref[k], preferred_element_type=F32)
        part = jnp.dot((_silu(hg) * hu * ce).astype(BF16), wd_ref[k], preferred_element_type=F32)
        acc = part if acc is None else acc + part
    for j in range(SLABS):
        o_ref[:, j, :] = acc[:, j * LANES:(j + 1) * LANES]


def _moe_call(hx, tile_group, src, w_gate, w_up, w_down, l):
    n_pad = src.shape[0]
    tm = MOE_TILE
    grp_w = lambda shape: pl.BlockSpec((None,) + shape, lambda i, tg, sr: (l, tg[i], 0, 0))
    return pl.pallas_call(
        _moe_kernel,
        grid_spec=pltpu.PrefetchScalarGridSpec(
            num_scalar_prefetch=2,
            grid=(n_pad // tm,),
            in_specs=[pl.BlockSpec(memory_space=pl.ANY),
                      grp_w((EXPERTS_PER_GROUP, D_MODEL, D_FF)), grp_w((EXPERTS_PER_GROUP, D_MODEL, D_FF)),
                      grp_w((EXPERTS_PER_GROUP, D_FF, D_MODEL))],
            out_specs=pl.BlockSpec((tm, SLABS, LANES), lambda i, tg, sr: (i, 0, 0)),
            scratch_shapes=[pltpu.VMEM((2, SLABS + 1, tm, LANES), F32), pltpu.SemaphoreType.DMA((2,))]),
        out_shape=jax.ShapeDtypeStruct((n_pad, SLABS, LANES), F32),
        compiler_params=_cparams(1),
        name="moe_sorted",
    )(tile_group, src, hx, w_gate, w_up, w_down)


def _ln2_kernel(pos_ref, f_hbm, x1_ref, mod_ref, lg_ref, lb_ref, o_ref, buf_ref, sem_ref):
    tm = o_ref.shape[0]
    rows_ref = _gathered_tile(pos_ref, f_hbm, buf_ref, sem_ref, tm)
    f = jnp.concatenate([rows_ref[j] for j in range(SLABS)], axis=1)
    v = ALPHA * x1_ref[...] + mod_ref[GATE2:GATE2 + 1, :] * f
    o_ref[...] = _layer_norm(v, lg_ref[...], lb_ref[...])


def _ln2_call(f_sorted, pos, x1, T, mod, ln_g, ln_b):
    n = x1.shape[0]
    tm = ROW_TILE
    per_seq = mod.shape[0] > 1
    return pl.pallas_call(
        _ln2_kernel,
        grid_spec=pltpu.PrefetchScalarGridSpec(
            num_scalar_prefetch=1,
            grid=(n // tm,),
            in_specs=[pl.BlockSpec(memory_space=pl.ANY),
                      pl.BlockSpec((tm, D_MODEL), lambda i, ps: (i, 0)),
                      pl.BlockSpec((None, 6, D_MODEL), lambda i, ps: ((i * tm) // T if per_seq else 0, 0, 0)),
                      pl.BlockSpec((1, D_MODEL), lambda i, ps: (0, 0)),
                      pl.BlockSpec((1, D_MODEL), lambda i, ps: (0, 0))],
            out_specs=pl.BlockSpec((tm, D_MODEL), lambda i, ps: (i, 0)),
            scratch_shapes=[pltpu.VMEM((2, SLABS, tm, LANES), F32), pltpu.SemaphoreType.DMA((2,))]),
        out_shape=jax.ShapeDtypeStruct((n, D_MODEL), F32),
        compiler_params=_cparams(1),
        name="moe_combine_ln",
    )(pos, f_sorted, x1, mod, ln_g, ln_b)


def _dispatch_plan(group, tm):
    n = group.shape[0]
    n_pad = n + N_GROUPS * tm
    onehot = (group[:, None] == jnp.arange(N_GROUPS, dtype=jnp.int32)[None, :]).astype(jnp.int32)
    csum = jnp.cumsum(onehot, axis=0)
    rank = jnp.take_along_axis(csum, group[:, None], axis=1)[:, 0] - 1
    padded = ((csum[-1] + tm - 1) // tm) * tm
    ends = jnp.cumsum(padded)
    pos = (ends - padded)[group] + rank
    src = jnp.zeros((n_pad,), jnp.int32).at[pos].set(jnp.arange(n, dtype=jnp.int32))
    tile_start = jnp.arange(n_pad // tm, dtype=jnp.int32) * tm
    tile_group = jnp.sum((tile_start[:, None] >= ends[None, :]).astype(jnp.int32), axis=1)
    tile_group = jnp.minimum(tile_group, N_GROUPS - 1)
    return pos, src, tile_group


def _grid_pos_embed(n_tokens):
    rows = n_tokens // GRID_W
    r = jnp.repeat(jnp.arange(rows, dtype=F32), GRID_W)
    c = jnp.tile(jnp.arange(GRID_W, dtype=F32), rows)
    q = D_MODEL // 4
    freq = jnp.exp(-jnp.log(10000.0) * jnp.arange(q, dtype=F32) / q)
    ar = r[:, None] * freq
    ac = c[:, None] * freq
    return jnp.concatenate([jnp.sin(ar), jnp.cos(ar), jnp.sin(ac), jnp.cos(ac)], axis=-1)


def _lru_gate_params(wa, ba, wx, bx, lam):
    ncb = D_GROUP // LANES

    def dense(w):
        z = jnp.zeros((ncb, LANES, LANES), F32)
        z = z.at[:, :LRU_BLOCK, :LRU_BLOCK].set(w[0::2])
        return z.at[:, LRU_BLOCK:, LRU_BLOCK:].set(w[1::2])

    wg = jnp.concatenate([dense(wa[0]), dense(wx[0]), dense(wa[1]), dense(wx[1])], axis=-1).astype(BF16)
    per_blk = lambda v: v.reshape(ncb, 1, LANES)
    bg = jnp.concatenate([per_blk(ba[0]), per_blk(bx[0]), per_blk(ba[1]), per_blk(bx[1])], axis=-1)
    lm = jnp.concatenate([per_blk(lam[0]), per_blk(lam[1])], axis=-1)
    return wg, bg, lm


def _mixer_states(B, l, states):
    if states is None:
        return (jnp.zeros((2, B, D_GROUP), F32),
                jnp.zeros((B, 2, N_HEADS, D_HEAD, D_HEAD), F32),
                jnp.zeros((B, 2, N_HEADS, D_HEAD, D_HEAD), F32),
                jnp.zeros((B, 2, N_HEADS, D_HEAD), F32),
                jnp.zeros((B, 2, N_HEADS, D_HEAD), F32))
    h, s, c, n, m = states
    return (jnp.swapaxes(h[:, l], 0, 1), s[:, l], c[:, l], n[:, l],
            jnp.broadcast_to(m[:, l][..., None], (B, 2, N_HEADS, D_HEAD)))


def kernel(x_prompt, x_sample, state_lru_h, state_hgrn_S, state_mlstm_C, state_mlstm_n, state_mlstm_m,
           c, c_ctx, w_ada, b_ada, w_in, b_in, conv_w, conv_b, lru_wa, lru_ba, lru_wx, lru_bx, lru_lam,
           hg_lb, hg_norm_g, ml_norm_g, w_out, ln1_g, ln1_b, ln2_g, ln2_b,
           w_router, b_router, w_gate, w_up, w_down):
    Bp, Tp, _ = x_prompt.shape
    Bs, Ts, _ = x_sample.shape
    assert Bs % SUBLANES == 0 and Bp % SUBLANES == 0 and Bs + 1 <= 16
    assert Tp % ROW_TILE == 0 and Ts % ROW_TILE == 0

    lb_soft = jax.nn.softmax(hg_lb.astype(F32), axis=0)
    hg_lower = jnp.cumsum(lb_soft, axis=0) - lb_soft[0:1]

    c16 = jnp.concatenate([c, c_ctx[None], jnp.zeros((16 - Bs - 1, D_MODEL), F32)], axis=0)
    mod = _ada_call(c16, w_ada, b_ada)

    w_in_p = jnp.pad(w_in, ((0, 0), (0, 0), (0, D_IN_PAD - D_IN))).astype(BF16)
    b_in_p = jnp.pad(b_in, ((0, 0), (0, D_IN_PAD - D_IN))).reshape(DEPTH, 1, D_IN_PAD)
    w_out_b = w_out.astype(BF16)
    w_gate_b, w_up_b, w_down_b = w_gate.astype(BF16), w_up.astype(BF16), w_down.astype(BF16)
    w_router_t = w_router.T
    b_router_c = b_router.reshape(N_EXPERTS, 1)

    xs = x_sample + _grid_pos_embed(Ts).astype(x_sample.dtype)
    streams = [
        dict(x=x_prompt.reshape(Bp * Tp, D_MODEL), T=Tp, B=Bp, states=None),
        dict(x=xs.reshape(Bs * Ts, D_MODEL), T=Ts, B=Bs,
             states=(state_lru_h, state_hgrn_S, state_mlstm_C, state_mlstm_n, state_mlstm_m)),
    ]
    finals = []
    for l in range(DEPTH):
        wg, bg, lm = _lru_gate_params(lru_wa[l], lru_ba[l], lru_wx[l], lru_bx[l], lru_lam[l])
        row = lambda v: v[l].reshape(1, -1)
        for si, st in enumerate(streams):
            T, B = st["T"], st["B"]
            mod_s = (mod[l, :Bs] if si == 1 else mod[l, Bs:Bs + 1]).reshape(-1, 6, D_MODEL)
            h0, s0, c0, n0, m0 = _mixer_states(B, l, st["states"])
            z = _inproj_call(st["x"], T, mod_s, w_in_p, b_in_p, l)
            y_r, h_T = _rglru_call(z.reshape(B, T, D_IN_PAD), h0, conv_w[l], row(conv_b), wg, bg, lm)
            y_h, s_T = _hgrn_call(z, B, s0, hg_lower[l], row(hg_norm_g))
            y_m, c_T, n_T, m_T = _mlstm_call(z, B, c0, n0, m0, row(ml_norm_g))
            x1, hx, group = _outproj_call(y_r.reshape(B * T, D_GROUP), y_h, y_m, st["x"], T, mod_s, w_out_b, l,
                                          row(ln1_g), row(ln1_b), w_router_t, b_router_c)
            pos, src, tile_group = _dispatch_plan(group[0], MOE_TILE)
            f_sorted = _moe_call(hx, tile_group, src, w_gate_b, w_up_b, w_down_b, l)
            st["x"] = _ln2_call(f_sorted, pos, x1, T, mod_s, row(ln2_g), row(ln2_b))
            if si == 0:
                finals.append((jnp.swapaxes(h_T, 0, 1), s_T, c_T, n_T, m_T[..., 0]))
    outs = [st["x"].reshape(st["B"], st["T"], D_MODEL) for st in streams]
    stack = lambda i: jnp.stack([f[i] for f in finals], axis=1)
    return (outs[0], outs[1], stack(0), stack(1), stack(2), stack(3), stack(4))
```

```python
import functools

import jax
import jax.numpy as jnp
from jax import lax
from jax.experimental import pallas as pl
from jax.experimental.pallas import tpu as pltpu

F32 = jnp.float32
BF16 = jnp.bfloat16
HIGHEST = lax.Precision.HIGHEST

D_MODEL = 1024
DEPTH = 2
GRID_W = 64
D_GROUP = 512
D_MIX = 3 * D_GROUP
LRU_BLOCKS = 8
LRU_BLOCK = D_GROUP // LRU_BLOCKS
LRU_C = 8.0
N_HEADS = 4
D_HEAD = D_GROUP // N_HEADS
N_EXPERTS = 16
N_GROUPS = 4
EXPERTS_PER_GROUP = N_EXPERTS // N_GROUPS
D_FF = 512
ALPHA = (2.0 * DEPTH) ** 0.25
EPS = 1e-5
NEG = -1e30
TINY = 1e-30
N_GATE = 4 * N_HEADS
D_IN = 11 * D_GROUP + N_GATE
D_IN_PAD = 45 * 128
GATE_BLK = (11 * D_GROUP) // 128

SUBLANES = 8
LANES = 128
VMEM_LIMIT = 56 * 1024 * 1024

LRU_TC = 32
LRU_PAD = SUBLANES
HG_L = 64
HG_SUB = 16
ML_L = 128
ML_SUB = 32


def _cparams(n_axes):
    return pltpu.CompilerParams(dimension_semantics=("arbitrary",) * n_axes,
                                vmem_limit_bytes=VMEM_LIMIT)


def _dot(a, b):
    return jnp.dot(a.astype(BF16), b.astype(BF16), preferred_element_type=F32)


def _dot_nt(a, b):
    return lax.dot_general(a.astype(BF16), b.astype(BF16), (((1,), (1,)), ((), ())),
                           preferred_element_type=F32)


def _dot_tn(a, b):
    return lax.dot_general(a.astype(BF16), b.astype(BF16), (((0,), (0,)), ((), ())),
                           preferred_element_type=F32)


def _dot_f32(a, b):
    return jnp.dot(a, b, precision=HIGHEST, preferred_element_type=F32)


def _dot_split3(mask, x, mask_on_left):
    m = jnp.where(mask, 1.0, 0.0).astype(BF16)
    hi = x.astype(BF16)
    r1 = x - hi.astype(F32)
    mid = r1.astype(BF16)
    lo = (r1 - mid.astype(F32)).astype(BF16)
    mm = (lambda p: jnp.dot(m, p, preferred_element_type=F32)) if mask_on_left else (
        lambda p: jnp.dot(p, m, preferred_element_type=F32))
    return mm(hi) + mm(mid) + mm(lo)


def _softplus(x):
    return jnp.maximum(x, 0.0) + jnp.log1p(jnp.exp(-jnp.abs(x)))


def _silu(x):
    return x * jax.nn.sigmoid(x)


def _tri_masks(n):
    r = lax.broadcasted_iota(jnp.int32, (n, n), 0)
    c = lax.broadcasted_iota(jnp.int32, (n, n), 1)
    return r >= c, r <= c


def _ada_kernel(c_ref, w_ref, b_ref, o_ref):
    o_ref[0] = _dot_f32(_silu(c_ref[...]), w_ref[0]) + b_ref[0]


def _ada_call(c16, w_ada, b_ada):
    tn = 1536
    return pl.pallas_call(
        _ada_kernel,
        grid=(DEPTH, 6 * D_MODEL // tn),
        in_specs=[pl.BlockSpec((16, D_MODEL), lambda l, j: (0, 0)),
                  pl.BlockSpec((1, D_MODEL, tn), lambda l, j: (l, 0, j)),
                  pl.BlockSpec((1, 1, tn), lambda l, j: (l, 0, j))],
        out_specs=pl.BlockSpec((1, 16, tn), lambda l, j: (l, 0, j)),
        out_shape=jax.ShapeDtypeStruct((DEPTH, 16, 6 * D_MODEL), F32),
        compiler_params=_cparams(2),
        name="adaln",
    )(c16, w_ada, b_ada.reshape(DEPTH, 1, 6 * D_MODEL))


INPROJ_CW = 640
ROW_TILE = 256
WIDE_TILE = 512

SHIFT1, SCALE1, GATE1, SHIFT2, SCALE2, GATE2 = range(6)


def _mod_spec(mod, T, tm, n_grid_axes=1):
    per_seq = mod.shape[0] > 1
    if n_grid_axes == 1:
        return pl.BlockSpec((None, 6, D_MODEL), lambda i: ((i * tm) // T if per_seq else 0, 0, 0))
    return pl.BlockSpec((None, 6, D_MODEL), lambda i, e: ((i * tm) // T if per_seq else 0, 0, 0))


def _inproj_kernel(x_ref, mod_ref, w_ref, b_ref, z_ref):
    m = mod_ref[...]
    hb = (x_ref[...] * (1.0 + m[SCALE1:SCALE1 + 1]) + m[SHIFT1:SHIFT1 + 1]).astype(BF16)
    for j in range(D_IN_PAD // INPROJ_CW):
        cs = slice(j * INPROJ_CW, (j + 1) * INPROJ_CW)
        z_ref[:, cs] = jnp.dot(hb, w_ref[:, cs], preferred_element_type=F32) + b_ref[:, cs]


def _inproj_call(x, T, mod, w, b, l):
    n = x.shape[0]
    tm = WIDE_TILE if (mod.shape[0] == 1 or T % WIDE_TILE == 0) else ROW_TILE
    return pl.pallas_call(
        _inproj_kernel,
        grid=(n // tm,),
        in_specs=[pl.BlockSpec((tm, D_MODEL), lambda i: (i, 0)),
                  _mod_spec(mod, T, tm),
                  pl.BlockSpec((None, D_MODEL, D_IN_PAD), lambda i: (l, 0, 0), pipeline_mode=pl.Buffered(1)),
                  pl.BlockSpec((None, 1, D_IN_PAD), lambda i: (l, 0, 0))],
        out_specs=pl.BlockSpec((tm, D_IN_PAD), lambda i: (i, 0)),
        out_shape=jax.ShapeDtypeStruct((n, D_IN_PAD), F32),
        compiler_params=_cparams(1),
        name="inproj",
    )(x, mod, w, b)


def _rglru_kernel(xr_ref, gr_ref, cw_ref, cb_ref, wg_ref, bg_ref, lam_ref, h0_ref,
                  y_ref, hT_ref, xs_ref, os_ref, ob_ref):
    T = xr_ref.shape[1]
    P = T + LRU_PAD
    TC = LRU_TC
    nch = T // TC
    CP = 256
    zeros = jnp.zeros((LRU_PAD, LANES), F32)
    for s in range(SUBLANES + 1):
        xs_ref[s * P:s * P + LRU_PAD] = zeros
    for s in range(SUBLANES):
        def copy_in(c, carry, s=s):
            r0 = pl.multiple_of(c * CP, CP)
            xs_ref[pl.ds(s * P + LRU_PAD + r0, CP)] = xr_ref[s, pl.ds(r0, CP)]
            return carry
        lax.fori_loop(0, T // CP, copy_in, 0)

    cw = cw_ref[...]
    cb = cb_ref[...]
    sp = _softplus(-lam_ref[0])

    def gates(t0, d):
        win = [xs_ref[pl.ds(t0 + k + LRU_PAD - 2, SUBLANES, stride=P)] for k in range(TC + 3)]
        xc = jnp.stack([cw[0:1] * win[i] + cw[1:2] * win[i + 1] + cw[2:3] * win[i + 2] + cw[3:4] * win[i + 3] + cb
                        for i in range(TC)])
        x2 = xc.reshape(TC * SUBLANES, LANES)
        g = (jnp.dot(x2.astype(BF16), wg_ref[0, :, d * 256:(d + 1) * 256], preferred_element_type=F32)
             + bg_ref[0, :, d * 256:(d + 1) * 256])
        r = jax.nn.sigmoid(g[:, :LANES])
        i = jax.nn.sigmoid(g[:, LANES:])
        log_a = -LRU_C * r * sp[:, d * LANES:(d + 1) * LANES]
        a = jnp.exp(log_a)
        u = jnp.sqrt(jnp.maximum(1.0 - jnp.exp(2.0 * log_a), 0.0)) * (i * x2)
        return a.reshape(TC, SUBLANES, LANES), u.reshape(TC, SUBLANES, LANES)

    def step(c, carry):
        hf, hb = carry
        tf = pl.multiple_of(c * TC, TC)
        tb = pl.multiple_of((nch - 1 - c) * TC, TC)
        af, uf = gates(tf, 0)
        ab, ub = gates(tb, 1)
        for i in range(TC):
            hf = af[i] * hf + uf[i]
            os_ref[pl.ds(tf + i, SUBLANES, stride=P)] = hf
            k = TC - 1 - i
            hb = ab[k] * hb + ub[k]
            ob_ref[pl.ds(tb + k, SUBLANES, stride=P)] = hb
        return hf, hb

    hT_ref[0], hT_ref[1] = lax.fori_loop(0, nch, step, (h0_ref[0], h0_ref[1]))

    for s in range(SUBLANES):
        def copy_out(c, carry, s=s):
            r0 = pl.multiple_of(c * CP, CP)
            o = os_ref[pl.ds(s * P + r0, CP)] + ob_ref[pl.ds(s * P + r0, CP)]
            y_ref[s, pl.ds(r0, CP)] = jax.nn.gelu(gr_ref[s, pl.ds(r0, CP)]) * o
            return carry
        lax.fori_loop(0, T // CP, copy_out, 0)


def _rglru_call(z3, h0, cw, cb, wg, bg, lam):
    B, T, _ = z3.shape
    nb = B // SUBLANES
    ncb = D_GROUP // LANES
    blk = (SUBLANES, T, LANES)
    pitch = T + LRU_PAD
    return pl.pallas_call(
        _rglru_kernel,
        grid=(nb, ncb),
        in_specs=[pl.BlockSpec(blk, lambda b, c: (b, 0, c)),
                  pl.BlockSpec(blk, lambda b, c: (b, 0, ncb + c)),
                  pl.BlockSpec((4, LANES), lambda b, c: (0, c)),
                  pl.BlockSpec((1, LANES), lambda b, c: (0, c)),
                  pl.BlockSpec((1, LANES, 4 * LANES), lambda b, c: (c, 0, 0)),
                  pl.BlockSpec((1, 1, 4 * LANES), lambda b, c: (c, 0, 0)),
                  pl.BlockSpec((1, 1, 2 * LANES), lambda b, c: (c, 0, 0)),
                  pl.BlockSpec((2, SUBLANES, LANES), lambda b, c: (0, b, c))],
        out_specs=[pl.BlockSpec(blk, lambda b, c: (b, 0, c)),
                   pl.BlockSpec((2, SUBLANES, LANES), lambda b, c: (0, b, c))],
        out_shape=[jax.ShapeDtypeStruct((B, T, D_GROUP), F32),
                   jax.ShapeDtypeStruct((2, B, D_GROUP), F32)],
        scratch_shapes=[pltpu.VMEM((SUBLANES * pitch + LRU_PAD, LANES), F32),
                        pltpu.VMEM((SUBLANES * pitch, LANES), F32),
                        pltpu.VMEM((SUBLANES * pitch, LANES), F32)],
        compiler_params=_cparams(2),
        name="rglru",
    )(z3, z3, cw, cb, wg, bg, lam, h0)


def _hgrn_kernel(q_ref, ff_ref, fb_ref, v_ref, og_ref, lb_ref, ng_ref, s0_ref,
                 y_ref, sT_ref, st_ref, of_ref, ob_ref):
    T = q_ref.shape[0]
    L = HG_L
    SUB = HG_SUB
    nch = T // L
    nsub = L // SUB
    tril, triu = _tri_masks(L)
    heads = [slice(hd * D_HEAD, (hd + 1) * D_HEAD) for hd in range(N_HEADS)]

    for d in range(2):
        for hd in range(N_HEADS):
            st_ref[d, hd] = s0_ref[d, hd].T

    def step(c, carry):
        jobs = ((0, pl.multiple_of(c * L, L), of_ref), (1, pl.multiple_of((nch - 1 - c) * L, L), ob_ref))
        gated = []
        for d, t0, _ in jobs:
            q = _silu(q_ref[pl.ds(t0, L), :])
            v = v_ref[pl.ds(t0, L), :].astype(BF16)
            lb = lb_ref[d:d + 1, :]
            sig = jax.nn.sigmoid((ff_ref if d == 0 else fb_ref)[pl.ds(t0, L), :])
            log_f = jnp.log(jnp.maximum(lb + (1.0 - lb) * sig, TINY))
            kk = (1.0 - lb) * (1.0 - sig)
            mask = tril if d == 0 else triu
            b = _dot_split3(mask, log_f, mask_on_left=True)
            gated.append((q, v, kk, b, mask))
        factored = []
        for (d, _, _), (q, v, kk, b, mask) in zip(jobs, gated):
            b_tot = b[L - 1:L] if d == 0 else b[0:1]
            subs = []
            for i in range(nsub):
                rows = slice(i * SUB, (i + 1) * SUB)
                if d == 0:
                    cols = slice(0, (i + 1) * SUB)
                    edge = b[i * SUB - 1:i * SUB] if i > 0 else 0.0
                else:
                    cols = slice(i * SUB, L)
                    edge = b[(i + 1) * SUB:(i + 1) * SUB + 1] if i < nsub - 1 else 0.0
                subs.append((rows, cols, (q[rows] * jnp.exp(b[rows] - edge)).astype(BF16),
                             (kk[cols] * jnp.exp(edge - b[cols])).astype(BF16)))
            factored.append(((q * jnp.exp(b)).astype(BF16), (kk * jnp.exp(b_tot - b)).astype(BF16),
                             jnp.exp(b_tot), subs))
        scores = [[[_dot_nt(qs[:, hs], ks[:, hs]) for (_, _, qs, ks) in subs] for hs in heads]
                  for (_, _, _, subs) in factored]
        states = [[st_ref[d, hd] for hd in range(N_HEADS)] for d, _, _ in jobs]
        inter = [[_dot_nt(qdec[:, hs], states[j][hd]) for hd, hs in enumerate(heads)]
                 for j, (qdec, _, _, _) in enumerate(factored)]
        update = [[_dot_tn(gated[j][1][:, hs], kdec[:, hs]) for hs in heads]
                  for j, (_, kdec, _, _) in enumerate(factored)]
        for j, (d, t0, o_ref) in enumerate(jobs):
            v, mask = gated[j][1], gated[j][4]
            dec, subs = factored[j][2], factored[j][3]
            for hd, hs in enumerate(heads):
                intra = [_dot(jnp.where(mask[rows, cols], scores[j][hd][i], 0.0), v[cols, hs])
                         for i, (rows, cols, _, _) in enumerate(subs)]
                o_ref[pl.ds(t0, L), hs] = jnp.concatenate(intra, axis=0) + inter[j][hd]
                st_ref[d, hd] = states[j][hd] * dec[:, hs] + update[j][hd]
        return carry

    lax.fori_loop(0, nch, step, 0)

    def combine(c, carry):
        t0 = pl.multiple_of(c * L, L)
        for hs in heads:
            o = of_ref[pl.ds(t0, L), hs] + ob_ref[pl.ds(t0, L), hs]
            o = o * lax.rsqrt(jnp.mean(jnp.square(o), -1, keepdims=True) + EPS)
            y_ref[pl.ds(t0, L), hs] = o * ng_ref[:, hs] * _silu(og_ref[pl.ds(t0, L), hs])
        return carry

    lax.fori_loop(0, nch, combine, 0)

    for d in range(2):
        for hd in range(N_HEADS):
            sT_ref[d, hd] = st_ref[d, hd].T


def _seq_col_spec(T, j):
    return pl.BlockSpec((T, D_GROUP), lambda b, j=j: (b, j))


def _hgrn_call(z2, B, s0, lower, norm_g):
    T = z2.shape[0] // B
    col = functools.partial(_seq_col_spec, T)
    st_spec = pl.BlockSpec((None, 2, N_HEADS, D_HEAD, D_HEAD), lambda b: (b, 0, 0, 0, 0))
    return pl.pallas_call(
        _hgrn_kernel,
        grid=(B,),
        in_specs=[col(2), col(3), col(4), col(5), col(6),
                  pl.BlockSpec((2, D_GROUP), lambda b: (0, 0)),
                  pl.BlockSpec((1, D_GROUP), lambda b: (0, 0)),
                  st_spec],
        out_specs=[pl.BlockSpec((T, D_GROUP), lambda b: (b, 0)), st_spec],
        out_shape=[jax.ShapeDtypeStruct((B * T, D_GROUP), F32),
                   jax.ShapeDtypeStruct((B, 2, N_HEADS, D_HEAD, D_HEAD), F32)],
        scratch_shapes=[pltpu.VMEM((2, N_HEADS, D_HEAD, D_HEAD), F32),
                        pltpu.VMEM((T, D_GROUP), F32),
                        pltpu.VMEM((T, D_GROUP), F32)],
        compiler_params=_cparams(1),
        name="hgrn2",
    )(z2, z2, z2, z2, z2, lower, norm_g, s0)


def _mlstm_kernel(q_ref, k_ref, v_ref, og_ref, g_ref, ng_ref, c0_ref, n0_ref, m0_ref,
                  y_ref, cT_ref, nT_ref, mT_ref, cn_ref, of_ref, ob_ref):
    T = q_ref.shape[0]
    L = ML_L
    SUB = ML_SUB
    nch = T // L
    nsub = L // SUB
    tril, triu = _tri_masks(L)
    gcol = lax.broadcasted_iota(jnp.int32, (L, N_GATE), 1)
    heads = [slice(hd * D_HEAD, (hd + 1) * D_HEAD) for hd in range(N_HEADS)]
    ones = jnp.ones((L, D_HEAD), BF16)

    for d in range(2):
        for hd in range(N_HEADS):
            cn_ref[d, hd, :, 0:D_HEAD] = c0_ref[d, hd]
            cn_ref[d, hd, :, D_HEAD:] = jnp.broadcast_to(n0_ref[d, hd:hd + 1, :], (D_HEAD, D_HEAD)).T
    mT_ref[...] = m0_ref[...]

    def step(c, carry):
        jobs = ((0, pl.multiple_of(c * L, L), of_ref), (1, pl.multiple_of((nch - 1 - c) * L, L), ob_ref))
        loaded, q_state = [], []
        for d, t0, _ in jobs:
            g = g_ref[pl.ds(t0, L), 0:N_GATE]
            g = jnp.where(gcol % 8 >= N_HEADS, jax.nn.log_sigmoid(g), g)
            bc = _dot_split3(tril if d == 0 else triu, g, mask_on_left=True)
            qb = q_ref[pl.ds(t0, L), :].astype(BF16)
            kf = k_ref[pl.ds(t0, L), :] * (D_HEAD ** -0.5)
            vb = v_ref[pl.ds(t0, L), :].astype(BF16)
            v1 = [jnp.concatenate([vb[:, hs], ones], axis=1) for hs in heads]
            loaded.append((g, bc, qb, kf, v1))
            q_state.append([_dot(qb[:, hs], cn_ref[d, hd]) for hd, hs in enumerate(heads)])
        scaled = []
        for j, (d, _, _) in enumerate(jobs):
            g, bc, qb, kf, v1 = loaded[j]
            per_head = []
            for hd, hs in enumerate(heads):
                ic, fc = d * 8 + hd, d * 8 + N_HEADS + hd
                b_col = bc[:, fc:fc + 1]
                r_col = g[:, ic:ic + 1] - b_col
                b_tot = b_col[L - 1:L] if d == 0 else b_col[0:1]
                m_st = mT_ref[d, hd:hd + 1, 0:1]
                blk_max = [jnp.max(r_col[i * SUB:(i + 1) * SUB], axis=0, keepdims=True) for i in range(nsub)]
                r_rep = jnp.broadcast_to(r_col, (L, D_HEAD))
                subs = []
                for i in range(nsub):
                    rows = slice(i * SUB, (i + 1) * SUB)
                    cols = slice(0, (i + 1) * SUB) if d == 0 else slice(i * SUB, L)
                    seen = blk_max[:i + 1] if d == 0 else blk_max[i:]
                    mu = jnp.maximum(m_st, functools.reduce(jnp.maximum, seen))
                    kp = (kf[cols, hs] * jnp.exp(r_rep[cols] - mu)).astype(BF16)
                    floor = jnp.exp(-b_col[rows] - mu)
                    subs.append((rows, cols, mu, kp, floor))
                per_head.append((b_tot, m_st, subs))
            scaled.append(per_head)
        scores = [[[_dot_nt(loaded[j][2][rows, hs], kp) for (rows, _, _, kp, _) in scaled[j][hd][2]]
                   for hd, hs in enumerate(heads)] for j in range(2)]
        full = [nsub - 1, 0]
        update = [[_dot_tn(scaled[j][hd][2][full[j]][3], loaded[j][4][hd]) for hd in range(N_HEADS)]
                  for j in range(2)]
        for j, (d, t0, o_ref) in enumerate(jobs):
            mask = tril if d == 0 else triu
            for hd, hs in enumerate(heads):
                b_tot, m_st, subs = scaled[j][hd]
                outs = []
                for i, (rows, cols, mu, _, floor) in enumerate(subs):
                    w = jnp.where(mask[rows, cols], scores[j][hd][i], 0.0)
                    nd = _dot(w, loaded[j][4][hd][cols]) + jnp.exp(m_st - mu) * q_state[j][hd][rows]
                    outs.append(nd[:, 0:D_HEAD] / jnp.maximum(jnp.abs(nd[:, D_HEAD:]), floor))
                o_ref[pl.ds(t0, L), hs] = jnp.concatenate(outs, axis=0)
                mu_full = subs[full[j]][2]
                cn_ref[d, hd] = jnp.exp(m_st - mu_full) * cn_ref[d, hd] + update[j][hd]
                mT_ref[d, hd:hd + 1, :] = jnp.broadcast_to(b_tot + mu_full, (1, LANES))
        return carry

    lax.fori_loop(0, nch, step, 0)

    for d in range(2):
        for hd in range(N_HEADS):
            cT_ref[d, hd] = cn_ref[d, hd, :, 0:D_HEAD]
            nT_ref[d, hd:hd + 1, :] = cn_ref[d, hd, :, D_HEAD:].T[0:1, :]

    def combine(c, carry):
        t0 = pl.multiple_of(c * L, L)
        for hs in heads:
            o = of_ref[pl.ds(t0, L), hs] + ob_ref[pl.ds(t0, L), hs]
            mu = jnp.mean(o, -1, keepdims=True)
            var = jnp.mean(jnp.square(o - mu), -1, keepdims=True)
            o = (o - mu) * lax.rsqrt(var + EPS)
            y_ref[pl.ds(t0, L), hs] = jax.nn.sigmoid(og_ref[pl.ds(t0, L), hs]) * (o * ng_ref[:, hs])
        return carry

    lax.fori_loop(0, nch, combine, 0)


def _mlstm_call(z2, B, c0, n0, m0, norm_g):
    T = z2.shape[0] // B
    col = functools.partial(_seq_col_spec, T)
    c_spec = pl.BlockSpec((None, 2, N_HEADS, D_HEAD, D_HEAD), lambda b: (b, 0, 0, 0, 0))
    v_spec = pl.BlockSpec((None, 2, N_HEADS, D_HEAD), lambda b: (b, 0, 0, 0))
    return pl.pallas_call(
        _mlstm_kernel,
        grid=(B,),
        in_specs=[col(7), col(8), col(9), col(10),
                  pl.BlockSpec((T, LANES), lambda b: (b, GATE_BLK)),
                  pl.BlockSpec((1, D_GROUP), lambda b: (0, 0)),
                  c_spec, v_spec, v_spec],
        out_specs=[pl.BlockSpec((T, D_GROUP), lambda b: (b, 0)), c_spec, v_spec, v_spec],
        out_shape=[jax.ShapeDtypeStruct((B * T, D_GROUP), F32),
                   jax.ShapeDtypeStruct((B, 2, N_HEADS, D_HEAD, D_HEAD), F32),
                   jax.ShapeDtypeStruct((B, 2, N_HEADS, D_HEAD), F32),
                   jax.ShapeDtypeStruct((B, 2, N_HEADS, D_HEAD), F32)],
        scratch_shapes=[pltpu.VMEM((2, N_HEADS, D_HEAD, 2 * D_HEAD), F32),
                        pltpu.VMEM((T, D_GROUP), F32), pltpu.VMEM((T, D_GROUP), F32)],
        compiler_params=_cparams(1),
        name="mlstm",
    )(z2, z2, z2, z2, z2, norm_g, c0, n0, m0)


def _layer_norm(v, g, b):
    mu = jnp.mean(v, -1, keepdims=True)
    var = jnp.mean(jnp.square(v - mu), -1, keepdims=True)
    return (v - mu) * lax.rsqrt(var + EPS) * g + b


def _route(logits_t, b_router):
    mx = jnp.max(logits_t, axis=0, keepdims=True)
    ex = jnp.exp(logits_t - mx)
    scores = ex / jnp.sum(ex, axis=0, keepdims=True)
    sel = scores + b_router
    rows = [sel[e:e + 1] for e in range(N_EXPERTS)]
    first, second, grp = [], [], []
    for gi in range(N_GROUPS):
        r = rows[gi * EXPERTS_PER_GROUP:(gi + 1) * EXPERTS_PER_GROUP]
        m1 = functools.reduce(jnp.maximum, r)
        taken = jnp.zeros_like(m1, dtype=jnp.bool_)
        f = []
        for x in r:
            hit = jnp.logical_and(x == m1, jnp.logical_not(taken))
            f.append(hit)
            taken = jnp.logical_or(taken, hit)
        rest = [jnp.where(fi, -jnp.inf, x) for fi, x in zip(f, r)]
        m2 = functools.reduce(jnp.maximum, rest)
        taken = jnp.zeros_like(m1, dtype=jnp.bool_)
        s = []
        for x in rest:
            hit = jnp.logical_and(x == m2, jnp.logical_not(taken))
            s.append(hit)
            taken = jnp.logical_or(taken, hit)
        first.append(f)
        second.append(s)
        grp.append(m1 + m2)
    gmax = functools.reduce(jnp.maximum, grp)
    taken = jnp.zeros_like(gmax, dtype=jnp.bool_)
    chosen = []
    group = jnp.zeros_like(gmax, dtype=jnp.int32)
    for gi in range(N_GROUPS):
        best = jnp.logical_and(grp[gi] == gmax, jnp.logical_not(taken))
        taken = jnp.logical_or(taken, best)
        group = jnp.where(best, gi, group)
        for j in range(EXPERTS_PER_GROUP):
            chosen.append(jnp.logical_and(best, jnp.logical_or(first[gi][j], second[gi][j])))
    picked = [jnp.where(ch, scores[e:e + 1], 0.0) for e, ch in enumerate(chosen)]
    denom = functools.reduce(lambda a, b: a + b, picked)
    return jnp.concatenate([p / denom for p in picked], axis=0), group


SLABS = D_MODEL // LANES


def _outproj_kernel(yr_ref, yh_ref, ym_ref, x_ref, mod_ref, w_ref, lg_ref, lb_ref,
                    wr_ref, br_ref, x1_ref, hx_ref, grp_ref):
    tm = x_ref.shape[0]
    m = mod_ref[...]
    y = (jnp.dot(yr_ref[...].astype(BF16), w_ref[0:D_GROUP], preferred_element_type=F32)
         + jnp.dot(yh_ref[...].astype(BF16), w_ref[D_GROUP:2 * D_GROUP], preferred_element_type=F32)
         + jnp.dot(ym_ref[...].astype(BF16), w_ref[2 * D_GROUP:], preferred_element_type=F32))
    x1 = _layer_norm(ALPHA * x_ref[...] + m[GATE1:GATE1 + 1] * y, lg_ref[...], lb_ref[...])
    x1_ref[...] = x1
    hm = x1 * (1.0 + m[SCALE2:SCALE2 + 1]) + m[SHIFT2:SHIFT2 + 1]
    for j in range(SLABS):
        hx_ref[:, j, :] = hm[:, j * LANES:(j + 1) * LANES]
    logits_t = lax.dot_general(wr_ref[...], hm, (((1,), (1,)), ((), ())), precision=HIGHEST,
                               preferred_element_type=F32)
    cmb_t, group = _route(logits_t, br_ref[...])
    cmb_rows = jnp.concatenate([cmb_t, jnp.zeros((LANES - N_EXPERTS, tm), F32)], axis=0)
    hx_ref[:, SLABS, :] = cmb_rows.T
    grp_ref[...] = group


def _outproj_call(yr, yh, ym, x, T, mod, w_out, l, ln_g, ln_b, w_router_t, b_router):
    n = x.shape[0]
    tm = WIDE_TILE if (mod.shape[0] == 1 or T % WIDE_TILE == 0) else ROW_TILE
    row = lambda w: pl.BlockSpec((tm, w), lambda i: (i, 0))
    full = lambda a: pl.BlockSpec(a.shape, lambda i: (0,) * a.ndim)
    return pl.pallas_call(
        _outproj_kernel,
        grid=(n // tm,),
        in_specs=[row(D_GROUP), row(D_GROUP), row(D_GROUP), row(D_MODEL), _mod_spec(mod, T, tm),
                  pl.BlockSpec((None, D_MIX, D_MODEL), lambda i: (l, 0, 0)),
                  full(ln_g), full(ln_b), full(w_router_t), full(b_router)],
        out_specs=[row(D_MODEL), pl.BlockSpec((tm, SLABS + 1, LANES), lambda i: (i, 0, 0)),
                   pl.BlockSpec((1, tm), lambda i: (0, i))],
        out_shape=[jax.ShapeDtypeStruct((n, D_MODEL), F32),
                   jax.ShapeDtypeStruct((n, SLABS + 1, LANES), F32),
                   jax.ShapeDtypeStruct((1, n), jnp.int32)],
        compiler_params=_cparams(1),
        name="outproj_ln_router",
    )(yr, yh, ym, x, mod, w_out, ln_g, ln_b, w_router_t, b_router)


MOE_TILE = 256


def _row_gather(idx_ref, first, n_rows, src_hbm, dst_ref, sem):
    def body(p, carry):
        for k in range(2):
            r = 2 * p + k
            pltpu.make_async_copy(src_hbm.at[idx_ref[first + r]], dst_ref.at[:, r, :], sem).start(priority=k)
        return carry
    lax.fori_loop(0, n_rows // 2, body, 0, unroll=4)


def _row_gather_wait(n_rows, src_hbm, dst_ref, sem):
    def body(r, carry):
        pltpu.make_async_copy(src_hbm.at[0], dst_ref.at[:, 0, :], sem).wait()
        return carry
    lax.fori_loop(0, n_rows, body, 0, unroll=8)


def _gathered_tile(idx_ref, src_hbm, buf_ref, sem_ref, tm):
    i = pl.program_id(0)
    slot = lax.rem(i, 2)

    @pl.when(i == 0)
    def _():
        _row_gather(idx_ref, 0, tm, src_hbm, buf_ref.at[0], sem_ref.at[0])

    @pl.when(i + 1 < pl.num_programs(0))
    def _():
        _row_gather(idx_ref, (i + 1) * tm, tm, src_hbm, buf_ref.at[1 - slot], sem_ref.at[1 - slot])

    _row_gather_wait(tm, src_hbm, buf_ref.at[slot], sem_ref.at[slot])
    return buf_ref.at[slot]


def _moe_kernel(tile_group_ref, src_ref, hx_hbm, wg_ref, wu_ref, wd_ref, o_ref, buf_ref, sem_ref):
    tm = o_ref.shape[0]
    rows_ref = _gathered_tile(src_ref, hx_hbm, buf_ref, sem_ref, tm)
    hm = jnp.concatenate([rows_ref[j] for j in range(SLABS)], axis=1).astype(BF16)
    cmb = rows_ref[SLABS]
    lane = lax.broadcasted_iota(jnp.int32, cmb.shape, 1)
    first_expert = tile_group_ref[pl.program_id(0)] * EXPERTS_PER_GROUP
    acc = None
    for k in range(EXPERTS_PER_GROUP):
        ce = jnp.sum(jnp.where(lane == first_expert + k, cmb, 0.0), axis=1, keepdims=True)
        hg = jnp.dot(hm, wg_ref[k], preferred_element_type=F32)
        hu = jnp.dot(hm, wu_ref[k], preferred_element_type=F32)
        part = jnp.dot((_silu(hg) * hu * ce).astype(BF16), wd_ref[k], preferred_element_type=F32)
        acc = part if acc is None else acc + part
    for j in range(SLABS):
        o_ref[:, j, :] = acc[:, j * LANES:(j + 1) * LANES]


def _moe_call(hx, tile_group, src, w_gate, w_up, w_down, l):
    n_pad = src.shape[0]
    tm = MOE_TILE
    grp_w = lambda shape: pl.BlockSpec((None,) + shape, lambda i, tg, sr: (l, tg[i], 0, 0))
    return pl.pallas_call(
        _moe_kernel,
        grid_spec=pltpu.PrefetchScalarGridSpec(
            num_scalar_prefetch=2,
            grid=(n_pad // tm,),
            in_specs=[pl.BlockSpec(memory_space=pl.ANY),
                      grp_w((EXPERTS_PER_GROUP, D_MODEL, D_FF)), grp_w((EXPERTS_PER_GROUP, D_MODEL, D_FF)),
                      grp_w((EXPERTS_PER_GROUP, D_FF, D_MODEL))],
            out_specs=pl.BlockSpec((tm, SLABS, LANES), lambda i, tg, sr: (i, 0, 0)),
            scratch_shapes=[pltpu.VMEM((2, SLABS + 1, tm, LANES), F32), pltpu.SemaphoreType.DMA((2,))]),
        out_shape=jax.ShapeDtypeStruct((n_pad, SLABS, LANES), F32),
        compiler_params=_cparams(1),
        name="moe_sorted",
    )(tile_group, src, hx, w_gate, w_up, w_down)


def _ln2_kernel(pos_ref, f_hbm, x1_ref, mod_ref, lg_ref, lb_ref, o_ref, buf_ref, sem_ref):
    tm = o_ref.shape[0]
    rows_ref = _gathered_tile(pos_ref, f_hbm, buf_ref, sem_ref, tm)
    f = jnp.concatenate([rows_ref[j] for j in range(SLABS)], axis=1)
    v = ALPHA * x1_ref[...] + mod_ref[GATE2:GATE2 + 1, :] * f
    o_ref[...] = _layer_norm(v, lg_ref[...], lb_ref[...])


def _ln2_call(f_sorted, pos, x1, T, mod, ln_g, ln_b):
    n = x1.shape[0]
    tm = ROW_TILE
    per_seq = mod.shape[0] > 1
    return pl.pallas_call(
        _ln2_kernel,
        grid_spec=pltpu.PrefetchScalarGridSpec(
            num_scalar_prefetch=1,
            grid=(n // tm,),
            in_specs=[pl.BlockSpec(memory_space=pl.ANY),
                      pl.BlockSpec((tm, D_MODEL), lambda i, ps: (i, 0)),
                      pl.BlockSpec((None, 6, D_MODEL), lambda i, ps: ((i * tm) // T if per_seq else 0, 0, 0)),
                      pl.BlockSpec((1, D_MODEL), lambda i, ps: (0, 0)),
                      pl.BlockSpec((1, D_MODEL), lambda i, ps: (0, 0))],
            out_specs=pl.BlockSpec((tm, D_MODEL), lambda i, ps: (i, 0)),
            scratch_shapes=[pltpu.VMEM((2, SLABS, tm, LANES), F32), pltpu.SemaphoreType.DMA((2,))]),
        out_shape=jax.ShapeDtypeStruct((n, D_MODEL), F32),
        compiler_params=_cparams(1),
        name="moe_combine_ln",
    )(pos, f_sorted, x1, mod, ln_g, ln_b)


def _dispatch_plan(group, tm):
    n = group.shape[0]
    n_pad = n + N_GROUPS * tm
    onehot = (group[:, None] == jnp.arange(N_GROUPS, dtype=jnp.int32)[None, :]).astype(jnp.int32)
    csum = jnp.cumsum(onehot, axis=0)
    rank = jnp.take_along_axis(csum, group[:, None], axis=1)[:, 0] - 1
    padded = ((csum[-1] + tm - 1) // tm) * tm
    ends = jnp.cumsum(padded)
    pos = (ends - padded)[group] + rank
    src = jnp.zeros((n_pad,), jnp.int32).at[pos].set(jnp.arange(n, dtype=jnp.int32))
    tile_start = jnp.arange(n_pad // tm, dtype=jnp.int32) * tm
    tile_group = jnp.sum((tile_start[:, None] >= ends[None, :]).astype(jnp.int32), axis=1)
    tile_group = jnp.minimum(tile_group, N_GROUPS - 1)
    return pos, src, tile_group


def _grid_pos_embed(n_tokens):
    rows = n_tokens // GRID_W
    r = jnp.repeat(jnp.arange(rows, dtype=F32), GRID_W)
    c = jnp.tile(jnp.arange(GRID_W, dtype=F32), rows)
    q = D_MODEL // 4
    freq = jnp.exp(-jnp.log(10000.0) * jnp.arange(q, dtype=F32) / q)
    ar = r[:, None] * freq
    ac = c[:, None] * freq
    return jnp.concatenate([jnp.sin(ar), jnp.cos(ar), jnp.sin(ac), jnp.cos(ac)], axis=-1)


def _lru_gate_params(wa, ba, wx, bx, lam):
    ncb = D_GROUP // LANES

    def dense(w):
        z = jnp.zeros((ncb, LANES, LANES), F32)
        z = z.at[:, :LRU_BLOCK, :LRU_BLOCK].set(w[0::2])
        return z.at[:, LRU_BLOCK:, LRU_BLOCK:].set(w[1::2])

    wg = jnp.concatenate([dense(wa[0]), dense(wx[0]), dense(wa[1]), dense(wx[1])], axis=-1).astype(BF16)
    per_blk = lambda v: v.reshape(ncb, 1, LANES)
    bg = jnp.concatenate([per_blk(ba[0]), per_blk(bx[0]), per_blk(ba[1]), per_blk(bx[1])], axis=-1)
    lm = jnp.concatenate([per_blk(lam[0]), per_blk(lam[1])], axis=-1)
    return wg, bg, lm


def _mixer_states(B, l, states):
    if states is None:
        return (jnp.zeros((2, B, D_GROUP), F32),
                jnp.zeros((B, 2, N_HEADS, D_HEAD, D_HEAD), F32),
                jnp.zeros((B, 2, N_HEADS, D_HEAD, D_HEAD), F32),
                jnp.zeros((B, 2, N_HEADS, D_HEAD), F32),
                jnp.zeros((B, 2, N_HEADS, D_HEAD), F32))
    h, s, c, n, m = states
    return (jnp.swapaxes(h[:, l], 0, 1), s[:, l], c[:, l], n[:, l],
            jnp.broadcast_to(m[:, l][..., None], (B, 2, N_HEADS, D_HEAD)))


def kernel(x_prompt, x_sample, state_lru_h, state_hgrn_S, state_mlstm_C, state_mlstm_n, state_mlstm_m,
           c, c_ctx, w_ada, b_ada, w_in, b_in, conv_w, conv_b, lru_wa, lru_ba, lru_wx, lru_bx, lru_lam,
           hg_lb, hg_norm_g, ml_norm_g, w_out, ln1_g, ln1_b, ln2_g, ln2_b,
           w_router, b_router, w_gate, w_up, w_down):
    Bp, Tp, _ = x_prompt.shape
    Bs, Ts, _ = x_sample.shape
    assert Bs % SUBLANES == 0 and Bp % SUBLANES == 0 and Bs + 1 <= 16
    assert Tp % ROW_TILE == 0 and Ts % ROW_TILE == 0

    lb_soft = jax.nn.softmax(hg_lb.astype(F32), axis=0)
    hg_lower = jnp.cumsum(lb_soft, axis=0) - lb_soft[0:1]

    c16 = jnp.concatenate([c, c_ctx[None], jnp.zeros((16 - Bs - 1, D_MODEL), F32)], axis=0)
    mod = _ada_call(c16, w_ada, b_ada)

    w_in_p = jnp.pad(w_in, ((0, 0), (0, 0), (0, D_IN_PAD - D_IN))).astype(BF16)
    b_in_p = jnp.pad(b_in, ((0, 0), (0, D_IN_PAD - D_IN))).reshape(DEPTH, 1, D_IN_PAD)
    w_out_b = w_out.astype(BF16)
    w_gate_b, w_up_b, w_down_b = w_gate.astype(BF16), w_up.astype(BF16), w_down.astype(BF16)
    w_router_t = w_router.T
    b_router_c = b_router.reshape(N_EXPERTS, 1)

    xs = x_sample + _grid_pos_embed(Ts).astype(x_sample.dtype)
    streams = [
        dict(x=x_prompt.reshape(Bp * Tp, D_MODEL), T=Tp, B=Bp, states=None),
        dict(x=xs.reshape(Bs * Ts, D_MODEL), T=Ts, B=Bs,
             states=(state_lru_h, state_hgrn_S, state_mlstm_C, state_mlstm_n, state_mlstm_m)),
    ]
    finals = []
    for l in range(DEPTH):
        wg, bg, lm = _lru_gate_params(lru_wa[l], lru_ba[l], lru_wx[l], lru_bx[l], lru_lam[l])
        row = lambda v: v[l].reshape(1, -1)
        for si, st in enumerate(streams):
            T, B = st["T"], st["B"]
            mod_s = (mod[l, :Bs] if si == 1 else mod[l, Bs:Bs + 1]).reshape(-1, 6, D_MODEL)
            h0, s0, c0, n0, m0 = _mixer_states(B, l, st["states"])
            z = _inproj_call(st["x"], T, mod_s, w_in_p, b_in_p, l)
            y_r, h_T = _rglru_call(z.reshape(B, T, D_IN_PAD), h0, conv_w[l], row(conv_b), wg, bg, lm)
            y_h, s_T = _hgrn_call(z, B, s0, hg_lower[l], row(hg_norm_g))
            y_m, c_T, n_T, m_T = _mlstm_call(z, B, c0, n0, m0, row(ml_norm_g))
            x1, hx, group = _outproj_call(y_r.reshape(B * T, D_GROUP), y_h, y_m, st["x"], T, mod_s, w_out_b, l,
                                          row(ln1_g), row(ln1_b), w_router_t, b_router_c)
            pos, src, tile_group = _dispatch_plan(group[0], MOE_TILE)
            f_sorted = _moe_call(hx, tile_group, src, w_gate_b, w_up_b, w_down_b, l)
            st["x"] = _ln2_call(f_sorted, pos, x1, T, mod_s, row(ln2_g), row(ln2_b))
            if si == 0:
                finals.append((jnp.swapaxes(h_T, 0, 1), s_T, c_T, n_T, m_T[..., 0]))
    outs = [st["x"].reshape(st["B"], st["T"], D_MODEL) for st in streams]
    stack = lambda i: jnp.stack([f[i] for f in finals], axis=1)
    return (outs[0], outs[1], stack(0), stack(1), stack(2), stack(3), stack(4))
```

```python
import functools

import jax
import jax.numpy as jnp
from jax import lax
from jax.experimental import pallas as pl
from jax.experimental.pallas import tpu as pltpu

F32 = jnp.float32
BF16 = jnp.bfloat16
HIGHEST = lax.Precision.HIGHEST

D_MODEL = 1024
DEPTH = 2
GRID_W = 64
D_GROUP = 512
D_MIX = 3 * D_GROUP
LRU_BLOCKS = 8
LRU_BLOCK = D_GROUP // LRU_BLOCKS
LRU_C = 8.0
N_HEADS = 4
D_HEAD = D_GROUP // N_HEADS
N_EXPERTS = 16
N_GROUPS = 4
EXPERTS_PER_GROUP = N_EXPERTS // N_GROUPS
D_FF = 512
ALPHA = (2.0 * DEPTH) ** 0.25
EPS = 1e-5
NEG = -1e30
TINY = 1e-30
N_GATE = 4 * N_HEADS
D_IN = 11 * D_GROUP + N_GATE
D_IN_PAD = 45 * 128
GATE_BLK = (11 * D_GROUP) // 128

SUBLANES = 8
LANES = 128
VMEM_LIMIT = 56 * 1024 * 1024

LRU_TC = 32
LRU_PAD = SUBLANES
HG_L = 64
HG_SUB = 16
ML_L = 128
ML_SUB = 32


def _cparams(n_axes):
    return pltpu.CompilerParams(dimension_semantics=("arbitrary",) * n_axes,
                                vmem_limit_bytes=VMEM_LIMIT)


def _dot(a, b):
    return jnp.dot(a.astype(BF16), b.astype(BF16), preferred_element_type=F32)


def _dot_nt(a, b):
    return lax.dot_general(a.astype(BF16), b.astype(BF16), (((1,), (1,)), ((), ())),
                           preferred_element_type=F32)


def _dot_tn(a, b):
    return lax.dot_general(a.astype(BF16), b.astype(BF16), (((0,), (0,)), ((), ())),
                           preferred_element_type=F32)


def _dot_f32(a, b):
    return jnp.dot(a, b, precision=HIGHEST, preferred_element_type=F32)


def _dot_split3(mask, x, mask_on_left):
    m = jnp.where(mask, 1.0, 0.0).astype(BF16)
    hi = x.astype(BF16)
    r1 = x - hi.astype(F32)
    mid = r1.astype(BF16)
    lo = (r1 - mid.astype(F32)).astype(BF16)
    mm = (lambda p: jnp.dot(m, p, preferred_element_type=F32)) if mask_on_left else (
        lambda p: jnp.dot(p, m, preferred_element_type=F32))
    return mm(hi) + mm(mid) + mm(lo)


def _softplus(x):
    return jnp.maximum(x, 0.0) + jnp.log1p(jnp.exp(-jnp.abs(x)))


def _silu(x):
    return x * jax.nn.sigmoid(x)


def _tri_masks(n):
    r = lax.broadcasted_iota(jnp.int32, (n, n), 0)
    c = lax.broadcasted_iota(jnp.int32, (n, n), 1)
    return r >= c, r <= c


def _ada_kernel(c_ref, w_ref, b_ref, o_ref):
    o_ref[0] = _dot_f32(_silu(c_ref[...]), w_ref[0]) + b_ref[0]


def _ada_call(c16, w_ada, b_ada):
    tn = 1536
    return pl.pallas_call(
        _ada_kernel,
        grid=(DEPTH, 6 * D_MODEL // tn),
        in_specs=[pl.BlockSpec((16, D_MODEL), lambda l, j: (0, 0)),
                  pl.BlockSpec((1, D_MODEL, tn), lambda l, j: (l, 0, j)),
                  pl.BlockSpec((1, 1, tn), lambda l, j: (l, 0, j))],
        out_specs=pl.BlockSpec((1, 16, tn), lambda l, j: (l, 0, j)),
        out_shape=jax.ShapeDtypeStruct((DEPTH, 16, 6 * D_MODEL), F32),
        compiler_params=_cparams(2),
        name="adaln",
    )(c16, w_ada, b_ada.reshape(DEPTH, 1, 6 * D_MODEL))


INPROJ_CW = 640
ROW_TILE = 256
WIDE_TILE = 512

SHIFT1, SCALE1, GATE1, SHIFT2, SCALE2, GATE2 = range(6)


def _mod_spec(mod, T, tm, n_grid_axes=1):
    per_seq = mod.shape[0] > 1
    if n_grid_axes == 1:
        return pl.BlockSpec((None, 6, D_MODEL), lambda i: ((i * tm) // T if per_seq else 0, 0, 0))
    return pl.BlockSpec((None, 6, D_MODEL), lambda i, e: ((i * tm) // T if per_seq else 0, 0, 0))


def _inproj_kernel(x_ref, mod_ref, w_ref, b_ref, z_ref):
    m = mod_ref[...]
    hb = (x_ref[...] * (1.0 + m[SCALE1:SCALE1 + 1]) + m[SHIFT1:SHIFT1 + 1]).astype(BF16)
    for j in range(D_IN_PAD // INPROJ_CW):
        cs = slice(j * INPROJ_CW, (j + 1) * INPROJ_CW)
        z_ref[:, cs] = jnp.dot(hb, w_ref[:, cs], preferred_element_type=F32) + b_ref[:, cs]


def _inproj_call(x, T, mod, w, b, l):
    n = x.shape[0]
    tm = WIDE_TILE if (mod.shape[0] == 1 or T % WIDE_TILE == 0) else ROW_TILE
    return pl.pallas_call(
        _inproj_kernel,
        grid=(n // tm,),
        in_specs=[pl.BlockSpec((tm, D_MODEL), lambda i: (i, 0)),
                  _mod_spec(mod, T, tm),
                  pl.BlockSpec((None, D_MODEL, D_IN_PAD), lambda i: (l, 0, 0), pipeline_mode=pl.Buffered(1)),
                  pl.BlockSpec((None, 1, D_IN_PAD), lambda i: (l, 0, 0))],
        out_specs=pl.BlockSpec((tm, D_IN_PAD), lambda i: (i, 0)),
        out_shape=jax.ShapeDtypeStruct((n, D_IN_PAD), F32),
        compiler_params=_cparams(1),
        name="inproj",
    )(x, mod, w, b)


def _rglru_kernel(xr_ref, gr_ref, cw_ref, cb_ref, wg_ref, bg_ref, lam_ref, h0_ref,
                  y_ref, hT_ref, xs_ref, os_ref, ob_ref):
    T = xr_ref.shape[1]
    P = T + LRU_PAD
    TC = LRU_TC
    nch = T // TC
    CP = 256
    zeros = jnp.zeros((LRU_PAD, LANES), F32)
    for s in range(SUBLANES + 1):
        xs_ref[s * P:s * P + LRU_PAD] = zeros
    for s in range(SUBLANES):
        def copy_in(c, carry, s=s):
            r0 = pl.multiple_of(c * CP, CP)
            xs_ref[pl.ds(s * P + LRU_PAD + r0, CP)] = xr_ref[s, pl.ds(r0, CP)]
            return carry
        lax.fori_loop(0, T // CP, copy_in, 0)

    cw = cw_ref[...]
    cb = cb_ref[...]
    sp = _softplus(-lam_ref[0])

    def gates(t0, d):
        win = [xs_ref[pl.ds(t0 + k + LRU_PAD - 2, SUBLANES, stride=P)] for k in range(TC + 3)]
        xc = jnp.stack([cw[0:1] * win[i] + cw[1:2] * win[i + 1] + cw[2:3] * win[i + 2] + cw[3:4] * win[i + 3] + cb
                        for i in range(TC)])
        x2 = xc.reshape(TC * SUBLANES, LANES)
        g = (jnp.dot(x2.astype(BF16), wg_ref[0, :, d * 256:(d + 1) * 256], preferred_element_type=F32)
             + bg_ref[0, :, d * 256:(d + 1) * 256])
        r = jax.nn.sigmoid(g[:, :LANES])
        i = jax.nn.sigmoid(g[:, LANES:])
        log_a = -LRU_C * r * sp[:, d * LANES:(d + 1) * LANES]
        a = jnp.exp(log_a)
        u = jnp.sqrt(jnp.maximum(1.0 - jnp.exp(2.0 * log_a), 0.0)) * (i * x2)
        return a.reshape(TC, SUBLANES, LANES), u.reshape(TC, SUBLANES, LANES)

    def step(c, carry):
        hf, hb = carry
        tf = pl.multiple_of(c * TC, TC)
        tb = pl.multiple_of((nch - 1 - c) * TC, TC)
        af, uf = gates(tf, 0)
        ab, ub = gates(tb, 1)
        for i in range(TC):
            hf = af[i] * hf + uf[i]
            os_ref[pl.ds(tf + i, SUBLANES, stride=P)] = hf
            k = TC - 1 - i
            hb = ab[k] * hb + ub[k]
            ob_ref[pl.ds(tb + k, SUBLANES, stride=P)] = hb
        return hf, hb

    hT_ref[0], hT_ref[1] = lax.fori_loop(0, nch, step, (h0_ref[0], h0_ref[1]))

    for s in range(SUBLANES):
        def copy_out(c, carry, s=s):
            r0 = pl.multiple_of(c * CP, CP)
            o = os_ref[pl.ds(s * P + r0, CP)] + ob_ref[pl.ds(s * P + r0, CP)]
            y_ref[s, pl.ds(r0, CP)] = jax.nn.gelu(gr_ref[s, pl.ds(r0, CP)]) * o
            return carry
        lax.fori_loop(0, T // CP, copy_out, 0)


def _rglru_call(z3, h0, cw, cb, wg, bg, lam):
    B, T, _ = z3.shape
    nb = B // SUBLANES
    ncb = D_GROUP // LANES
    blk = (SUBLANES, T, LANES)
    pitch = T + LRU_PAD
    return pl.pallas_call(
        _rglru_kernel,
        grid=(nb, ncb),
        in_specs=[pl.BlockSpec(blk, lambda b, c: (b, 0, c)),
                  pl.BlockSpec(blk, lambda b, c: (b, 0, ncb + c)),
                  pl.BlockSpec((4, LANES), lambda b, c: (0, c)),
                  pl.BlockSpec((1, LANES), lambda b, c: (0, c)),
                  pl.BlockSpec((1, LANES, 4 * LANES), lambda b, c: (c, 0, 0)),
                  pl.BlockSpec((1, 1, 4 * LANES), lambda b, c: (c, 0, 0)),
                  pl.BlockSpec((1, 1, 2 * LANES), lambda b, c: (c, 0, 0)),
                  pl.BlockSpec((2, SUBLANES, LANES), lambda b, c: (0, b, c))],
        out_specs=[pl.BlockSpec(blk, lambda b, c: (b, 0, c)),
                   pl.BlockSpec((2, SUBLANES, LANES), lambda b, c: (0, b, c))],
        out_shape=[jax.ShapeDtypeStruct((B, T, D_GROUP), F32),
                   jax.ShapeDtypeStruct((2, B, D_GROUP), F32)],
        scratch_shapes=[pltpu.VMEM((SUBLANES * pitch + LRU_PAD, LANES), F32),
                        pltpu.VMEM((SUBLANES * pitch, LANES), F32),
                        pltpu.VMEM((SUBLANES * pitch, LANES), F32)],
        compiler_params=_cparams(2),
        name="rglru",
    )(z3, z3, cw, cb, wg, bg, lam, h0)


def _hgrn_kernel(q_ref, ff_ref, fb_ref, v_ref, og_ref, lb_ref, ng_ref, s0_ref,
                 y_ref, sT_ref, st_ref, of_ref, ob_ref):
    T = q_ref.shape[0]
    L = HG_L
    SUB = HG_SUB
    nch = T // L
    nsub = L // SUB
    tril, triu = _tri_masks(L)
    heads = [slice(hd * D_HEAD, (hd + 1) * D_HEAD) for hd in range(N_HEADS)]

    for d in range(2):
        for hd in range(N_HEADS):
            st_ref[d, hd] = s0_ref[d, hd].T

    def step(c, carry):
        jobs = ((0, pl.multiple_of(c * L, L), of_ref), (1, pl.multiple_of((nch - 1 - c) * L, L), ob_ref))
        gated = []
        for d, t0, _ in jobs:
            q = _silu(q_ref[pl.ds(t0, L), :])
            v = v_ref[pl.ds(t0, L), :].astype(BF16)
            lb = lb_ref[d:d + 1, :]
            sig = jax.nn.sigmoid((ff_ref if d == 0 else fb_ref)[pl.ds(t0, L), :])
            log_f = jnp.log(jnp.maximum(lb + (1.0 - lb) * sig, TINY))
            kk = (1.0 - lb) * (1.0 - sig)
            mask = tril if d == 0 else triu
            b = _dot_split3(mask, log_f, mask_on_left=True)
            gated.append((q, v, kk, b, mask))
        factored = []
        for (d, _, _), (q, v, kk, b, mask) in zip(jobs, gated):
            b_tot = b[L - 1:L] if d == 0 else b[0:1]
            subs = []
            for i in range(nsub):
                rows = slice(i * SUB, (i + 1) * SUB)
                if d == 0:
                    cols = slice(0, (i + 1) * SUB)
                    edge = b[i * SUB - 1:i * SUB] if i > 0 else 0.0
                else:
                    cols = slice(i * SUB, L)
                    edge = b[(i + 1) * SUB:(i + 1) * SUB + 1] if i < nsub - 1 else 0.0
                subs.append((rows, cols, (q[rows] * jnp.exp(b[rows] - edge)).astype(BF16),
                             (kk[cols] * jnp.exp(edge - b[cols])).astype(BF16)))
            factored.append(((q * jnp.exp(b)).astype(BF16), (kk * jnp.exp(b_tot - b)).astype(BF16),
                             jnp.exp(b_tot), subs))
        scores = [[[_dot_nt(qs[:, hs], ks[:, hs]) for (_, _, qs, ks) in subs] for hs in heads]
                  for (_, _, _, subs) in factored]
        states = [[st_ref[d, hd] for hd in range(N_HEADS)] for d, _, _ in jobs]
        inter = [[_dot_nt(qdec[:, hs], states[j][hd]) for hd, hs in enumerate(heads)]
                 for j, (qdec, _, _, _) in enumerate(factored)]
        update = [[_dot_tn(gated[j][1][:, hs], kdec[:, hs]) for hs in heads]
                  for j, (_, kdec, _, _) in enumerate(factored)]
        for j, (d, t0, o_ref) in enumerate(jobs):
            v, mask = gated[j][1], gated[j][4]
            dec, subs = factored[j][2], factored[j][3]
            for hd, hs in enumerate(heads):
                intra = [_dot(jnp.where(mask[rows, cols], scores[j][hd][i], 0.0), v[cols, hs])
                         for i, (rows, cols, _, _) in enumerate(subs)]
                o_ref[pl.ds(t0, L), hs] = jnp.concatenate(intra, axis=0) + inter[j][hd]
                st_ref[d, hd] = states[j][hd] * dec[:, hs] + update[j][hd]
        return carry

    lax.fori_loop(0, nch, step, 0)

    def combine(c, carry):
        t0 = pl.multiple_of(c * L, L)
        for hs in heads:
            o = of_ref[pl.ds(t0, L), hs] + ob_ref[pl.ds(t0, L), hs]
            o = o * lax.rsqrt(jnp.mean(jnp.square(o), -1, keepdims=True) + EPS)
            y_ref[pl.ds(t0, L), hs] = o * ng_ref[:, hs] * _silu(og_ref[pl.ds(t0, L), hs])
        return carry

    lax.fori_loop(0, nch, combine, 0)

    for d in range(2):
        for hd in range(N_HEADS):
            sT_ref[d, hd] = st_ref[d, hd].T


def _seq_col_spec(T, j):
    return pl.BlockSpec((T, D_GROUP), lambda b, j=j: (b, j))


def _hgrn_call(z2, B, s0, lower, norm_g):
    T = z2.shape[0] // B
    col = functools.partial(_seq_col_spec, T)
    st_spec = pl.BlockSpec((None, 2, N_HEADS, D_HEAD, D_HEAD), lambda b: (b, 0, 0, 0, 0))
    return pl.pallas_call(
        _hgrn_kernel,
        grid=(B,),
        in_specs=[col(2), col(3), col(4), col(5), col(6),
                  pl.BlockSpec((2, D_GROUP), lambda b: (0, 0)),
                  pl.BlockSpec((1, D_GROUP), lambda b: (0, 0)),
                  st_spec],
        out_specs=[pl.BlockSpec((T, D_GROUP), lambda b: (b, 0)), st_spec],
        out_shape=[jax.ShapeDtypeStruct((B * T, D_GROUP), F32),
                   jax.ShapeDtypeStruct((B, 2, N_HEADS, D_HEAD, D_HEAD), F32)],
        scratch_shapes=[pltpu.VMEM((2, N_HEADS, D_HEAD, D_HEAD), F32),
                        pltpu.VMEM((T, D_GROUP), F32),
                        pltpu.VMEM((T, D_GROUP), F32)],
        compiler_params=_cparams(1),
        name="hgrn2",
    )(z2, z2, z2, z2, z2, lower, norm_g, s0)


def _mlstm_kernel(q_ref, k_ref, v_ref, og_ref, g_ref, ng_ref, c0_ref, n0_ref, m0_ref,
                  y_ref, cT_ref, nT_ref, mT_ref, cn_ref, of_ref, ob_ref):
    T = q_ref.shape[0]
    L = ML_L
    SUB = ML_SUB
    nch = T // L
    nsub = L // SUB
    tril, triu = _tri_masks(L)
    gcol = lax.broadcasted_iota(jnp.int32, (L, N_GATE), 1)
    heads = [slice(hd * D_HEAD, (hd + 1) * D_HEAD) for hd in range(N_HEADS)]
    ones = jnp.ones((L, D_HEAD), BF16)

    for d in range(2):
        for hd in range(N_HEADS):
            cn_ref[d, hd, :, 0:D_HEAD] = c0_ref[d, hd]
            cn_ref[d, hd, :, D_HEAD:] = jnp.broadcast_to(n0_ref[d, hd:hd + 1, :], (D_HEAD, D_HEAD)).T
    mT_ref[...] = m0_ref[...]

    def step(c, carry):
        jobs = ((0, pl.multiple_of(c * L, L), of_ref), (1, pl.multiple_of((nch - 1 - c) * L, L), ob_ref))
        loaded, q_state = [], []
        for d, t0, _ in jobs:
            g = g_ref[pl.ds(t0, L), 0:N_GATE]
            g = jnp.where(gcol % 8 >= N_HEADS, jax.nn.log_sigmoid(g), g)
            bc = _dot_split3(tril if d == 0 else triu, g, mask_on_left=True)
            qb = q_ref[pl.ds(t0, L), :].astype(BF16)
            kf = k_ref[pl.ds(t0, L), :] * (D_HEAD ** -0.5)
            vb = v_ref[pl.ds(t0, L), :].astype(BF16)
            v1 = [jnp.concatenate([vb[:, hs], ones], axis=1) for hs in heads]
            loaded.append((g, bc, qb, kf, v1))
            q_state.append([_dot(qb[:, hs], cn_ref[d, hd]) for hd, hs in enumerate(heads)])
        scaled = []
        for j, (d, _, _) in enumerate(jobs):
            g, bc, qb, kf, v1 = loaded[j]
            per_head = []
            for hd, hs in enumerate(heads):
                ic, fc = d * 8 + hd, d * 8 + N_HEADS + hd
                b_col = bc[:, fc:fc + 1]
                r_col = g[:, ic:ic + 1] - b_col
                b_tot = b_col[L - 1:L] if d == 0 else b_col[0:1]
                m_st = mT_ref[d, hd:hd + 1, 0:1]
                blk_max = [jnp.max(r_col[i * SUB:(i + 1) * SUB], axis=0, keepdims=True) for i in range(nsub)]
                r_rep = jnp.broadcast_to(r_col, (L, D_HEAD))
                subs = []
                for i in range(nsub):
                    rows = slice(i * SUB, (i + 1) * SUB)
                    cols = slice(0, (i + 1) * SUB) if d == 0 else slice(i * SUB, L)
                    seen = blk_max[:i + 1] if d == 0 else blk_max[i:]
                    mu = jnp.maximum(m_st, functools.reduce(jnp.maximum, seen))
                    kp = (kf[cols, hs] * jnp.exp(r_rep[cols] - mu)).astype(BF16)
                    floor = jnp.exp(-b_col[rows] - mu)
                    subs.append((rows, cols, mu, kp, floor))
                per_head.append((b_tot, m_st, subs))
            scaled.append(per_head)
        scores = [[[_dot_nt(loaded[j][2][rows, hs], kp) for (rows, _, _, kp, _) in scaled[j][hd][2]]
                   for hd, hs in enumerate(heads)] for j in range(2)]
        full = [nsub - 1, 0]
        update = [[_dot_tn(scaled[j][hd][2][full[j]][3], loaded[j][4][hd]) for hd in range(N_HEADS)]
                  for j in range(2)]
        for j, (d, t0, o_ref) in enumerate(jobs):
            mask = tril if d == 0 else triu
            for hd, hs in enumerate(heads):
                b_tot, m_st, subs = scaled[j][hd]
                outs = []
                for i, (rows, cols, mu, _, floor) in enumerate(subs):
                    w = jnp.where(mask[rows, cols], scores[j][hd][i], 0.0)
                    nd = _dot(w, loaded[j][4][hd][cols]) + jnp.exp(m_st - mu) * q_state[j][hd][rows]
                    outs.append(nd[:, 0:D_HEAD] / jnp.maximum(jnp.abs(nd[:, D_HEAD:]), floor))
                o_ref[pl.ds(t0, L), hs] = jnp.concatenate(outs, axis=0)
                mu_full = subs[full[j]][2]
                cn_ref[d, hd] = jnp.exp(m_st - mu_full) * cn_ref[d, hd] + update[j][hd]
                mT_ref[d, hd:hd + 1, :] = jnp.broadcast_to(b_tot + mu_full, (1, LANES))
        return carry

    lax.fori_loop(0, nch, step, 0)

    for d in range(2):
        for hd in range(N_HEADS):
            cT_ref[d, hd] = cn_ref[d, hd, :, 0:D_HEAD]
            nT_ref[d, hd:hd + 1, :] = cn_ref[d, hd, :, D_HEAD:].T[0:1, :]

    def combine(c, carry):
        t0 = pl.multiple_of(c * L, L)
        for hs in heads:
            o = of_ref[pl.ds(t0, L), hs] + ob_ref[pl.ds(t0, L), hs]
            mu = jnp.mean(o, -1, keepdims=True)
            var = jnp.mean(jnp.square(o - mu), -1, keepdims=True)
            o = (o - mu) * lax.rsqrt(var + EPS)
            y_ref[pl.ds(t0, L), hs] = jax.nn.sigmoid(og_ref[pl.ds(t0, L), hs]) * (o * ng_ref[:, hs])
        return carry

    lax.fori_loop(0, nch, combine, 0)


def _mlstm_call(z2, B, c0, n0, m0, norm_g):
    T = z2.shape[0] // B
    col = functools.partial(_seq_col_spec, T)
    c_spec = pl.BlockSpec((None, 2, N_HEADS, D_HEAD, D_HEAD), lambda b: (b, 0, 0, 0, 0))
    v_spec = pl.BlockSpec((None, 2, N_HEADS, D_HEAD), lambda b: (b, 0, 0, 0))
    return pl.pallas_call(
        _mlstm_kernel,
        grid=(B,),
        in_specs=[col(7), col(8), col(9), col(10),
                  pl.BlockSpec((T, LANES), lambda b: (b, GATE_BLK)),
                  pl.BlockSpec((1, D_GROUP), lambda b: (0, 0)),
                  c_spec, v_spec, v_spec],
        out_specs=[pl.BlockSpec((T, D_GROUP), lambda b: (b, 0)), c_spec, v_spec, v_spec],
        out_shape=[jax.ShapeDtypeStruct((B * T, D_GROUP), F32),
                   jax.ShapeDtypeStruct((B, 2, N_HEADS, D_HEAD, D_HEAD), F32),
                   jax.ShapeDtypeStruct((B, 2, N_HEADS, D_HEAD), F32),
                   jax.ShapeDtypeStruct((B, 2, N_HEADS, D_HEAD), F32)],
        scratch_shapes=[pltpu.VMEM((2, N_HEADS, D_HEAD, 2 * D_HEAD), F32),
                        pltpu.VMEM((T, D_GROUP), F32), pltpu.VMEM((T, D_GROUP), F32)],
        compiler_params=_cparams(1),
        name="mlstm",
    )(z2, z2, z2, z2, z2, norm_g, c0, n0, m0)


def _layer_norm(v, g, b):
    mu = jnp.mean(v, -1, keepdims=True)
    var = jnp.mean(jnp.square(v - mu), -1, keepdims=True)
    return (v - mu) * lax.rsqrt(var + EPS) * g + b


EXPERT_PAIRS = ((0, 1), (0, 2), (1, 2), (1, 3), (0, 3), (2, 3))
N_CLASSES = N_GROUPS * len(EXPERT_PAIRS)


def _route(logits_t, b_router):
    mx = jnp.max(logits_t, axis=0, keepdims=True)
    ex = jnp.exp(logits_t - mx)
    scores = ex / jnp.sum(ex, axis=0, keepdims=True)
    sel = scores + b_router
    rows = [sel[e:e + 1] for e in range(N_EXPERTS)]
    first, second, grp = [], [], []
    for gi in range(N_GROUPS):
        r = rows[gi * EXPERTS_PER_GROUP:(gi + 1) * EXPERTS_PER_GROUP]
        m1 = functools.reduce(jnp.maximum, r)
        taken = jnp.zeros_like(m1, dtype=jnp.bool_)
        f = []
        for x in r:
            hit = jnp.logical_and(x == m1, jnp.logical_not(taken))
            f.append(hit)
            taken = jnp.logical_or(taken, hit)
        rest = [jnp.where(fi, -jnp.inf, x) for fi, x in zip(f, r)]
        m2 = functools.reduce(jnp.maximum, rest)
        taken = jnp.zeros_like(m1, dtype=jnp.bool_)
        s = []
        for x in rest:
            hit = jnp.logical_and(x == m2, jnp.logical_not(taken))
            s.append(hit)
            taken = jnp.logical_or(taken, hit)
        first.append(f)
        second.append(s)
        grp.append(m1 + m2)
    gmax = functools.reduce(jnp.maximum, grp)
    taken = jnp.zeros_like(gmax, dtype=jnp.bool_)
    chosen = []
    cls = jnp.zeros_like(gmax, dtype=jnp.int32)
    for gi in range(N_GROUPS):
        best = jnp.logical_and(grp[gi] == gmax, jnp.logical_not(taken))
        taken = jnp.logical_or(taken, best)
        in_grp = [jnp.logical_and(best, jnp.logical_or(first[gi][j], second[gi][j]))
                  for j in range(EXPERTS_PER_GROUP)]
        chosen.extend(in_grp)
        for p, (a, b) in enumerate(EXPERT_PAIRS):
            cls = jnp.where(jnp.logical_and(in_grp[a], in_grp[b]), gi * len(EXPERT_PAIRS) + p, cls)
    picked = [jnp.where(ch, scores[e:e + 1], 0.0) for e, ch in enumerate(chosen)]
    denom = functools.reduce(lambda a, b: a + b, picked)
    return jnp.concatenate([p / denom for p in picked], axis=0), cls


SLABS = D_MODEL // LANES


def _outproj_kernel(yr_ref, yh_ref, ym_ref, x_ref, mod_ref, w_ref, lg_ref, lb_ref,
                    wr_ref, br_ref, x1_ref, hx_ref, cls_ref):
    tm = x_ref.shape[0]
    m = mod_ref[...]
    y = (jnp.dot(yr_ref[...].astype(BF16), w_ref[0:D_GROUP], preferred_element_type=F32)
         + jnp.dot(yh_ref[...].astype(BF16), w_ref[D_GROUP:2 * D_GROUP], preferred_element_type=F32)
         + jnp.dot(ym_ref[...].astype(BF16), w_ref[2 * D_GROUP:], preferred_element_type=F32))
    x1 = _layer_norm(ALPHA * x_ref[...] + m[GATE1:GATE1 + 1] * y, lg_ref[...], lb_ref[...])
    x1_ref[...] = x1
    hm = x1 * (1.0 + m[SCALE2:SCALE2 + 1]) + m[SHIFT2:SHIFT2 + 1]
    for j in range(SLABS):
        hx_ref[:, j, :] = hm[:, j * LANES:(j + 1) * LANES]
    logits_t = lax.dot_general(wr_ref[...], hm, (((1,), (1,)), ((), ())), precision=HIGHEST,
                               preferred_element_type=F32)
    cmb_t, cls = _route(logits_t, br_ref[...])
    cmb_rows = jnp.concatenate([cmb_t, jnp.zeros((LANES - N_EXPERTS, tm), F32)], axis=0)
    hx_ref[:, SLABS, :] = cmb_rows.T
    cls_ref[...] = cls


def _outproj_call(yr, yh, ym, x, T, mod, w_out, l, ln_g, ln_b, w_router_t, b_router):
    n = x.shape[0]
    tm = WIDE_TILE if (mod.shape[0] == 1 or T % WIDE_TILE == 0) else ROW_TILE
    row = lambda w: pl.BlockSpec((tm, w), lambda i: (i, 0))
    full = lambda a: pl.BlockSpec(a.shape, lambda i: (0,) * a.ndim)
    return pl.pallas_call(
        _outproj_kernel,
        grid=(n // tm,),
        in_specs=[row(D_GROUP), row(D_GROUP), row(D_GROUP), row(D_MODEL), _mod_spec(mod, T, tm),
                  pl.BlockSpec((None, D_MIX, D_MODEL), lambda i: (l, 0, 0)),
                  full(ln_g), full(ln_b), full(w_router_t), full(b_router)],
        out_specs=[row(D_MODEL), pl.BlockSpec((tm, SLABS + 1, LANES), lambda i: (i, 0, 0)),
                   pl.BlockSpec((1, tm), lambda i: (0, i))],
        out_shape=[jax.ShapeDtypeStruct((n, D_MODEL), F32),
                   jax.ShapeDtypeStruct((n, SLABS + 1, LANES), F32),
                   jax.ShapeDtypeStruct((1, n), jnp.int32)],
        compiler_params=_cparams(1),
        name="outproj_ln_router",
    )(yr, yh, ym, x, mod, w_out, ln_g, ln_b, w_router_t, b_router)


MOE_TILE = 256


def _row_gather(idx_ref, first, n_rows, src_hbm, dst_ref, sem):
    def body(p, carry):
        for k in range(2):
            r = 2 * p + k
            pltpu.make_async_copy(src_hbm.at[idx_ref[first + r]], dst_ref.at[:, r, :], sem).start(priority=k)
        return carry
    lax.fori_loop(0, n_rows // 2, body, 0, unroll=4)


def _row_gather_wait(n_rows, src_hbm, dst_ref, sem):
    def body(r, carry):
        pltpu.make_async_copy(src_hbm.at[0], dst_ref.at[:, 0, :], sem).wait()
        return carry
    lax.fori_loop(0, n_rows, body, 0, unroll=8)


def _gathered_tile(idx_ref, src_hbm, buf_ref, sem_ref, tm):
    i = pl.program_id(0)
    slot = lax.rem(i, 2)

    @pl.when(i == 0)
    def _():
        _row_gather(idx_ref, 0, tm, src_hbm, buf_ref.at[0], sem_ref.at[0])

    @pl.when(i + 1 < pl.num_programs(0))
    def _():
        _row_gather(idx_ref, (i + 1) * tm, tm, src_hbm, buf_ref.at[1 - slot], sem_ref.at[1 - slot])

    _row_gather_wait(tm, src_hbm, buf_ref.at[slot], sem_ref.at[slot])
    return buf_ref.at[slot]


def _moe_kernel(tile_group_ref, need_ref, src_ref, hx_hbm, wg_ref, wu_ref, wd_ref, o_ref, buf_ref, sem_ref,
                acc_ref):
    tm = o_ref.shape[0]
    i = pl.program_id(0)
    rows_ref = _gathered_tile(src_ref, hx_hbm, buf_ref, sem_ref, tm)
    hm = jnp.concatenate([rows_ref[j] for j in range(SLABS)], axis=1).astype(BF16)
    cmb = rows_ref[SLABS]
    lane = lax.broadcasted_iota(jnp.int32, cmb.shape, 1)
    first_expert = tile_group_ref[i] * EXPERTS_PER_GROUP
    acc_ref[...] = jnp.zeros_like(acc_ref)
    for k in range(EXPERTS_PER_GROUP):
        @pl.when(need_ref[i * EXPERTS_PER_GROUP + k] != 0)
        def _(k=k):
            ce = jnp.sum(jnp.where(lane == first_expert + k, cmb, 0.0), axis=1, keepdims=True)
            hg = jnp.dot(hm, wg_ref[k], preferred_element_type=F32)
            hu = jnp.dot(hm, wu_ref[k], preferred_element_type=F32)
            acc_ref[...] += jnp.dot((_silu(hg) * hu * ce).astype(BF16), wd_ref[k], preferred_element_type=F32)
    for j in range(SLABS):
        o_ref[:, j, :] = acc_ref[:, j * LANES:(j + 1) * LANES]


def _moe_call(hx, tile_group, need, src, w_gate, w_up, w_down, l):
    n_pad = src.shape[0]
    tm = MOE_TILE
    grp_w = lambda shape: pl.BlockSpec((None,) + shape, lambda i, tg, nd, sr: (l, tg[i], 0, 0))
    return pl.pallas_call(
        _moe_kernel,
        grid_spec=pltpu.PrefetchScalarGridSpec(
            num_scalar_prefetch=3,
            grid=(n_pad // tm,),
            in_specs=[pl.BlockSpec(memory_space=pl.ANY),
                      grp_w((EXPERTS_PER_GROUP, D_MODEL, D_FF)), grp_w((EXPERTS_PER_GROUP, D_MODEL, D_FF)),
                      grp_w((EXPERTS_PER_GROUP, D_FF, D_MODEL))],
            out_specs=pl.BlockSpec((tm, SLABS, LANES), lambda i, tg, nd, sr: (i, 0, 0)),
            scratch_shapes=[pltpu.VMEM((2, SLABS + 1, tm, LANES), F32), pltpu.SemaphoreType.DMA((2,)),
                            pltpu.VMEM((tm, D_MODEL), F32)]),
        out_shape=jax.ShapeDtypeStruct((n_pad, SLABS, LANES), F32),
        compiler_params=_cparams(1),
        name="moe_sorted",
    )(tile_group, need, src, hx, w_gate, w_up, w_down)


def _ln2_kernel(pos_ref, f_hbm, x1_ref, mod_ref, lg_ref, lb_ref, o_ref, buf_ref, sem_ref):
    tm = o_ref.shape[0]
    rows_ref = _gathered_tile(pos_ref, f_hbm, buf_ref, sem_ref, tm)
    f = jnp.concatenate([rows_ref[j] for j in range(SLABS)], axis=1)
    v = ALPHA * x1_ref[...] + mod_ref[GATE2:GATE2 + 1, :] * f
    o_ref[...] = _layer_norm(v, lg_ref[...], lb_ref[...])


def _ln2_call(f_sorted, pos, x1, T, mod, ln_g, ln_b):
    n = x1.shape[0]
    tm = ROW_TILE
    per_seq = mod.shape[0] > 1
    return pl.pallas_call(
        _ln2_kernel,
        grid_spec=pltpu.PrefetchScalarGridSpec(
            num_scalar_prefetch=1,
            grid=(n // tm,),
            in_specs=[pl.BlockSpec(memory_space=pl.ANY),
                      pl.BlockSpec((tm, D_MODEL), lambda i, ps: (i, 0)),
                      pl.BlockSpec((None, 6, D_MODEL), lambda i, ps: ((i * tm) // T if per_seq else 0, 0, 0)),
                      pl.BlockSpec((1, D_MODEL), lambda i, ps: (0, 0)),
                      pl.BlockSpec((1, D_MODEL), lambda i, ps: (0, 0))],
            out_specs=pl.BlockSpec((tm, D_MODEL), lambda i, ps: (i, 0)),
            scratch_shapes=[pltpu.VMEM((2, SLABS, tm, LANES), F32), pltpu.SemaphoreType.DMA((2,))]),
        out_shape=jax.ShapeDtypeStruct((n, D_MODEL), F32),
        compiler_params=_cparams(1),
        name="moe_combine_ln",
    )(pos, f_sorted, x1, mod, ln_g, ln_b)


def _dispatch_plan(cls, tm):
    n = cls.shape[0]
    n_pairs = len(EXPERT_PAIRS)
    n_pad = n + N_GROUPS * tm
    i32 = jnp.int32
    onehot = (cls[:, None] == jnp.arange(N_CLASSES, dtype=i32)[None, :]).astype(i32)
    csum = jnp.cumsum(onehot, axis=0)
    rank = jnp.take_along_axis(csum, cls[:, None], axis=1)[:, 0] - 1
    count = csum[-1].reshape(N_GROUPS, n_pairs)
    padded = ((jnp.sum(count, axis=1) + tm - 1) // tm) * tm
    ends = jnp.cumsum(padded)
    class_start = ((ends - padded)[:, None] + jnp.cumsum(count, axis=1) - count).reshape(N_CLASSES)
    pos = class_start[cls] + rank
    src = jnp.zeros((n_pad,), i32).at[pos].set(jnp.arange(n, dtype=i32))
    tile_start = jnp.arange(n_pad // tm, dtype=i32) * tm
    tile_group = jnp.minimum(jnp.sum((tile_start[:, None] >= ends[None, :]).astype(i32), axis=1), N_GROUPS - 1)
    class_end = class_start + count.reshape(N_CLASSES)
    overlap = ((class_start[None, :] < tile_start[:, None] + tm) & (class_end[None, :] > tile_start[:, None])
               & (class_end > class_start)[None, :]).astype(i32)
    pair_has = jnp.array([[int(k in p) for k in range(EXPERTS_PER_GROUP)] for p in EXPERT_PAIRS], i32)
    need = jnp.minimum(overlap @ jnp.tile(pair_has, (N_GROUPS, 1)), 1).reshape(-1)
    return pos, src, tile_group, need


def _grid_pos_embed(n_tokens):
    rows = n_tokens // GRID_W
    r = jnp.repeat(jnp.arange(rows, dtype=F32), GRID_W)
    c = jnp.tile(jnp.arange(GRID_W, dtype=F32), rows)
    q = D_MODEL // 4
    freq = jnp.exp(-jnp.log(10000.0) * jnp.arange(q, dtype=F32) / q)
    ar = r[:, None] * freq
    ac = c[:, None] * freq
    return jnp.concatenate([jnp.sin(ar), jnp.cos(ar), jnp.sin(ac), jnp.cos(ac)], axis=-1)


def _lru_gate_params(wa, ba, wx, bx, lam):
    ncb = D_GROUP // LANES

    def dense(w):
        z = jnp.zeros((ncb, LANES, LANES), F32)
        z = z.at[:, :LRU_BLOCK, :LRU_BLOCK].set(w[0::2])
        return z.at[:, LRU_BLOCK:, LRU_BLOCK:].set(w[1::2])

    wg = jnp.concatenate([dense(wa[0]), dense(wx[0]), dense(wa[1]), dense(wx[1])], axis=-1).astype(BF16)
    per_blk = lambda v: v.reshape(ncb, 1, LANES)
    bg = jnp.concatenate([per_blk(ba[0]), per_blk(bx[0]), per_blk(ba[1]), per_blk(bx[1])], axis=-1)
    lm = jnp.concatenate([per_blk(lam[0]), per_blk(lam[1])], axis=-1)
    return wg, bg, lm


def _mixer_states(B, l, states):
    if states is None:
        return (jnp.zeros((2, B, D_GROUP), F32),
                jnp.zeros((B, 2, N_HEADS, D_HEAD, D_HEAD), F32),
                jnp.zeros((B, 2, N_HEADS, D_HEAD, D_HEAD), F32),
                jnp.zeros((B, 2, N_HEADS, D_HEAD), F32),
                jnp.zeros((B, 2, N_HEADS, D_HEAD), F32))
    h, s, c, n, m = states
    return (jnp.swapaxes(h[:, l], 0, 1), s[:, l], c[:, l], n[:, l],
            jnp.broadcast_to(m[:, l][..., None], (B, 2, N_HEADS, D_HEAD)))


def kernel(x_prompt, x_sample, state_lru_h, state_hgrn_S, state_mlstm_C, state_mlstm_n, state_mlstm_m,
           c, c_ctx, w_ada, b_ada, w_in, b_in, conv_w, conv_b, lru_wa, lru_ba, lru_wx, lru_bx, lru_lam,
           hg_lb, hg_norm_g, ml_norm_g, w_out, ln1_g, ln1_b, ln2_g, ln2_b,
           w_router, b_router, w_gate, w_up, w_down):
    Bp, Tp, _ = x_prompt.shape
    Bs, Ts, _ = x_sample.shape
    assert Bs % SUBLANES == 0 and Bp % SUBLANES == 0 and Bs + 1 <= 16
    assert Tp % ROW_TILE == 0 and Ts % ROW_TILE == 0

    lb_soft = jax.nn.softmax(hg_lb.astype(F32), axis=0)
    hg_lower = jnp.cumsum(lb_soft, axis=0) - lb_soft[0:1]

    c16 = jnp.concatenate([c, c_ctx[None], jnp.zeros((16 - Bs - 1, D_MODEL), F32)], axis=0)
    mod = _ada_call(c16, w_ada, b_ada)

    w_in_p = jnp.pad(w_in, ((0, 0), (0, 0), (0, D_IN_PAD - D_IN))).astype(BF16)
    b_in_p = jnp.pad(b_in, ((0, 0), (0, D_IN_PAD - D_IN))).reshape(DEPTH, 1, D_IN_PAD)
    w_out_b = w_out.astype(BF16)
    w_gate_b, w_up_b, w_down_b = w_gate.astype(BF16), w_up.astype(BF16), w_down.astype(BF16)
    w_router_t = w_router.T
    b_router_c = b_router.reshape(N_EXPERTS, 1)

    xs = x_sample + _grid_pos_embed(Ts).astype(x_sample.dtype)
    streams = [
        dict(x=x_prompt.reshape(Bp * Tp, D_MODEL), T=Tp, B=Bp, states=None),
        dict(x=xs.reshape(Bs * Ts, D_MODEL), T=Ts, B=Bs,
             states=(state_lru_h, state_hgrn_S, state_mlstm_C, state_mlstm_n, state_mlstm_m)),
    ]
    finals = []
    for l in range(DEPTH):
        wg, bg, lm = _lru_gate_params(lru_wa[l], lru_ba[l], lru_wx[l], lru_bx[l], lru_lam[l])
        row = lambda v: v[l].reshape(1, -1)
        for si, st in enumerate(streams):
            T, B = st["T"], st["B"]
            mod_s = (mod[l, :Bs] if si == 1 else mod[l, Bs:Bs + 1]).reshape(-1, 6, D_MODEL)
            h0, s0, c0, n0, m0 = _mixer_states(B, l, st["states"])
            z = _inproj_call(st["x"], T, mod_s, w_in_p, b_in_p, l)
            y_r, h_T = _rglru_call(z.reshape(B, T, D_IN_PAD), h0, conv_w[l], row(conv_b), wg, bg, lm)
            y_h, s_T = _hgrn_call(z, B, s0, hg_lower[l], row(hg_norm_g))
            y_m, c_T, n_T, m_T = _mlstm_call(z, B, c0, n0, m0, row(ml_norm_g))
            x1, hx, cls = _outproj_call(y_r.reshape(B * T, D_GROUP), y_h, y_m, st["x"], T, mod_s, w_out_b, l,
                                        row(ln1_g), row(ln1_b), w_router_t, b_router_c)
            pos, src, tile_group, need = _dispatch_plan(cls[0], MOE_TILE)
            f_sorted = _moe_call(hx, tile_group, need, src, w_gate_b, w_up_b, w_down_b, l)
            st["x"] = _ln2_call(f_sorted, pos, x1, T, mod_s, row(ln2_g), row(ln2_b))
            if si == 0:
                finals.append((jnp.swapaxes(h_T, 0, 1), s_T, c_T, n_T, m_T[..., 0]))
    outs = [st["x"].reshape(st["B"], st["T"], D_MODEL) for st in streams]
    stack = lambda i: jnp.stack([f[i] for f in finals], axis=1)
    return (outs[0], outs[1], stack(0), stack(1), stack(2), stack(3), stack(4))
```

```python
import functools

import jax
import jax.numpy as jnp
from jax import lax
from jax.experimental import pallas as pl
from jax.experimental.pallas import tpu as pltpu

F32 = jnp.float32
BF16 = jnp.bfloat16
HIGHEST = lax.Precision.HIGHEST

D_MODEL = 1024
DEPTH = 2
GRID_W = 64
D_GROUP = 512
D_MIX = 3 * D_GROUP
LRU_BLOCKS = 8
LRU_BLOCK = D_GROUP // LRU_BLOCKS
LRU_C = 8.0
N_HEADS = 4
D_HEAD = D_GROUP // N_HEADS
N_EXPERTS = 16
N_GROUPS = 4
EXPERTS_PER_GROUP = N_EXPERTS // N_GROUPS
D_FF = 512
ALPHA = (2.0 * DEPTH) ** 0.25
EPS = 1e-5
NEG = -1e30
TINY = 1e-30
N_GATE = 4 * N_HEADS
D_IN = 11 * D_GROUP + N_GATE
D_IN_PAD = 45 * 128
GATE_BLK = (11 * D_GROUP) // 128

SUBLANES = 8
LANES = 128
VMEM_LIMIT = 56 * 1024 * 1024

LRU_TC = 32
LRU_PAD = SUBLANES
HG_L = 64
HG_SUB = 16
ML_L = 128
ML_SUB = 32


def _cparams(n_axes):
    return pltpu.CompilerParams(dimension_semantics=("arbitrary",) * n_axes,
                                vmem_limit_bytes=VMEM_LIMIT)


def _dot(a, b):
    return jnp.dot(a.astype(BF16), b.astype(BF16), preferred_element_type=F32)


def _dot_nt(a, b):
    return lax.dot_general(a.astype(BF16), b.astype(BF16), (((1,), (1,)), ((), ())),
                           preferred_element_type=F32)


def _dot_tn(a, b):
    return lax.dot_general(a.astype(BF16), b.astype(BF16), (((0,), (0,)), ((), ())),
                           preferred_element_type=F32)


def _dot_f32(a, b):
    return jnp.dot(a, b, precision=HIGHEST, preferred_element_type=F32)


def _dot_split3(mask, x, mask_on_left):
    m = jnp.where(mask, 1.0, 0.0).astype(BF16)
    hi = x.astype(BF16)
    r1 = x - hi.astype(F32)
    mid = r1.astype(BF16)
    lo = (r1 - mid.astype(F32)).astype(BF16)
    mm = (lambda p: jnp.dot(m, p, preferred_element_type=F32)) if mask_on_left else (
        lambda p: jnp.dot(p, m, preferred_element_type=F32))
    return mm(hi) + mm(mid) + mm(lo)


def _softplus(x):
    return jnp.maximum(x, 0.0) + jnp.log1p(jnp.exp(-jnp.abs(x)))


def _silu(x):
    return x * jax.nn.sigmoid(x)


def _tri_masks(n):
    r = lax.broadcasted_iota(jnp.int32, (n, n), 0)
    c = lax.broadcasted_iota(jnp.int32, (n, n), 1)
    return r >= c, r <= c


def _ada_kernel(c_ref, w_ref, b_ref, o_ref):
    o_ref[0] = _dot_f32(_silu(c_ref[...]), w_ref[0]) + b_ref[0]


def _ada_call(c16, w_ada, b_ada):
    tn = 1536
    return pl.pallas_call(
        _ada_kernel,
        grid=(DEPTH, 6 * D_MODEL // tn),
        in_specs=[pl.BlockSpec((16, D_MODEL), lambda l, j: (0, 0)),
                  pl.BlockSpec((1, D_MODEL, tn), lambda l, j: (l, 0, j)),
                  pl.BlockSpec((1, 1, tn), lambda l, j: (l, 0, j))],
        out_specs=pl.BlockSpec((1, 16, tn), lambda l, j: (l, 0, j)),
        out_shape=jax.ShapeDtypeStruct((DEPTH, 16, 6 * D_MODEL), F32),
        compiler_params=_cparams(2),
        name="adaln",
    )(c16, w_ada, b_ada.reshape(DEPTH, 1, 6 * D_MODEL))


INPROJ_CW = 640
ROW_TILE = 256
WIDE_TILE = 512

SHIFT1, SCALE1, GATE1, SHIFT2, SCALE2, GATE2 = range(6)


def _mod_spec(mod, T, tm, n_grid_axes=1):
    per_seq = mod.shape[0] > 1
    if n_grid_axes == 1:
        return pl.BlockSpec((None, 6, D_MODEL), lambda i: ((i * tm) // T if per_seq else 0, 0, 0))
    return pl.BlockSpec((None, 6, D_MODEL), lambda i, e: ((i * tm) // T if per_seq else 0, 0, 0))


def _inproj_kernel(x_ref, mod_ref, w_ref, b_ref, z_ref):
    m = mod_ref[...]
    hb = (x_ref[...] * (1.0 + m[SCALE1:SCALE1 + 1]) + m[SHIFT1:SHIFT1 + 1]).astype(BF16)
    for j in range(D_IN_PAD // INPROJ_CW):
        cs = slice(j * INPROJ_CW, (j + 1) * INPROJ_CW)
        z_ref[:, cs] = jnp.dot(hb, w_ref[:, cs], preferred_element_type=F32) + b_ref[:, cs]


def _inproj_call(x, T, mod, w, b, l):
    n = x.shape[0]
    tm = WIDE_TILE if (mod.shape[0] == 1 or T % WIDE_TILE == 0) else ROW_TILE
    return pl.pallas_call(
        _inproj_kernel,
        grid=(n // tm,),
        in_specs=[pl.BlockSpec((tm, D_MODEL), lambda i: (i, 0)),
                  _mod_spec(mod, T, tm),
                  pl.BlockSpec((None, D_MODEL, D_IN_PAD), lambda i: (l, 0, 0), pipeline_mode=pl.Buffered(1)),
                  pl.BlockSpec((None, 1, D_IN_PAD), lambda i: (l, 0, 0))],
        out_specs=pl.BlockSpec((tm, D_IN_PAD), lambda i: (i, 0)),
        out_shape=jax.ShapeDtypeStruct((n, D_IN_PAD), F32),
        compiler_params=_cparams(1),
        name="inproj",
    )(x, mod, w, b)


def _rglru_kernel(xr_ref, gr_ref, cw_ref, cb_ref, wg_ref, bg_ref, lam_ref, h0_ref,
                  y_ref, hT_ref, xs_ref, os_ref, ob_ref):
    T = xr_ref.shape[1]
    P = T + LRU_PAD
    TC = LRU_TC
    nch = T // TC
    CP = 256
    zeros = jnp.zeros((LRU_PAD, LANES), F32)
    for s in range(SUBLANES + 1):
        xs_ref[s * P:s * P + LRU_PAD] = zeros
    for s in range(SUBLANES):
        def copy_in(c, carry, s=s):
            r0 = pl.multiple_of(c * CP, CP)
            xs_ref[pl.ds(s * P + LRU_PAD + r0, CP)] = xr_ref[s, pl.ds(r0, CP)]
            return carry
        lax.fori_loop(0, T // CP, copy_in, 0)

    cw = cw_ref[...]
    cb = cb_ref[...]
    sp = _softplus(-lam_ref[0])

    def gates(t0, d):
        win = [xs_ref[pl.ds(t0 + k + LRU_PAD - 2, SUBLANES, stride=P)] for k in range(TC + 3)]
        xc = jnp.stack([cw[0:1] * win[i] + cw[1:2] * win[i + 1] + cw[2:3] * win[i + 2] + cw[3:4] * win[i + 3] + cb
                        for i in range(TC)])
        x2 = xc.reshape(TC * SUBLANES, LANES)
        g = (jnp.dot(x2.astype(BF16), wg_ref[0, :, d * 256:(d + 1) * 256], preferred_element_type=F32)
             + bg_ref[0, :, d * 256:(d + 1) * 256])
        r = jax.nn.sigmoid(g[:, :LANES])
        i = jax.nn.sigmoid(g[:, LANES:])
        log_a = -LRU_C * r * sp[:, d * LANES:(d + 1) * LANES]
        a = jnp.exp(log_a)
        u = jnp.sqrt(jnp.maximum(1.0 - jnp.exp(2.0 * log_a), 0.0)) * (i * x2)
        return a.reshape(TC, SUBLANES, LANES), u.reshape(TC, SUBLANES, LANES)

    def step(c, carry):
        hf, hb = carry
        tf = pl.multiple_of(c * TC, TC)
        tb = pl.multiple_of((nch - 1 - c) * TC, TC)
        af, uf = gates(tf, 0)
        ab, ub = gates(tb, 1)
        for i in range(TC):
            hf = af[i] * hf + uf[i]
            os_ref[pl.ds(tf + i, SUBLANES, stride=P)] = hf
            k = TC - 1 - i
            hb = ab[k] * hb + ub[k]
            ob_ref[pl.ds(tb + k, SUBLANES, stride=P)] = hb
        return hf, hb

    hT_ref[0], hT_ref[1] = lax.fori_loop(0, nch, step, (h0_ref[0], h0_ref[1]))

    for s in range(SUBLANES):
        def copy_out(c, carry, s=s):
            r0 = pl.multiple_of(c * CP, CP)
            o = os_ref[pl.ds(s * P + r0, CP)] + ob_ref[pl.ds(s * P + r0, CP)]
            y_ref[s, pl.ds(r0, CP)] = jax.nn.gelu(gr_ref[s, pl.ds(r0, CP)]) * o
            return carry
        lax.fori_loop(0, T // CP, copy_out, 0)


def _rglru_call(z3, h0, cw, cb, wg, bg, lam):
    B, T, _ = z3.shape
    nb = B // SUBLANES
    ncb = D_GROUP // LANES
    blk = (SUBLANES, T, LANES)
    pitch = T + LRU_PAD
    return pl.pallas_call(
        _rglru_kernel,
        grid=(nb, ncb),
        in_specs=[pl.BlockSpec(blk, lambda b, c: (b, 0, c)),
                  pl.BlockSpec(blk, lambda b, c: (b, 0, ncb + c)),
                  pl.BlockSpec((4, LANES), lambda b, c: (0, c)),
                  pl.BlockSpec((1, LANES), lambda b, c: (0, c)),
                  pl.BlockSpec((1, LANES, 4 * LANES), lambda b, c: (c, 0, 0)),
                  pl.BlockSpec((1, 1, 4 * LANES), lambda b, c: (c, 0, 0)),
                  pl.BlockSpec((1, 1, 2 * LANES), lambda b, c: (c, 0, 0)),
                  pl.BlockSpec((2, SUBLANES, LANES), lambda b, c: (0, b, c))],
        out_specs=[pl.BlockSpec(blk, lambda b, c: (b, 0, c)),
                   pl.BlockSpec((2, SUBLANES, LANES), lambda b, c: (0, b, c))],
        out_shape=[jax.ShapeDtypeStruct((B, T, D_GROUP), F32),
                   jax.ShapeDtypeStruct((2, B, D_GROUP), F32)],
        scratch_shapes=[pltpu.VMEM((SUBLANES * pitch + LRU_PAD, LANES), F32),
                        pltpu.VMEM((SUBLANES * pitch, LANES), F32),
                        pltpu.VMEM((SUBLANES * pitch, LANES), F32)],
        compiler_params=_cparams(2),
        name="rglru",
    )(z3, z3, cw, cb, wg, bg, lam, h0)


def _hgrn_kernel(q_ref, ff_ref, fb_ref, v_ref, og_ref, lb_ref, ng_ref, s0_ref,
                 y_ref, sT_ref, st_ref, of_ref, ob_ref):
    T = q_ref.shape[0]
    L = HG_L
    SUB = HG_SUB
    nch = T // L
    nsub = L // SUB
    tril, triu = _tri_masks(L)
    heads = [slice(hd * D_HEAD, (hd + 1) * D_HEAD) for hd in range(N_HEADS)]

    for d in range(2):
        for hd in range(N_HEADS):
            st_ref[d, hd] = s0_ref[d, hd].T

    def step(c, carry):
        jobs = ((0, pl.multiple_of(c * L, L), of_ref), (1, pl.multiple_of((nch - 1 - c) * L, L), ob_ref))
        gated = []
        for d, t0, _ in jobs:
            q = _silu(q_ref[pl.ds(t0, L), :])
            v = v_ref[pl.ds(t0, L), :].astype(BF16)
            lb = lb_ref[d:d + 1, :]
            sig = jax.nn.sigmoid((ff_ref if d == 0 else fb_ref)[pl.ds(t0, L), :])
            log_f = jnp.log(jnp.maximum(lb + (1.0 - lb) * sig, TINY))
            kk = (1.0 - lb) * (1.0 - sig)
            mask = tril if d == 0 else triu
            b = _dot_split3(mask, log_f, mask_on_left=True)
            gated.append((q, v, kk, b, mask))
        factored = []
        for (d, _, _), (q, v, kk, b, mask) in zip(jobs, gated):
            b_tot = b[L - 1:L] if d == 0 else b[0:1]
            subs = []
            for i in range(nsub):
                rows = slice(i * SUB, (i + 1) * SUB)
                if d == 0:
                    cols = slice(0, (i + 1) * SUB)
                    edge = b[i * SUB - 1:i * SUB] if i > 0 else 0.0
                else:
                    cols = slice(i * SUB, L)
                    edge = b[(i + 1) * SUB:(i + 1) * SUB + 1] if i < nsub - 1 else 0.0
                subs.append((rows, cols, (q[rows] * jnp.exp(b[rows] - edge)).astype(BF16),
                             (kk[cols] * jnp.exp(edge - b[cols])).astype(BF16)))
            factored.append(((q * jnp.exp(b)).astype(BF16), (kk * jnp.exp(b_tot - b)).astype(BF16),
                             jnp.exp(b_tot), subs))
        scores = [[[_dot_nt(qs[:, hs], ks[:, hs]) for (_, _, qs, ks) in subs] for hs in heads]
                  for (_, _, _, subs) in factored]
        states = [[st_ref[d, hd] for hd in range(N_HEADS)] for d, _, _ in jobs]
        inter = [[_dot_nt(qdec[:, hs], states[j][hd]) for hd, hs in enumerate(heads)]
                 for j, (qdec, _, _, _) in enumerate(factored)]
        update = [[_dot_tn(gated[j][1][:, hs], kdec[:, hs]) for hs in heads]
                  for j, (_, kdec, _, _) in enumerate(factored)]
        for j, (d, t0, o_ref) in enumerate(jobs):
            v, mask = gated[j][1], gated[j][4]
            dec, subs = factored[j][2], factored[j][3]
            for hd, hs in enumerate(heads):
                intra = [_dot(jnp.where(mask[rows, cols], scores[j][hd][i], 0.0), v[cols, hs])
                         for i, (rows, cols, _, _) in enumerate(subs)]
                o_ref[pl.ds(t0, L), hs] = jnp.concatenate(intra, axis=0) + inter[j][hd]
                st_ref[d, hd] = states[j][hd] * dec[:, hs] + update[j][hd]
        return carry

    lax.fori_loop(0, nch, step, 0)

    def combine(c, carry):
        t0 = pl.multiple_of(c * L, L)
        for hs in heads:
            o = of_ref[pl.ds(t0, L), hs] + ob_ref[pl.ds(t0, L), hs]
            o = o * lax.rsqrt(jnp.mean(jnp.square(o), -1, keepdims=True) + EPS)
            y_ref[pl.ds(t0, L), hs] = o * ng_ref[:, hs] * _silu(og_ref[pl.ds(t0, L), hs])
        return carry

    lax.fori_loop(0, nch, combine, 0)

    for d in range(2):
        for hd in range(N_HEADS):
            sT_ref[d, hd] = st_ref[d, hd].T


def _seq_col_spec(T, j):
    return pl.BlockSpec((T, D_GROUP), lambda b, j=j: (b, j))


def _hgrn_call(z2, B, s0, lower, norm_g):
    T = z2.shape[0] // B
    col = functools.partial(_seq_col_spec, T)
    st_spec = pl.BlockSpec((None, 2, N_HEADS, D_HEAD, D_HEAD), lambda b: (b, 0, 0, 0, 0))
    return pl.pallas_call(
        _hgrn_kernel,
        grid=(B,),
        in_specs=[col(2), col(3), col(4), col(5), col(6),
                  pl.BlockSpec((2, D_GROUP), lambda b: (0, 0)),
                  pl.BlockSpec((1, D_GROUP), lambda b: (0, 0)),
                  st_spec],
        out_specs=[pl.BlockSpec((T, D_GROUP), lambda b: (b, 0)), st_spec],
        out_shape=[jax.ShapeDtypeStruct((B * T, D_GROUP), F32),
                   jax.ShapeDtypeStruct((B, 2, N_HEADS, D_HEAD, D_HEAD), F32)],
        scratch_shapes=[pltpu.VMEM((2, N_HEADS, D_HEAD, D_HEAD), F32),
                        pltpu.VMEM((T, D_GROUP), F32),
                        pltpu.VMEM((T, D_GROUP), F32)],
        compiler_params=_cparams(1),
        name="hgrn2",
    )(z2, z2, z2, z2, z2, lower, norm_g, s0)


def _mlstm_kernel(q_ref, k_ref, v_ref, og_ref, g_ref, ng_ref, c0_ref, n0_ref, m0_ref,
                  y_ref, cT_ref, nT_ref, mT_ref, cn_ref, of_ref, ob_ref):
    T = q_ref.shape[0]
    L = ML_L
    SUB = ML_SUB
    nch = T // L
    nsub = L // SUB
    tril, triu = _tri_masks(L)
    gcol = lax.broadcasted_iota(jnp.int32, (L, N_GATE), 1)
    heads = [slice(hd * D_HEAD, (hd + 1) * D_HEAD) for hd in range(N_HEADS)]
    ones = jnp.ones((L, D_HEAD), BF16)

    for d in range(2):
        for hd in range(N_HEADS):
            cn_ref[d, hd, :, 0:D_HEAD] = c0_ref[d, hd]
            cn_ref[d, hd, :, D_HEAD:] = jnp.broadcast_to(n0_ref[d, hd:hd + 1, :], (D_HEAD, D_HEAD)).T
    mT_ref[...] = m0_ref[...]

    def step(c, carry):
        jobs = ((0, pl.multiple_of(c * L, L), of_ref), (1, pl.multiple_of((nch - 1 - c) * L, L), ob_ref))
        loaded, q_state = [], []
        for d, t0, _ in jobs:
            g = g_ref[pl.ds(t0, L), 0:N_GATE]
            g = jnp.where(gcol % 8 >= N_HEADS, jax.nn.log_sigmoid(g), g)
            bc = _dot_split3(tril if d == 0 else triu, g, mask_on_left=True)
            qb = q_ref[pl.ds(t0, L), :].astype(BF16)
            kf = k_ref[pl.ds(t0, L), :] * (D_HEAD ** -0.5)
            vb = v_ref[pl.ds(t0, L), :].astype(BF16)
            v1 = [jnp.concatenate([vb[:, hs], ones], axis=1) for hs in heads]
            loaded.append((g, bc, qb, kf, v1))
            q_state.append([_dot(qb[:, hs], cn_ref[d, hd]) for hd, hs in enumerate(heads)])
        scaled = []
        for j, (d, _, _) in enumerate(jobs):
            g, bc, qb, kf, v1 = loaded[j]
            per_head = []
            for hd, hs in enumerate(heads):
                ic, fc = d * 8 + hd, d * 8 + N_HEADS + hd
                b_col = bc[:, fc:fc + 1]
                r_col = g[:, ic:ic + 1] - b_col
                b_tot = b_col[L - 1:L] if d == 0 else b_col[0:1]
                m_st = mT_ref[d, hd:hd + 1, 0:1]
                blk_max = [jnp.max(r_col[i * SUB:(i + 1) * SUB], axis=0, keepdims=True) for i in range(nsub)]
                r_rep = jnp.broadcast_to(r_col, (L, D_HEAD))
                subs = []
                for i in range(nsub):
                    rows = slice(i * SUB, (i + 1) * SUB)
                    cols = slice(0, (i + 1) * SUB) if d == 0 else slice(i * SUB, L)
                    seen = blk_max[:i + 1] if d == 0 else blk_max[i:]
                    mu = jnp.maximum(m_st, functools.reduce(jnp.maximum, seen))
                    kp = (kf[cols, hs] * jnp.exp(r_rep[cols] - mu)).astype(BF16)
                    floor = jnp.exp(-b_col[rows] - mu)
                    subs.append((rows, cols, mu, kp, floor))
                per_head.append((b_tot, m_st, subs))
            scaled.append(per_head)
        scores = [[[_dot_nt(loaded[j][2][rows, hs], kp) for (rows, _, _, kp, _) in scaled[j][hd][2]]
                   for hd, hs in enumerate(heads)] for j in range(2)]
        full = [nsub - 1, 0]
        update = [[_dot_tn(scaled[j][hd][2][full[j]][3], loaded[j][4][hd]) for hd in range(N_HEADS)]
                  for j in range(2)]
        for j, (d, t0, o_ref) in enumerate(jobs):
            mask = tril if d == 0 else triu
            for hd, hs in enumerate(heads):
                b_tot, m_st, subs = scaled[j][hd]
                outs = []
                for i, (rows, cols, mu, _, floor) in enumerate(subs):
                    w = jnp.where(mask[rows, cols], scores[j][hd][i], 0.0)
                    nd = _dot(w, loaded[j][4][hd][cols]) + jnp.exp(m_st - mu) * q_state[j][hd][rows]
                    outs.append(nd[:, 0:D_HEAD] / jnp.maximum(jnp.abs(nd[:, D_HEAD:]), floor))
                o_ref[pl.ds(t0, L), hs] = jnp.concatenate(outs, axis=0)
                mu_full = subs[full[j]][2]
                cn_ref[d, hd] = jnp.exp(m_st - mu_full) * cn_ref[d, hd] + update[j][hd]
                mT_ref[d, hd:hd + 1, :] = jnp.broadcast_to(b_tot + mu_full, (1, LANES))
        return carry

    lax.fori_loop(0, nch, step, 0)

    for d in range(2):
        for hd in range(N_HEADS):
            cT_ref[d, hd] = cn_ref[d, hd, :, 0:D_HEAD]
            nT_ref[d, hd:hd + 1, :] = cn_ref[d, hd, :, D_HEAD:].T[0:1, :]

    def combine(c, carry):
        t0 = pl.multiple_of(c * L, L)
        for hs in heads:
            o = of_ref[pl.ds(t0, L), hs] + ob_ref[pl.ds(t0, L), hs]
            mu = jnp.mean(o, -1, keepdims=True)
            var = jnp.mean(jnp.square(o - mu), -1, keepdims=True)
            o = (o - mu) * lax.rsqrt(var + EPS)
            y_ref[pl.ds(t0, L), hs] = jax.nn.sigmoid(og_ref[pl.ds(t0, L), hs]) * (o * ng_ref[:, hs])
        return carry

    lax.fori_loop(0, nch, combine, 0)


def _mlstm_call(z2, B, c0, n0, m0, norm_g):
    T = z2.shape[0] // B
    col = functools.partial(_seq_col_spec, T)
    c_spec = pl.BlockSpec((None, 2, N_HEADS, D_HEAD, D_HEAD), lambda b: (b, 0, 0, 0, 0))
    v_spec = pl.BlockSpec((None, 2, N_HEADS, D_HEAD), lambda b: (b, 0, 0, 0))
    return pl.pallas_call(
        _mlstm_kernel,
        grid=(B,),
        in_specs=[col(7), col(8), col(9), col(10),
                  pl.BlockSpec((T, LANES), lambda b: (b, GATE_BLK)),
                  pl.BlockSpec((1, D_GROUP), lambda b: (0, 0)),
                  c_spec, v_spec, v_spec],
        out_specs=[pl.BlockSpec((T, D_GROUP), lambda b: (b, 0)), c_spec, v_spec, v_spec],
        out_shape=[jax.ShapeDtypeStruct((B * T, D_GROUP), F32),
                   jax.ShapeDtypeStruct((B, 2, N_HEADS, D_HEAD, D_HEAD), F32),
                   jax.ShapeDtypeStruct((B, 2, N_HEADS, D_HEAD), F32),
                   jax.ShapeDtypeStruct((B, 2, N_HEADS, D_HEAD), F32)],
        scratch_shapes=[pltpu.VMEM((2, N_HEADS, D_HEAD, 2 * D_HEAD), F32),
                        pltpu.VMEM((T, D_GROUP), F32), pltpu.VMEM((T, D_GROUP), F32)],
        compiler_params=_cparams(1),
        name="mlstm",
    )(z2, z2, z2, z2, z2, norm_g, c0, n0, m0)


def _layer_norm(v, g, b):
    mu = jnp.mean(v, -1, keepdims=True)
    var = jnp.mean(jnp.square(v - mu), -1, keepdims=True)
    return (v - mu) * lax.rsqrt(var + EPS) * g + b


EXPERT_PAIRS = ((0, 1), (0, 2), (1, 2), (1, 3), (0, 3), (2, 3))
N_CLASSES = N_GROUPS * len(EXPERT_PAIRS)
CLASS_ROWS = 32


def _route(logits_t, b_router):
    mx = jnp.max(logits_t, axis=0, keepdims=True)
    ex = jnp.exp(logits_t - mx)
    scores = ex / jnp.sum(ex, axis=0, keepdims=True)
    sel = scores + b_router
    rows = [sel[e:e + 1] for e in range(N_EXPERTS)]
    first, second, grp = [], [], []
    for gi in range(N_GROUPS):
        r = rows[gi * EXPERTS_PER_GROUP:(gi + 1) * EXPERTS_PER_GROUP]
        m1 = functools.reduce(jnp.maximum, r)
        taken = jnp.zeros_like(m1, dtype=jnp.bool_)
        f = []
        for x in r:
            hit = jnp.logical_and(x == m1, jnp.logical_not(taken))
            f.append(hit)
            taken = jnp.logical_or(taken, hit)
        rest = [jnp.where(fi, -jnp.inf, x) for fi, x in zip(f, r)]
        m2 = functools.reduce(jnp.maximum, rest)
        taken = jnp.zeros_like(m1, dtype=jnp.bool_)
        s = []
        for x in rest:
            hit = jnp.logical_and(x == m2, jnp.logical_not(taken))
            s.append(hit)
            taken = jnp.logical_or(taken, hit)
        first.append(f)
        second.append(s)
        grp.append(m1 + m2)
    gmax = functools.reduce(jnp.maximum, grp)
    taken = jnp.zeros_like(gmax, dtype=jnp.bool_)
    chosen = []
    cls = jnp.zeros_like(gmax, dtype=jnp.int32)
    for gi in range(N_GROUPS):
        best = jnp.logical_and(grp[gi] == gmax, jnp.logical_not(taken))
        taken = jnp.logical_or(taken, best)
        in_grp = [jnp.logical_and(best, jnp.logical_or(first[gi][j], second[gi][j]))
                  for j in range(EXPERTS_PER_GROUP)]
        chosen.extend(in_grp)
        for p, (a, b) in enumerate(EXPERT_PAIRS):
            cls = jnp.where(jnp.logical_and(in_grp[a], in_grp[b]), gi * len(EXPERT_PAIRS) + p, cls)
    picked = [jnp.where(ch, scores[e:e + 1], 0.0) for e, ch in enumerate(chosen)]
    denom = functools.reduce(lambda a, b: a + b, picked)
    return jnp.concatenate([p / denom for p in picked], axis=0), cls


SLABS = D_MODEL // LANES


def _outproj_kernel(yr_ref, yh_ref, ym_ref, x_ref, mod_ref, w_ref, lg_ref, lb_ref,
                    wr_ref, br_ref, x1_ref, hx_ref, cls_ref, rank_ref, count_ref, cnt_ref):
    tm = x_ref.shape[0]
    m = mod_ref[...]
    y = (jnp.dot(yr_ref[...].astype(BF16), w_ref[0:D_GROUP], preferred_element_type=F32)
         + jnp.dot(yh_ref[...].astype(BF16), w_ref[D_GROUP:2 * D_GROUP], preferred_element_type=F32)
         + jnp.dot(ym_ref[...].astype(BF16), w_ref[2 * D_GROUP:], preferred_element_type=F32))
    x1 = _layer_norm(ALPHA * x_ref[...] + m[GATE1:GATE1 + 1] * y, lg_ref[...], lb_ref[...])
    x1_ref[...] = x1
    hm = x1 * (1.0 + m[SCALE2:SCALE2 + 1]) + m[SHIFT2:SHIFT2 + 1]
    for j in range(SLABS):
        hx_ref[:, j, :] = hm[:, j * LANES:(j + 1) * LANES]
    logits_t = lax.dot_general(wr_ref[...], hm, (((1,), (1,)), ((), ())), precision=HIGHEST,
                               preferred_element_type=F32)
    cmb_t, cls = _route(logits_t, br_ref[...])
    cmb_rows = jnp.concatenate([cmb_t, jnp.zeros((LANES - N_EXPERTS, tm), F32)], axis=0)
    hx_ref[:, SLABS, :] = cmb_rows.T
    cls_ref[...] = cls

    @pl.when(pl.program_id(0) == 0)
    def _():
        cnt_ref[...] = jnp.zeros_like(cnt_ref)

    onehot = jnp.where(lax.broadcasted_iota(jnp.int32, (CLASS_ROWS, tm), 0) == cls, 1.0, 0.0)
    earlier = (lax.broadcasted_iota(jnp.int32, (tm, tm), 0) < lax.broadcasted_iota(jnp.int32, (tm, tm), 1))
    before = jnp.dot(onehot.astype(BF16), jnp.where(earlier, 1.0, 0.0).astype(BF16), preferred_element_type=F32)
    rank = jnp.sum(onehot * (before + cnt_ref[:, 0:1]), axis=0, keepdims=True)
    rank_ref[...] = rank.astype(jnp.int32)
    cnt_ref[...] = cnt_ref[...] + jnp.sum(onehot, axis=1, keepdims=True)
    count_ref[...] = cnt_ref[...]


def _outproj_call(yr, yh, ym, x, T, mod, w_out, l, ln_g, ln_b, w_router_t, b_router):
    n = x.shape[0]
    tm = WIDE_TILE if (mod.shape[0] == 1 or T % WIDE_TILE == 0) else ROW_TILE
    row = lambda w: pl.BlockSpec((tm, w), lambda i: (i, 0))
    full = lambda a: pl.BlockSpec(a.shape, lambda i: (0,) * a.ndim)
    return pl.pallas_call(
        _outproj_kernel,
        grid=(n // tm,),
        in_specs=[row(D_GROUP), row(D_GROUP), row(D_GROUP), row(D_MODEL), _mod_spec(mod, T, tm),
                  pl.BlockSpec((None, D_MIX, D_MODEL), lambda i: (l, 0, 0)),
                  full(ln_g), full(ln_b), full(w_router_t), full(b_router)],
        out_specs=[row(D_MODEL), pl.BlockSpec((tm, SLABS + 1, LANES), lambda i: (i, 0, 0)),
                   pl.BlockSpec((1, tm), lambda i: (0, i)), pl.BlockSpec((1, tm), lambda i: (0, i)),
                   pl.BlockSpec((CLASS_ROWS, LANES), lambda i: (0, 0))],
        out_shape=[jax.ShapeDtypeStruct((n, D_MODEL), F32),
                   jax.ShapeDtypeStruct((n, SLABS + 1, LANES), F32),
                   jax.ShapeDtypeStruct((1, n), jnp.int32), jax.ShapeDtypeStruct((1, n), jnp.int32),
                   jax.ShapeDtypeStruct((CLASS_ROWS, LANES), F32)],
        scratch_shapes=[pltpu.VMEM((CLASS_ROWS, LANES), F32)],
        compiler_params=_cparams(1),
        name="outproj_ln_router",
    )(yr, yh, ym, x, mod, w_out, ln_g, ln_b, w_router_t, b_router)


MOE_TILE = 256


def _row_gather(idx_ref, first, n_rows, src_hbm, dst_ref, sem):
    def body(p, carry):
        for k in range(2):
            r = 2 * p + k
            pltpu.make_async_copy(src_hbm.at[idx_ref[first + r]], dst_ref.at[:, r, :], sem).start(priority=k)
        return carry
    lax.fori_loop(0, n_rows // 2, body, 0, unroll=4)


def _row_gather_wait(n_rows, src_hbm, dst_ref, sem):
    def body(r, carry):
        pltpu.make_async_copy(src_hbm.at[0], dst_ref.at[:, 0, :], sem).wait()
        return carry
    lax.fori_loop(0, n_rows, body, 0, unroll=8)


def _gathered_tile(idx_ref, src_hbm, buf_ref, sem_ref, tm):
    i = pl.program_id(0)
    slot = lax.rem(i, 2)

    @pl.when(i == 0)
    def _():
        _row_gather(idx_ref, 0, tm, src_hbm, buf_ref.at[0], sem_ref.at[0])

    @pl.when(i + 1 < pl.num_programs(0))
    def _():
        _row_gather(idx_ref, (i + 1) * tm, tm, src_hbm, buf_ref.at[1 - slot], sem_ref.at[1 - slot])

    _row_gather_wait(tm, src_hbm, buf_ref.at[slot], sem_ref.at[slot])
    return buf_ref.at[slot]


def _moe_kernel(tile_group_ref, need_ref, src_ref, hx_hbm, wg_ref, wu_ref, wd_ref, o_ref, buf_ref, sem_ref,
                acc_ref):
    tm = o_ref.shape[0]
    i = pl.program_id(0)
    rows_ref = _gathered_tile(src_ref, hx_hbm, buf_ref, sem_ref, tm)
    hm = jnp.concatenate([rows_ref[j] for j in range(SLABS)], axis=1).astype(BF16)
    cmb = rows_ref[SLABS]
    lane = lax.broadcasted_iota(jnp.int32, cmb.shape, 1)
    first_expert = tile_group_ref[i] * EXPERTS_PER_GROUP
    acc_ref[...] = jnp.zeros_like(acc_ref)
    for k in range(EXPERTS_PER_GROUP):
        @pl.when(need_ref[i * EXPERTS_PER_GROUP + k] != 0)
        def _(k=k):
            ce = jnp.sum(jnp.where(lane == first_expert + k, cmb, 0.0), axis=1, keepdims=True)
            hg = jnp.dot(hm, wg_ref[k], preferred_element_type=F32)
            hu = jnp.dot(hm, wu_ref[k], preferred_element_type=F32)
            acc_ref[...] += jnp.dot((_silu(hg) * hu * ce).astype(BF16), wd_ref[k], preferred_element_type=F32)
    for j in range(SLABS):
        o_ref[:, j, :] = acc_ref[:, j * LANES:(j + 1) * LANES]


def _moe_call(hx, tile_group, need, src, w_gate, w_up, w_down, l):
    n_pad = src.shape[0]
    tm = MOE_TILE
    grp_w = lambda shape: pl.BlockSpec((None,) + shape, lambda i, tg, nd, sr: (l, tg[i], 0, 0))
    return pl.pallas_call(
        _moe_kernel,
        grid_spec=pltpu.PrefetchScalarGridSpec(
            num_scalar_prefetch=3,
            grid=(n_pad // tm,),
            in_specs=[pl.BlockSpec(memory_space=pl.ANY),
                      grp_w((EXPERTS_PER_GROUP, D_MODEL, D_FF)), grp_w((EXPERTS_PER_GROUP, D_MODEL, D_FF)),
                      grp_w((EXPERTS_PER_GROUP, D_FF, D_MODEL))],
            out_specs=pl.BlockSpec((tm, SLABS, LANES), lambda i, tg, nd, sr: (i, 0, 0)),
            scratch_shapes=[pltpu.VMEM((2, SLABS + 1, tm, LANES), F32), pltpu.SemaphoreType.DMA((2,)),
                            pltpu.VMEM((tm, D_MODEL), F32)]),
        out_shape=jax.ShapeDtypeStruct((n_pad, SLABS, LANES), F32),
        compiler_params=_cparams(1),
        name="moe_sorted",
    )(tile_group, need, src, hx, w_gate, w_up, w_down)


def _ln2_kernel(pos_ref, f_hbm, x1_ref, mod_ref, lg_ref, lb_ref, o_ref, buf_ref, sem_ref):
    tm = o_ref.shape[0]
    rows_ref = _gathered_tile(pos_ref, f_hbm, buf_ref, sem_ref, tm)
    f = jnp.concatenate([rows_ref[j] for j in range(SLABS)], axis=1)
    v = ALPHA * x1_ref[...] + mod_ref[GATE2:GATE2 + 1, :] * f
    o_ref[...] = _layer_norm(v, lg_ref[...], lb_ref[...])


def _ln2_call(f_sorted, pos, x1, T, mod, ln_g, ln_b):
    n = x1.shape[0]
    tm = ROW_TILE
    per_seq = mod.shape[0] > 1
    return pl.pallas_call(
        _ln2_kernel,
        grid_spec=pltpu.PrefetchScalarGridSpec(
            num_scalar_prefetch=1,
            grid=(n // tm,),
            in_specs=[pl.BlockSpec(memory_space=pl.ANY),
                      pl.BlockSpec((tm, D_MODEL), lambda i, ps: (i, 0)),
                      pl.BlockSpec((None, 6, D_MODEL), lambda i, ps: ((i * tm) // T if per_seq else 0, 0, 0)),
                      pl.BlockSpec((1, D_MODEL), lambda i, ps: (0, 0)),
                      pl.BlockSpec((1, D_MODEL), lambda i, ps: (0, 0))],
            out_specs=pl.BlockSpec((tm, D_MODEL), lambda i, ps: (i, 0)),
            scratch_shapes=[pltpu.VMEM((2, SLABS, tm, LANES), F32), pltpu.SemaphoreType.DMA((2,))]),
        out_shape=jax.ShapeDtypeStruct((n, D_MODEL), F32),
        compiler_params=_cparams(1),
        name="moe_combine_ln",
    )(pos, f_sorted, x1, mod, ln_g, ln_b)


def _dispatch_plan(cls, rank, class_count, tm):
    n = cls.shape[0]
    n_pairs = len(EXPERT_PAIRS)
    n_pad = n + N_GROUPS * tm
    i32 = jnp.int32
    count = class_count.reshape(N_GROUPS, n_pairs)
    padded = ((jnp.sum(count, axis=1) + tm - 1) // tm) * tm
    ends = jnp.cumsum(padded)
    class_start = ((ends - padded)[:, None] + jnp.cumsum(count, axis=1) - count).reshape(N_CLASSES)
    pos = class_start[cls] + rank
    src = jnp.zeros((n_pad,), i32).at[pos].set(jnp.arange(n, dtype=i32))
    tile_start = jnp.arange(n_pad // tm, dtype=i32) * tm
    tile_group = jnp.minimum(jnp.sum((tile_start[:, None] >= ends[None, :]).astype(i32), axis=1), N_GROUPS - 1)
    class_end = class_start + count.reshape(N_CLASSES)
    overlap = ((class_start[None, :] < tile_start[:, None] + tm) & (class_end[None, :] > tile_start[:, None])
               & (class_end > class_start)[None, :]).astype(i32)
    pair_has = jnp.array([[int(k in p) for k in range(EXPERTS_PER_GROUP)] for p in EXPERT_PAIRS], i32)
    need = jnp.minimum(overlap @ jnp.tile(pair_has, (N_GROUPS, 1)), 1).reshape(-1)
    return pos, src, tile_group, need


def _grid_pos_embed(n_tokens):
    rows = n_tokens // GRID_W
    r = jnp.repeat(jnp.arange(rows, dtype=F32), GRID_W)
    c = jnp.tile(jnp.arange(GRID_W, dtype=F32), rows)
    q = D_MODEL // 4
    freq = jnp.exp(-jnp.log(10000.0) * jnp.arange(q, dtype=F32) / q)
    ar = r[:, None] * freq
    ac = c[:, None] * freq
    return jnp.concatenate([jnp.sin(ar), jnp.cos(ar), jnp.sin(ac), jnp.cos(ac)], axis=-1)


def _lru_gate_params(wa, ba, wx, bx, lam):
    ncb = D_GROUP // LANES

    def dense(w):
        z = jnp.zeros((ncb, LANES, LANES), F32)
        z = z.at[:, :LRU_BLOCK, :LRU_BLOCK].set(w[0::2])
        return z.at[:, LRU_BLOCK:, LRU_BLOCK:].set(w[1::2])

    wg = jnp.concatenate([dense(wa[0]), dense(wx[0]), dense(wa[1]), dense(wx[1])], axis=-1).astype(BF16)
    per_blk = lambda v: v.reshape(ncb, 1, LANES)
    bg = jnp.concatenate([per_blk(ba[0]), per_blk(bx[0]), per_blk(ba[1]), per_blk(bx[1])], axis=-1)
    lm = jnp.concatenate([per_blk(lam[0]), per_blk(lam[1])], axis=-1)
    return wg, bg, lm


def _mixer_states(B, l, states):
    if states is None:
        return (jnp.zeros((2, B, D_GROUP), F32),
                jnp.zeros((B, 2, N_HEADS, D_HEAD, D_HEAD), F32),
                jnp.zeros((B, 2, N_HEADS, D_HEAD, D_HEAD), F32),
                jnp.zeros((B, 2, N_HEADS, D_HEAD), F32),
                jnp.zeros((B, 2, N_HEADS, D_HEAD), F32))
    h, s, c, n, m = states
    return (jnp.swapaxes(h[:, l], 0, 1), s[:, l], c[:, l], n[:, l],
            jnp.broadcast_to(m[:, l][..., None], (B, 2, N_HEADS, D_HEAD)))


def kernel(x_prompt, x_sample, state_lru_h, state_hgrn_S, state_mlstm_C, state_mlstm_n, state_mlstm_m,
           c, c_ctx, w_ada, b_ada, w_in, b_in, conv_w, conv_b, lru_wa, lru_ba, lru_wx, lru_bx, lru_lam,
           hg_lb, hg_norm_g, ml_norm_g, w_out, ln1_g, ln1_b, ln2_g, ln2_b,
           w_router, b_router, w_gate, w_up, w_down):
    Bp, Tp, _ = x_prompt.shape
    Bs, Ts, _ = x_sample.shape
    assert Bs % SUBLANES == 0 and Bp % SUBLANES == 0 and Bs + 1 <= 16
    assert Tp % ROW_TILE == 0 and Ts % ROW_TILE == 0

    lb_soft = jax.nn.softmax(hg_lb.astype(F32), axis=0)
    hg_lower = jnp.cumsum(lb_soft, axis=0) - lb_soft[0:1]

    c16 = jnp.concatenate([c, c_ctx[None], jnp.zeros((16 - Bs - 1, D_MODEL), F32)], axis=0)
    mod = _ada_call(c16, w_ada, b_ada)

    w_in_p = jnp.pad(w_in, ((0, 0), (0, 0), (0, D_IN_PAD - D_IN))).astype(BF16)
    b_in_p = jnp.pad(b_in, ((0, 0), (0, D_IN_PAD - D_IN))).reshape(DEPTH, 1, D_IN_PAD)
    w_out_b = w_out.astype(BF16)
    w_gate_b, w_up_b, w_down_b = w_gate.astype(BF16), w_up.astype(BF16), w_down.astype(BF16)
    w_router_t = w_router.T
    b_router_c = b_router.reshape(N_EXPERTS, 1)

    xs = x_sample + _grid_pos_embed(Ts).astype(x_sample.dtype)
    streams = [
        dict(x=x_prompt.reshape(Bp * Tp, D_MODEL), T=Tp, B=Bp, states=None),
        dict(x=xs.reshape(Bs * Ts, D_MODEL), T=Ts, B=Bs,
             states=(state_lru_h, state_hgrn_S, state_mlstm_C, state_mlstm_n, state_mlstm_m)),
    ]
    finals = []
    for l in range(DEPTH):
        wg, bg, lm = _lru_gate_params(lru_wa[l], lru_ba[l], lru_wx[l], lru_bx[l], lru_lam[l])
        row = lambda v: v[l].reshape(1, -1)
        for si, st in enumerate(streams):
            T, B = st["T"], st["B"]
            mod_s = (mod[l, :Bs] if si == 1 else mod[l, Bs:Bs + 1]).reshape(-1, 6, D_MODEL)
            h0, s0, c0, n0, m0 = _mixer_states(B, l, st["states"])
            z = _inproj_call(st["x"], T, mod_s, w_in_p, b_in_p, l)
            y_r, h_T = _rglru_call(z.reshape(B, T, D_IN_PAD), h0, conv_w[l], row(conv_b), wg, bg, lm)
            y_h, s_T = _hgrn_call(z, B, s0, hg_lower[l], row(hg_norm_g))
            y_m, c_T, n_T, m_T = _mlstm_call(z, B, c0, n0, m0, row(ml_norm_g))
            x1, hx, cls, rank, cnt = _outproj_call(y_r.reshape(B * T, D_GROUP), y_h, y_m, st["x"], T, mod_s,
                                                   w_out_b, l, row(ln1_g), row(ln1_b), w_router_t, b_router_c)
            pos, src, tile_group, need = _dispatch_plan(cls[0], rank[0], cnt[:N_CLASSES, 0].astype(jnp.int32),
                                                        MOE_TILE)
            f_sorted = _moe_call(hx, tile_group, need, src, w_gate_b, w_up_b, w_down_b, l)
            st["x"] = _ln2_call(f_sorted, pos, x1, T, mod_s, row(ln2_g), row(ln2_b))
            if si == 0:
                finals.append((jnp.swapaxes(h_T, 0, 1), s_T, c_T, n_T, m_T[..., 0]))
    outs = [st["x"].reshape(st["B"], st["T"], D_MODEL) for st in streams]
    stack = lambda i: jnp.stack([f[i] for f in finals], axis=1)
    return (outs[0], outs[1], stack(0), stack(1), stack(2), stack(3), stack(4))
```

```python
import functools

import jax
import jax.numpy as jnp
from jax import lax
from jax.experimental import pallas as pl
from jax.experimental.pallas import tpu as pltpu

F32 = jnp.float32
BF16 = jnp.bfloat16
HIGHEST = lax.Precision.HIGHEST

D_MODEL = 1024
DEPTH = 2
GRID_W = 64
D_GROUP = 512
D_MIX = 3 * D_GROUP
LRU_BLOCKS = 8
LRU_BLOCK = D_GROUP // LRU_BLOCKS
LRU_C = 8.0
N_HEADS = 4
D_HEAD = D_GROUP // N_HEADS
N_EXPERTS = 16
N_GROUPS = 4
EXPERTS_PER_GROUP = N_EXPERTS // N_GROUPS
D_FF = 512
ALPHA = (2.0 * DEPTH) ** 0.25
EPS = 1e-5
NEG = -1e30
TINY = 1e-30
N_GATE = 4 * N_HEADS
D_IN = 11 * D_GROUP + N_GATE
D_IN_PAD = 45 * 128
GATE_BLK = (11 * D_GROUP) // 128

SUBLANES = 8
LANES = 128
VMEM_LIMIT = 56 * 1024 * 1024

LRU_TC = 32
LRU_PAD = SUBLANES
HG_L = 128
HG_SUB = 16
ML_L = 128
ML_SUB = 32


def _cparams(n_axes):
    return pltpu.CompilerParams(dimension_semantics=("arbitrary",) * n_axes,
                                vmem_limit_bytes=VMEM_LIMIT)


def _dot(a, b):
    return jnp.dot(a.astype(BF16), b.astype(BF16), preferred_element_type=F32)


def _dot_nt(a, b):
    return lax.dot_general(a.astype(BF16), b.astype(BF16), (((1,), (1,)), ((), ())),
                           preferred_element_type=F32)


def _dot_tn(a, b):
    return lax.dot_general(a.astype(BF16), b.astype(BF16), (((0,), (0,)), ((), ())),
                           preferred_element_type=F32)


def _dot_f32(a, b):
    return jnp.dot(a, b, precision=HIGHEST, preferred_element_type=F32)


def _dot_split3(mask, x, mask_on_left):
    m = jnp.where(mask, 1.0, 0.0).astype(BF16)
    hi = x.astype(BF16)
    r1 = x - hi.astype(F32)
    mid = r1.astype(BF16)
    lo = (r1 - mid.astype(F32)).astype(BF16)
    mm = (lambda p: jnp.dot(m, p, preferred_element_type=F32)) if mask_on_left else (
        lambda p: jnp.dot(p, m, preferred_element_type=F32))
    return mm(hi) + mm(mid) + mm(lo)


def _softplus(x):
    return jnp.maximum(x, 0.0) + jnp.log1p(jnp.exp(-jnp.abs(x)))


def _silu(x):
    return x * jax.nn.sigmoid(x)


def _tri_masks(n):
    r = lax.broadcasted_iota(jnp.int32, (n, n), 0)
    c = lax.broadcasted_iota(jnp.int32, (n, n), 1)
    return r >= c, r <= c


def _ada_kernel(c_ref, w_ref, b_ref, o_ref):
    o_ref[0] = _dot_f32(_silu(c_ref[...]), w_ref[0]) + b_ref[0]


def _ada_call(c16, w_ada, b_ada):
    tn = 1536
    return pl.pallas_call(
        _ada_kernel,
        grid=(DEPTH, 6 * D_MODEL // tn),
        in_specs=[pl.BlockSpec((16, D_MODEL), lambda l, j: (0, 0)),
                  pl.BlockSpec((1, D_MODEL, tn), lambda l, j: (l, 0, j)),
                  pl.BlockSpec((1, 1, tn), lambda l, j: (l, 0, j))],
        out_specs=pl.BlockSpec((1, 16, tn), lambda l, j: (l, 0, j)),
        out_shape=jax.ShapeDtypeStruct((DEPTH, 16, 6 * D_MODEL), F32),
        compiler_params=_cparams(2),
        name="adaln",
    )(c16, w_ada, b_ada.reshape(DEPTH, 1, 6 * D_MODEL))


INPROJ_CW = 640
ROW_TILE = 256
WIDE_TILE = 512

SHIFT1, SCALE1, GATE1, SHIFT2, SCALE2, GATE2 = range(6)


def _mod_spec(mod, T, tm, n_grid_axes=1):
    per_seq = mod.shape[0] > 1
    if n_grid_axes == 1:
        return pl.BlockSpec((None, 6, D_MODEL), lambda i: ((i * tm) // T if per_seq else 0, 0, 0))
    return pl.BlockSpec((None, 6, D_MODEL), lambda i, e: ((i * tm) // T if per_seq else 0, 0, 0))


def _inproj_kernel(x_ref, mod_ref, w_ref, b_ref, z_ref):
    m = mod_ref[...]
    hb = (x_ref[...] * (1.0 + m[SCALE1:SCALE1 + 1]) + m[SHIFT1:SHIFT1 + 1]).astype(BF16)
    for j in range(D_IN_PAD // INPROJ_CW):
        cs = slice(j * INPROJ_CW, (j + 1) * INPROJ_CW)
        z_ref[:, cs] = jnp.dot(hb, w_ref[:, cs], preferred_element_type=F32) + b_ref[:, cs]


def _inproj_call(x, T, mod, w, b, l):
    n = x.shape[0]
    tm = WIDE_TILE if (mod.shape[0] == 1 or T % WIDE_TILE == 0) else ROW_TILE
    return pl.pallas_call(
        _inproj_kernel,
        grid=(n // tm,),
        in_specs=[pl.BlockSpec((tm, D_MODEL), lambda i: (i, 0)),
                  _mod_spec(mod, T, tm),
                  pl.BlockSpec((None, D_MODEL, D_IN_PAD), lambda i: (l, 0, 0), pipeline_mode=pl.Buffered(1)),
                  pl.BlockSpec((None, 1, D_IN_PAD), lambda i: (l, 0, 0))],
        out_specs=pl.BlockSpec((tm, D_IN_PAD), lambda i: (i, 0)),
        out_shape=jax.ShapeDtypeStruct((n, D_IN_PAD), F32),
        compiler_params=_cparams(1),
        name="inproj",
    )(x, mod, w, b)


def _rglru_kernel(xr_ref, gr_ref, cw_ref, cb_ref, wg_ref, bg_ref, lam_ref, h0_ref,
                  y_ref, hT_ref, xs_ref, os_ref, ob_ref):
    T = xr_ref.shape[1]
    P = T + LRU_PAD
    TC = LRU_TC
    nch = T // TC
    CP = 256
    zeros = jnp.zeros((LRU_PAD, LANES), F32)
    for s in range(SUBLANES + 1):
        xs_ref[s * P:s * P + LRU_PAD] = zeros
    for s in range(SUBLANES):
        def copy_in(c, carry, s=s):
            r0 = pl.multiple_of(c * CP, CP)
            xs_ref[pl.ds(s * P + LRU_PAD + r0, CP)] = xr_ref[s, pl.ds(r0, CP)]
            return carry
        lax.fori_loop(0, T // CP, copy_in, 0)

    cw = cw_ref[...]
    cb = cb_ref[...]
    sp = _softplus(-lam_ref[0])

    def gates(t0, d):
        win = [xs_ref[pl.ds(t0 + k + LRU_PAD - 2, SUBLANES, stride=P)] for k in range(TC + 3)]
        xc = jnp.stack([cw[0:1] * win[i] + cw[1:2] * win[i + 1] + cw[2:3] * win[i + 2] + cw[3:4] * win[i + 3] + cb
                        for i in range(TC)])
        x2 = xc.reshape(TC * SUBLANES, LANES)
        g = (jnp.dot(x2.astype(BF16), wg_ref[0, :, d * 256:(d + 1) * 256], preferred_element_type=F32)
             + bg_ref[0, :, d * 256:(d + 1) * 256])
        r = jax.nn.sigmoid(g[:, :LANES])
        i = jax.nn.sigmoid(g[:, LANES:])
        log_a = -LRU_C * r * sp[:, d * LANES:(d + 1) * LANES]
        a = jnp.exp(log_a)
        u = jnp.sqrt(jnp.maximum(1.0 - jnp.exp(2.0 * log_a), 0.0)) * (i * x2)
        return a.reshape(TC, SUBLANES, LANES), u.reshape(TC, SUBLANES, LANES)

    def step(c, carry):
        hf, hb = carry
        tf = pl.multiple_of(c * TC, TC)
        tb = pl.multiple_of((nch - 1 - c) * TC, TC)
        af, uf = gates(tf, 0)
        ab, ub = gates(tb, 1)
        for i in range(TC):
            hf = af[i] * hf + uf[i]
            os_ref[pl.ds(tf + i, SUBLANES, stride=P)] = hf
            k = TC - 1 - i
            hb = ab[k] * hb + ub[k]
            ob_ref[pl.ds(tb + k, SUBLANES, stride=P)] = hb
        return hf, hb

    hT_ref[0], hT_ref[1] = lax.fori_loop(0, nch, step, (h0_ref[0], h0_ref[1]))

    for s in range(SUBLANES):
        def copy_out(c, carry, s=s):
            r0 = pl.multiple_of(c * CP, CP)
            o = os_ref[pl.ds(s * P + r0, CP)] + ob_ref[pl.ds(s * P + r0, CP)]
            y_ref[s, pl.ds(r0, CP)] = jax.nn.gelu(gr_ref[s, pl.ds(r0, CP)]) * o
            return carry
        lax.fori_loop(0, T // CP, copy_out, 0)


def _rglru_call(z3, h0, cw, cb, wg, bg, lam):
    B, T, _ = z3.shape
    nb = B // SUBLANES
    ncb = D_GROUP // LANES
    blk = (SUBLANES, T, LANES)
    pitch = T + LRU_PAD
    return pl.pallas_call(
        _rglru_kernel,
        grid=(nb, ncb),
        in_specs=[pl.BlockSpec(blk, lambda b, c: (b, 0, c)),
                  pl.BlockSpec(blk, lambda b, c: (b, 0, ncb + c)),
                  pl.BlockSpec((4, LANES), lambda b, c: (0, c)),
                  pl.BlockSpec((1, LANES), lambda b, c: (0, c)),
                  pl.BlockSpec((1, LANES, 4 * LANES), lambda b, c: (c, 0, 0)),
                  pl.BlockSpec((1, 1, 4 * LANES), lambda b, c: (c, 0, 0)),
                  pl.BlockSpec((1, 1, 2 * LANES), lambda b, c: (c, 0, 0)),
                  pl.BlockSpec((2, SUBLANES, LANES), lambda b, c: (0, b, c))],
        out_specs=[pl.BlockSpec(blk, lambda b, c: (b, 0, c)),
                   pl.BlockSpec((2, SUBLANES, LANES), lambda b, c: (0, b, c))],
        out_shape=[jax.ShapeDtypeStruct((B, T, D_GROUP), F32),
                   jax.ShapeDtypeStruct((2, B, D_GROUP), F32)],
        scratch_shapes=[pltpu.VMEM((SUBLANES * pitch + LRU_PAD, LANES), F32),
                        pltpu.VMEM((SUBLANES * pitch, LANES), F32),
                        pltpu.VMEM((SUBLANES * pitch, LANES), F32)],
        compiler_params=_cparams(2),
        name="rglru",
    )(z3, z3, cw, cb, wg, bg, lam, h0)


def _hgrn_kernel(q_ref, ff_ref, fb_ref, v_ref, og_ref, lb_ref, ng_ref, s0_ref,
                 y_ref, sT_ref, st_ref, of_ref, ob_ref):
    T = q_ref.shape[0]
    L = HG_L
    SUB = HG_SUB
    nch = T // L
    nsub = L // SUB
    tril, triu = _tri_masks(L)
    heads = [slice(hd * D_HEAD, (hd + 1) * D_HEAD) for hd in range(N_HEADS)]

    for d in range(2):
        for hd in range(N_HEADS):
            st_ref[d, hd] = s0_ref[d, hd].T

    def step(c, carry):
        jobs = ((0, pl.multiple_of(c * L, L), of_ref), (1, pl.multiple_of((nch - 1 - c) * L, L), ob_ref))
        gated = []
        for d, t0, _ in jobs:
            q = _silu(q_ref[pl.ds(t0, L), :])
            v = v_ref[pl.ds(t0, L), :].astype(BF16)
            lb = lb_ref[d:d + 1, :]
            sig = jax.nn.sigmoid((ff_ref if d == 0 else fb_ref)[pl.ds(t0, L), :])
            log_f = jnp.log(jnp.maximum(lb + (1.0 - lb) * sig, TINY))
            kk = (1.0 - lb) * (1.0 - sig)
            mask = tril if d == 0 else triu
            b = _dot_split3(mask, log_f, mask_on_left=True)
            gated.append((q, v, kk, b, mask))
        factored = []
        for (d, _, _), (q, v, kk, b, mask) in zip(jobs, gated):
            b_tot = b[L - 1:L] if d == 0 else b[0:1]
            subs = []
            for i in range(nsub):
                rows = slice(i * SUB, (i + 1) * SUB)
                if d == 0:
                    cols = slice(0, (i + 1) * SUB)
                    edge = b[i * SUB - 1:i * SUB] if i > 0 else 0.0
                else:
                    cols = slice(i * SUB, L)
                    edge = b[(i + 1) * SUB:(i + 1) * SUB + 1] if i < nsub - 1 else 0.0
                subs.append((rows, cols, (q[rows] * jnp.exp(b[rows] - edge)).astype(BF16),
                             (kk[cols] * jnp.exp(edge - b[cols])).astype(BF16)))
            factored.append(((q * jnp.exp(b)).astype(BF16), (kk * jnp.exp(b_tot - b)).astype(BF16),
                             jnp.exp(b_tot), subs))
        scores = [[[_dot_nt(qs[:, hs], ks[:, hs]) for (_, _, qs, ks) in subs] for hs in heads]
                  for (_, _, _, subs) in factored]
        states = [[st_ref[d, hd] for hd in range(N_HEADS)] for d, _, _ in jobs]
        inter = [[_dot_nt(qdec[:, hs], states[j][hd]) for hd, hs in enumerate(heads)]
                 for j, (qdec, _, _, _) in enumerate(factored)]
        update = [[_dot_tn(gated[j][1][:, hs], kdec[:, hs]) for hs in heads]
                  for j, (_, kdec, _, _) in enumerate(factored)]
        for j, (d, t0, o_ref) in enumerate(jobs):
            v, mask = gated[j][1], gated[j][4]
            dec, subs = factored[j][2], factored[j][3]
            for hd, hs in enumerate(heads):
                intra = [_dot(jnp.where(mask[rows, cols], scores[j][hd][i], 0.0), v[cols, hs])
                         for i, (rows, cols, _, _) in enumerate(subs)]
                o_ref[pl.ds(t0, L), hs] = jnp.concatenate(intra, axis=0) + inter[j][hd]
                st_ref[d, hd] = states[j][hd] * dec[:, hs] + update[j][hd]
        return carry

    lax.fori_loop(0, nch, step, 0)

    def combine(c, carry):
        t0 = pl.multiple_of(c * L, L)
        for hs in heads:
            o = of_ref[pl.ds(t0, L), hs] + ob_ref[pl.ds(t0, L), hs]
            o = o * lax.rsqrt(jnp.mean(jnp.square(o), -1, keepdims=True) + EPS)
            y_ref[pl.ds(t0, L), hs] = o * ng_ref[:, hs] * _silu(og_ref[pl.ds(t0, L), hs])
        return carry

    lax.fori_loop(0, nch, combine, 0)

    for d in range(2):
        for hd in range(N_HEADS):
            sT_ref[d, hd] = st_ref[d, hd].T


def _seq_col_spec(T, j):
    return pl.BlockSpec((T, D_GROUP), lambda b, j=j: (b, j))


def _hgrn_call(z2, B, s0, lower, norm_g):
    T = z2.shape[0] // B
    col = functools.partial(_seq_col_spec, T)
    st_spec = pl.BlockSpec((None, 2, N_HEADS, D_HEAD, D_HEAD), lambda b: (b, 0, 0, 0, 0))
    return pl.pallas_call(
        _hgrn_kernel,
        grid=(B,),
        in_specs=[col(2), col(3), col(4), col(5), col(6),
                  pl.BlockSpec((2, D_GROUP), lambda b: (0, 0)),
                  pl.BlockSpec((1, D_GROUP), lambda b: (0, 0)),
                  st_spec],
        out_specs=[pl.BlockSpec((T, D_GROUP), lambda b: (b, 0)), st_spec],
        out_shape=[jax.ShapeDtypeStruct((B * T, D_GROUP), F32),
                   jax.ShapeDtypeStruct((B, 2, N_HEADS, D_HEAD, D_HEAD), F32)],
        scratch_shapes=[pltpu.VMEM((2, N_HEADS, D_HEAD, D_HEAD), F32),
                        pltpu.VMEM((T, D_GROUP), F32),
                        pltpu.VMEM((T, D_GROUP), F32)],
        compiler_params=_cparams(1),
        name="hgrn2",
    )(z2, z2, z2, z2, z2, lower, norm_g, s0)


def _mlstm_kernel(q_ref, k_ref, v_ref, og_ref, g_ref, ng_ref, c0_ref, n0_ref, m0_ref,
                  y_ref, cT_ref, nT_ref, mT_ref, cn_ref, of_ref, ob_ref):
    T = q_ref.shape[0]
    L = ML_L
    SUB = ML_SUB
    nch = T // L
    nsub = L // SUB
    tril, triu = _tri_masks(L)
    gcol = lax.broadcasted_iota(jnp.int32, (L, N_GATE), 1)
    heads = [slice(hd * D_HEAD, (hd + 1) * D_HEAD) for hd in range(N_HEADS)]
    ones = jnp.ones((L, D_HEAD), BF16)

    for d in range(2):
        for hd in range(N_HEADS):
            cn_ref[d, hd, :, 0:D_HEAD] = c0_ref[d, hd]
            cn_ref[d, hd, :, D_HEAD:] = jnp.broadcast_to(n0_ref[d, hd:hd + 1, :], (D_HEAD, D_HEAD)).T
    mT_ref[...] = m0_ref[...]

    def step(c, carry):
        jobs = ((0, pl.multiple_of(c * L, L), of_ref), (1, pl.multiple_of((nch - 1 - c) * L, L), ob_ref))
        loaded, q_state = [], []
        for d, t0, _ in jobs:
            g = g_ref[pl.ds(t0, L), 0:N_GATE]
            g = jnp.where(gcol % 8 >= N_HEADS, jax.nn.log_sigmoid(g), g)
            bc = _dot_split3(tril if d == 0 else triu, g, mask_on_left=True)
            qb = q_ref[pl.ds(t0, L), :].astype(BF16)
            kf = k_ref[pl.ds(t0, L), :] * (D_HEAD ** -0.5)
            vb = v_ref[pl.ds(t0, L), :].astype(BF16)
            v1 = [jnp.concatenate([vb[:, hs], ones], axis=1) for hs in heads]
            loaded.append((g, bc, qb, kf, v1))
            q_state.append([_dot(qb[:, hs], cn_ref[d, hd]) for hd, hs in enumerate(heads)])
        scaled = []
        for j, (d, _, _) in enumerate(jobs):
            g, bc, qb, kf, v1 = loaded[j]
            per_head = []
            for hd, hs in enumerate(heads):
                ic, fc = d * 8 + hd, d * 8 + N_HEADS + hd
                b_col = bc[:, fc:fc + 1]
                r_col = g[:, ic:ic + 1] - b_col
                b_tot = b_col[L - 1:L] if d == 0 else b_col[0:1]
                m_st = mT_ref[d, hd:hd + 1, 0:1]
                blk_max = [jnp.max(r_col[i * SUB:(i + 1) * SUB], axis=0, keepdims=True) for i in range(nsub)]
                r_rep = jnp.broadcast_to(r_col, (L, D_HEAD))
                subs = []
                for i in range(nsub):
                    rows = slice(i * SUB, (i + 1) * SUB)
                    cols = slice(0, (i + 1) * SUB) if d == 0 else slice(i * SUB, L)
                    seen = blk_max[:i + 1] if d == 0 else blk_max[i:]
                    mu = jnp.maximum(m_st, functools.reduce(jnp.maximum, seen))
                    kp = (kf[cols, hs] * jnp.exp(r_rep[cols] - mu)).astype(BF16)
                    floor = jnp.exp(-b_col[rows] - mu)
                    subs.append((rows, cols, mu, kp, floor))
                per_head.append((b_tot, m_st, subs))
            scaled.append(per_head)
        scores = [[[_dot_nt(loaded[j][2][rows, hs], kp) for (rows, _, _, kp, _) in scaled[j][hd][2]]
                   for hd, hs in enumerate(heads)] for j in range(2)]
        full = [nsub - 1, 0]
        update = [[_dot_tn(scaled[j][hd][2][full[j]][3], loaded[j][4][hd]) for hd in range(N_HEADS)]
                  for j in range(2)]
        for j, (d, t0, o_ref) in enumerate(jobs):
            mask = tril if d == 0 else triu
            for hd, hs in enumerate(heads):
                b_tot, m_st, subs = scaled[j][hd]
                outs = []
                for i, (rows, cols, mu, _, floor) in enumerate(subs):
                    w = jnp.where(mask[rows, cols], scores[j][hd][i], 0.0)
                    nd = _dot(w, loaded[j][4][hd][cols]) + jnp.exp(m_st - mu) * q_state[j][hd][rows]
                    outs.append(nd[:, 0:D_HEAD] / jnp.maximum(jnp.abs(nd[:, D_HEAD:]), floor))
                o_ref[pl.ds(t0, L), hs] = jnp.concatenate(outs, axis=0)
                mu_full = subs[full[j]][2]
                cn_ref[d, hd] = jnp.exp(m_st - mu_full) * cn_ref[d, hd] + update[j][hd]
                mT_ref[d, hd:hd + 1, :] = jnp.broadcast_to(b_tot + mu_full, (1, LANES))
        return carry

    lax.fori_loop(0, nch, step, 0)

    for d in range(2):
        for hd in range(N_HEADS):
            cT_ref[d, hd] = cn_ref[d, hd, :, 0:D_HEAD]
            nT_ref[d, hd:hd + 1, :] = cn_ref[d, hd, :, D_HEAD:].T[0:1, :]

    def combine(c, carry):
        t0 = pl.multiple_of(c * L, L)
        for hs in heads:
            o = of_ref[pl.ds(t0, L), hs] + ob_ref[pl.ds(t0, L), hs]
            mu = jnp.mean(o, -1, keepdims=True)
            var = jnp.mean(jnp.square(o - mu), -1, keepdims=True)
            o = (o - mu) * lax.rsqrt(var + EPS)
            y_ref[pl.ds(t0, L), hs] = jax.nn.sigmoid(og_ref[pl.ds(t0, L), hs]) * (o * ng_ref[:, hs])
        return carry

    lax.fori_loop(0, nch, combine, 0)


def _mlstm_call(z2, B, c0, n0, m0, norm_g):
    T = z2.shape[0] // B
    col = functools.partial(_seq_col_spec, T)
    c_spec = pl.BlockSpec((None, 2, N_HEADS, D_HEAD, D_HEAD), lambda b: (b, 0, 0, 0, 0))
    v_spec = pl.BlockSpec((None, 2, N_HEADS, D_HEAD), lambda b: (b, 0, 0, 0))
    return pl.pallas_call(
        _mlstm_kernel,
        grid=(B,),
        in_specs=[col(7), col(8), col(9), col(10),
                  pl.BlockSpec((T, LANES), lambda b: (b, GATE_BLK)),
                  pl.BlockSpec((1, D_GROUP), lambda b: (0, 0)),
                  c_spec, v_spec, v_spec],
        out_specs=[pl.BlockSpec((T, D_GROUP), lambda b: (b, 0)), c_spec, v_spec, v_spec],
        out_shape=[jax.ShapeDtypeStruct((B * T, D_GROUP), F32),
                   jax.ShapeDtypeStruct((B, 2, N_HEADS, D_HEAD, D_HEAD), F32),
                   jax.ShapeDtypeStruct((B, 2, N_HEADS, D_HEAD), F32),
                   jax.ShapeDtypeStruct((B, 2, N_HEADS, D_HEAD), F32)],
        scratch_shapes=[pltpu.VMEM((2, N_HEADS, D_HEAD, 2 * D_HEAD), F32),
                        pltpu.VMEM((T, D_GROUP), F32), pltpu.VMEM((T, D_GROUP), F32)],
        compiler_params=_cparams(1),
        name="mlstm",
    )(z2, z2, z2, z2, z2, norm_g, c0, n0, m0)


def _layer_norm(v, g, b):
    mu = jnp.mean(v, -1, keepdims=True)
    var = jnp.mean(jnp.square(v - mu), -1, keepdims=True)
    return (v - mu) * lax.rsqrt(var + EPS) * g + b


EXPERT_PAIRS = ((0, 1), (0, 2), (1, 2), (1, 3), (0, 3), (2, 3))
N_CLASSES = N_GROUPS * len(EXPERT_PAIRS)
CLASS_ROWS = 32


def _route(logits_t, b_router):
    mx = jnp.max(logits_t, axis=0, keepdims=True)
    ex = jnp.exp(logits_t - mx)
    scores = ex / jnp.sum(ex, axis=0, keepdims=True)
    sel = scores + b_router
    rows = [sel[e:e + 1] for e in range(N_EXPERTS)]
    first, second, grp = [], [], []
    for gi in range(N_GROUPS):
        r = rows[gi * EXPERTS_PER_GROUP:(gi + 1) * EXPERTS_PER_GROUP]
        m1 = functools.reduce(jnp.maximum, r)
        taken = jnp.zeros_like(m1, dtype=jnp.bool_)
        f = []
        for x in r:
            hit = jnp.logical_and(x == m1, jnp.logical_not(taken))
            f.append(hit)
            taken = jnp.logical_or(taken, hit)
        rest = [jnp.where(fi, -jnp.inf, x) for fi, x in zip(f, r)]
        m2 = functools.reduce(jnp.maximum, rest)
        taken = jnp.zeros_like(m1, dtype=jnp.bool_)
        s = []
        for x in rest:
            hit = jnp.logical_and(x == m2, jnp.logical_not(taken))
            s.append(hit)
            taken = jnp.logical_or(taken, hit)
        first.append(f)
        second.append(s)
        grp.append(m1 + m2)
    gmax = functools.reduce(jnp.maximum, grp)
    taken = jnp.zeros_like(gmax, dtype=jnp.bool_)
    chosen = []
    cls = jnp.zeros_like(gmax, dtype=jnp.int32)
    for gi in range(N_GROUPS):
        best = jnp.logical_and(grp[gi] == gmax, jnp.logical_not(taken))
        taken = jnp.logical_or(taken, best)
        in_grp = [jnp.logical_and(best, jnp.logical_or(first[gi][j], second[gi][j]))
                  for j in range(EXPERTS_PER_GROUP)]
        chosen.extend(in_grp)
        for p, (a, b) in enumerate(EXPERT_PAIRS):
            cls = jnp.where(jnp.logical_and(in_grp[a], in_grp[b]), gi * len(EXPERT_PAIRS) + p, cls)
    picked = [jnp.where(ch, scores[e:e + 1], 0.0) for e, ch in enumerate(chosen)]
    denom = functools.reduce(lambda a, b: a + b, picked)
    return jnp.concatenate([p / denom for p in picked], axis=0), cls


SLABS = D_MODEL // LANES


def _outproj_kernel(yr_ref, yh_ref, ym_ref, x_ref, mod_ref, w_ref, lg_ref, lb_ref,
                    wr_ref, br_ref, x1_ref, hx_ref, cls_ref, rank_ref, count_ref, cnt_ref):
    tm = x_ref.shape[0]
    m = mod_ref[...]
    y = (jnp.dot(yr_ref[...].astype(BF16), w_ref[0:D_GROUP], preferred_element_type=F32)
         + jnp.dot(yh_ref[...].astype(BF16), w_ref[D_GROUP:2 * D_GROUP], preferred_element_type=F32)
         + jnp.dot(ym_ref[...].astype(BF16), w_ref[2 * D_GROUP:], preferred_element_type=F32))
    x1 = _layer_norm(ALPHA * x_ref[...] + m[GATE1:GATE1 + 1] * y, lg_ref[...], lb_ref[...])
    x1_ref[...] = x1
    hm = x1 * (1.0 + m[SCALE2:SCALE2 + 1]) + m[SHIFT2:SHIFT2 + 1]
    for j in range(SLABS):
        hx_ref[:, j, :] = hm[:, j * LANES:(j + 1) * LANES]
    logits_t = lax.dot_general(wr_ref[...], hm, (((1,), (1,)), ((), ())), precision=HIGHEST,
                               preferred_element_type=F32)
    cmb_t, cls = _route(logits_t, br_ref[...])
    cmb_rows = jnp.concatenate([cmb_t, jnp.zeros((LANES - N_EXPERTS, tm), F32)], axis=0)
    hx_ref[:, SLABS, :] = cmb_rows.T
    cls_ref[...] = cls

    @pl.when(pl.program_id(0) == 0)
    def _():
        cnt_ref[...] = jnp.zeros_like(cnt_ref)

    onehot = jnp.where(lax.broadcasted_iota(jnp.int32, (CLASS_ROWS, tm), 0) == cls, 1.0, 0.0)
    earlier = (lax.broadcasted_iota(jnp.int32, (tm, tm), 0) < lax.broadcasted_iota(jnp.int32, (tm, tm), 1))
    before = jnp.dot(onehot.astype(BF16), jnp.where(earlier, 1.0, 0.0).astype(BF16), preferred_element_type=F32)
    rank = jnp.sum(onehot * (before + cnt_ref[:, 0:1]), axis=0, keepdims=True)
    rank_ref[...] = rank.astype(jnp.int32)
    cnt_ref[...] = cnt_ref[...] + jnp.sum(onehot, axis=1, keepdims=True)
    count_ref[...] = cnt_ref[...]


def _outproj_call(yr, yh, ym, x, T, mod, w_out, l, ln_g, ln_b, w_router_t, b_router):
    n = x.shape[0]
    tm = WIDE_TILE if (mod.shape[0] == 1 or T % WIDE_TILE == 0) else ROW_TILE
    row = lambda w: pl.BlockSpec((tm, w), lambda i: (i, 0))
    full = lambda a: pl.BlockSpec(a.shape, lambda i: (0,) * a.ndim)
    return pl.pallas_call(
        _outproj_kernel,
        grid=(n // tm,),
        in_specs=[row(D_GROUP), row(D_GROUP), row(D_GROUP), row(D_MODEL), _mod_spec(mod, T, tm),
                  pl.BlockSpec((None, D_MIX, D_MODEL), lambda i: (l, 0, 0)),
                  full(ln_g), full(ln_b), full(w_router_t), full(b_router)],
        out_specs=[row(D_MODEL), pl.BlockSpec((tm, SLABS + 1, LANES), lambda i: (i, 0, 0)),
                   pl.BlockSpec((1, tm), lambda i: (0, i)), pl.BlockSpec((1, tm), lambda i: (0, i)),
                   pl.BlockSpec((CLASS_ROWS, LANES), lambda i: (0, 0))],
        out_shape=[jax.ShapeDtypeStruct((n, D_MODEL), F32),
                   jax.ShapeDtypeStruct((n, SLABS + 1, LANES), F32),
                   jax.ShapeDtypeStruct((1, n), jnp.int32), jax.ShapeDtypeStruct((1, n), jnp.int32),
                   jax.ShapeDtypeStruct((CLASS_ROWS, LANES), F32)],
        scratch_shapes=[pltpu.VMEM((CLASS_ROWS, LANES), F32)],
        compiler_params=_cparams(1),
        name="outproj_ln_router",
    )(yr, yh, ym, x, mod, w_out, ln_g, ln_b, w_router_t, b_router)


MOE_TILE = 256


def _row_gather(idx_ref, first, n_rows, src_hbm, dst_ref, sem):
    def body(p, carry):
        for k in range(2):
            r = 2 * p + k
            pltpu.make_async_copy(src_hbm.at[idx_ref[first + r]], dst_ref.at[:, r, :], sem).start(priority=k)
        return carry
    lax.fori_loop(0, n_rows // 2, body, 0, unroll=4)


def _row_gather_wait(n_rows, src_hbm, dst_ref, sem):
    def body(r, carry):
        pltpu.make_async_copy(src_hbm.at[0], dst_ref.at[:, 0, :], sem).wait()
        return carry
    lax.fori_loop(0, n_rows, body, 0, unroll=8)


def _gathered_tile(idx_ref, src_hbm, buf_ref, sem_ref, tm):
    i = pl.program_id(0)
    slot = lax.rem(i, 2)

    @pl.when(i == 0)
    def _():
        _row_gather(idx_ref, 0, tm, src_hbm, buf_ref.at[0], sem_ref.at[0])

    @pl.when(i + 1 < pl.num_programs(0))
    def _():
        _row_gather(idx_ref, (i + 1) * tm, tm, src_hbm, buf_ref.at[1 - slot], sem_ref.at[1 - slot])

    _row_gather_wait(tm, src_hbm, buf_ref.at[slot], sem_ref.at[slot])
    return buf_ref.at[slot]


def _moe_kernel(tile_group_ref, need_ref, src_ref, hx_hbm, wg_ref, wu_ref, wd_ref, o_ref, buf_ref, sem_ref,
                acc_ref):
    tm = o_ref.shape[0]
    i = pl.program_id(0)
    rows_ref = _gathered_tile(src_ref, hx_hbm, buf_ref, sem_ref, tm)
    hm = jnp.concatenate([rows_ref[j] for j in range(SLABS)], axis=1).astype(BF16)
    cmb = rows_ref[SLABS]
    lane = lax.broadcasted_iota(jnp.int32, cmb.shape, 1)
    first_expert = tile_group_ref[i] * EXPERTS_PER_GROUP
    acc_ref[...] = jnp.zeros_like(acc_ref)
    for k in range(EXPERTS_PER_GROUP):
        @pl.when(need_ref[i * EXPERTS_PER_GROUP + k] != 0)
        def _(k=k):
            ce = jnp.sum(jnp.where(lane == first_expert + k, cmb, 0.0), axis=1, keepdims=True)
            hg = jnp.dot(hm, wg_ref[k], preferred_element_type=F32)
            hu = jnp.dot(hm, wu_ref[k], preferred_element_type=F32)
            acc_ref[...] += jnp.dot((_silu(hg) * hu * ce).astype(BF16), wd_ref[k], preferred_element_type=F32)
    for j in range(SLABS):
        o_ref[:, j, :] = acc_ref[:, j * LANES:(j + 1) * LANES]


def _moe_call(hx, tile_group, need, src, w_gate, w_up, w_down, l):
    n_pad = src.shape[0]
    tm = MOE_TILE
    grp_w = lambda shape: pl.BlockSpec((None,) + shape, lambda i, tg, nd, sr: (l, tg[i], 0, 0))
    return pl.pallas_call(
        _moe_kernel,
        grid_spec=pltpu.PrefetchScalarGridSpec(
            num_scalar_prefetch=3,
            grid=(n_pad // tm,),
            in_specs=[pl.BlockSpec(memory_space=pl.ANY),
                      grp_w((EXPERTS_PER_GROUP, D_MODEL, D_FF)), grp_w((EXPERTS_PER_GROUP, D_MODEL, D_FF)),
                      grp_w((EXPERTS_PER_GROUP, D_FF, D_MODEL))],
            out_specs=pl.BlockSpec((tm, SLABS, LANES), lambda i, tg, nd, sr: (i, 0, 0)),
            scratch_shapes=[pltpu.VMEM((2, SLABS + 1, tm, LANES), F32), pltpu.SemaphoreType.DMA((2,)),
                            pltpu.VMEM((tm, D_MODEL), F32)]),
        out_shape=jax.ShapeDtypeStruct((n_pad, SLABS, LANES), F32),
        compiler_params=_cparams(1),
        name="moe_sorted",
    )(tile_group, need, src, hx, w_gate, w_up, w_down)


def _ln2_kernel(pos_ref, f_hbm, x1_ref, mod_ref, lg_ref, lb_ref, o_ref, buf_ref, sem_ref):
    tm = o_ref.shape[0]
    rows_ref = _gathered_tile(pos_ref, f_hbm, buf_ref, sem_ref, tm)
    f = jnp.concatenate([rows_ref[j] for j in range(SLABS)], axis=1)
    v = ALPHA * x1_ref[...] + mod_ref[GATE2:GATE2 + 1, :] * f
    o_ref[...] = _layer_norm(v, lg_ref[...], lb_ref[...])


def _ln2_call(f_sorted, pos, x1, T, mod, ln_g, ln_b):
    n = x1.shape[0]
    per_seq = mod.shape[0] > 1
    tm = WIDE_TILE if (not per_seq or T % WIDE_TILE == 0) else ROW_TILE
    return pl.pallas_call(
        _ln2_kernel,
        grid_spec=pltpu.PrefetchScalarGridSpec(
            num_scalar_prefetch=1,
            grid=(n // tm,),
            in_specs=[pl.BlockSpec(memory_space=pl.ANY),
                      pl.BlockSpec((tm, D_MODEL), lambda i, ps: (i, 0)),
                      pl.BlockSpec((None, 6, D_MODEL), lambda i, ps: ((i * tm) // T if per_seq else 0, 0, 0)),
                      pl.BlockSpec((1, D_MODEL), lambda i, ps: (0, 0)),
                      pl.BlockSpec((1, D_MODEL), lambda i, ps: (0, 0))],
            out_specs=pl.BlockSpec((tm, D_MODEL), lambda i, ps: (i, 0)),
            scratch_shapes=[pltpu.VMEM((2, SLABS, tm, LANES), F32), pltpu.SemaphoreType.DMA((2,))]),
        out_shape=jax.ShapeDtypeStruct((n, D_MODEL), F32),
        compiler_params=_cparams(1),
        name="moe_combine_ln",
    )(pos, f_sorted, x1, mod, ln_g, ln_b)


def _dispatch_plan(cls, rank, class_count, tm):
    n = cls.shape[0]
    n_pairs = len(EXPERT_PAIRS)
    n_pad = n + N_GROUPS * tm
    i32 = jnp.int32
    count = class_count.reshape(N_GROUPS, n_pairs)
    padded = ((jnp.sum(count, axis=1) + tm - 1) // tm) * tm
    ends = jnp.cumsum(padded)
    class_start = ((ends - padded)[:, None] + jnp.cumsum(count, axis=1) - count).reshape(N_CLASSES)
    pos = class_start[cls] + rank
    src = jnp.zeros((n_pad,), i32).at[pos].set(jnp.arange(n, dtype=i32))
    tile_start = jnp.arange(n_pad // tm, dtype=i32) * tm
    tile_group = jnp.minimum(jnp.sum((tile_start[:, None] >= ends[None, :]).astype(i32), axis=1), N_GROUPS - 1)
    class_end = class_start + count.reshape(N_CLASSES)
    overlap = ((class_start[None, :] < tile_start[:, None] + tm) & (class_end[None, :] > tile_start[:, None])
               & (class_end > class_start)[None, :]).astype(i32)
    pair_has = jnp.array([[int(k in p) for k in range(EXPERTS_PER_GROUP)] for p in EXPERT_PAIRS], i32)
    need = jnp.minimum(overlap @ jnp.tile(pair_has, (N_GROUPS, 1)), 1).reshape(-1)
    return pos, src, tile_group, need


def _grid_pos_embed(n_tokens):
    rows = n_tokens // GRID_W
    r = jnp.repeat(jnp.arange(rows, dtype=F32), GRID_W)
    c = jnp.tile(jnp.arange(GRID_W, dtype=F32), rows)
    q = D_MODEL // 4
    freq = jnp.exp(-jnp.log(10000.0) * jnp.arange(q, dtype=F32) / q)
    ar = r[:, None] * freq
    ac = c[:, None] * freq
    return jnp.concatenate([jnp.sin(ar), jnp.cos(ar), jnp.sin(ac), jnp.cos(ac)], axis=-1)


def _lru_gate_params(wa, ba, wx, bx, lam):
    ncb = D_GROUP // LANES

    def dense(w):
        z = jnp.zeros((ncb, LANES, LANES), F32)
        z = z.at[:, :LRU_BLOCK, :LRU_BLOCK].set(w[0::2])
        return z.at[:, LRU_BLOCK:, LRU_BLOCK:].set(w[1::2])

    wg = jnp.concatenate([dense(wa[0]), dense(wx[0]), dense(wa[1]), dense(wx[1])], axis=-1).astype(BF16)
    per_blk = lambda v: v.reshape(ncb, 1, LANES)
    bg = jnp.concatenate([per_blk(ba[0]), per_blk(bx[0]), per_blk(ba[1]), per_blk(bx[1])], axis=-1)
    lm = jnp.concatenate([per_blk(lam[0]), per_blk(lam[1])], axis=-1)
    return wg, bg, lm


def _mixer_states(B, l, states):
    if states is None:
        return (jnp.zeros((2, B, D_GROUP), F32),
                jnp.zeros((B, 2, N_HEADS, D_HEAD, D_HEAD), F32),
                jnp.zeros((B, 2, N_HEADS, D_HEAD, D_HEAD), F32),
                jnp.zeros((B, 2, N_HEADS, D_HEAD), F32),
                jnp.zeros((B, 2, N_HEADS, D_HEAD), F32))
    h, s, c, n, m = states
    return (jnp.swapaxes(h[:, l], 0, 1), s[:, l], c[:, l], n[:, l],
            jnp.broadcast_to(m[:, l][..., None], (B, 2, N_HEADS, D_HEAD)))


def kernel(x_prompt, x_sample, state_lru_h, state_hgrn_S, state_mlstm_C, state_mlstm_n, state_mlstm_m,
           c, c_ctx, w_ada, b_ada, w_in, b_in, conv_w, conv_b, lru_wa, lru_ba, lru_wx, lru_bx, lru_lam,
           hg_lb, hg_norm_g, ml_norm_g, w_out, ln1_g, ln1_b, ln2_g, ln2_b,
           w_router, b_router, w_gate, w_up, w_down):
    Bp, Tp, _ = x_prompt.shape
    Bs, Ts, _ = x_sample.shape
    assert Bs % SUBLANES == 0 and Bp % SUBLANES == 0 and Bs + 1 <= 16
    assert Tp % ROW_TILE == 0 and Ts % ROW_TILE == 0

    lb_soft = jax.nn.softmax(hg_lb.astype(F32), axis=0)
    hg_lower = jnp.cumsum(lb_soft, axis=0) - lb_soft[0:1]

    c16 = jnp.concatenate([c, c_ctx[None], jnp.zeros((16 - Bs - 1, D_MODEL), F32)], axis=0)
    mod = _ada_call(c16, w_ada, b_ada)

    w_in_p = jnp.pad(w_in.astype(BF16), ((0, 0), (0, 0), (0, D_IN_PAD - D_IN)))
    b_in_p = jnp.pad(b_in, ((0, 0), (0, D_IN_PAD - D_IN))).reshape(DEPTH, 1, D_IN_PAD)
    w_out_b = w_out.astype(BF16)
    w_gate_b, w_up_b, w_down_b = w_gate.astype(BF16), w_up.astype(BF16), w_down.astype(BF16)
    w_router_t = w_router.T
    b_router_c = b_router.reshape(N_EXPERTS, 1)

    xs = x_sample + _grid_pos_embed(Ts).astype(x_sample.dtype)
    streams = [
        dict(x=x_prompt.reshape(Bp * Tp, D_MODEL), T=Tp, B=Bp, states=None),
        dict(x=xs.reshape(Bs * Ts, D_MODEL), T=Ts, B=Bs,
             states=(state_lru_h, state_hgrn_S, state_mlstm_C, state_mlstm_n, state_mlstm_m)),
    ]
    finals = []
    for l in range(DEPTH):
        wg, bg, lm = _lru_gate_params(lru_wa[l], lru_ba[l], lru_wx[l], lru_bx[l], lru_lam[l])
        row = lambda v: v[l].reshape(1, -1)
        for si, st in enumerate(streams):
            T, B = st["T"], st["B"]
            mod_s = (mod[l, :Bs] if si == 1 else mod[l, Bs:Bs + 1]).reshape(-1, 6, D_MODEL)
            h0, s0, c0, n0, m0 = _mixer_states(B, l, st["states"])
            z = _inproj_call(st["x"], T, mod_s, w_in_p, b_in_p, l)
            y_r, h_T = _rglru_call(z.reshape(B, T, D_IN_PAD), h0, conv_w[l], row(conv_b), wg, bg, lm)
            y_h, s_T = _hgrn_call(z, B, s0, hg_lower[l], row(hg_norm_g))
            y_m, c_T, n_T, m_T = _mlstm_call(z, B, c0, n0, m0, row(ml_norm_g))
            x1, hx, cls, rank, cnt = _outproj_call(y_r.reshape(B * T, D_GROUP), y_h, y_m, st["x"], T, mod_s,
                                                   w_out_b, l, row(ln1_g), row(ln1_b), w_router_t, b_router_c)
            pos, src, tile_group, need = _dispatch_plan(cls[0], rank[0], cnt[:N_CLASSES, 0].astype(jnp.int32),
                                                        MOE_TILE)
            f_sorted = _moe_call(hx, tile_group, need, src, w_gate_b, w_up_b, w_down_b, l)
            st["x"] = _ln2_call(f_sorted, pos, x1, T, mod_s, row(ln2_g), row(ln2_b))
            if si == 0:
                finals.append((jnp.swapaxes(h_T, 0, 1), s_T, c_T, n_T, m_T[..., 0]))
    outs = [st["x"].reshape(st["B"], st["T"], D_MODEL) for st in streams]
    stack = lambda i: jnp.stack([f[i] for f in finals], axis=1)
    return (outs[0], outs[1], stack(0), stack(1), stack(2), stack(3), stack(4))
```

```python
import functools

import jax
import jax.numpy as jnp
from jax import lax
from jax.experimental import pallas as pl
from jax.experimental.pallas import tpu as pltpu

F32 = jnp.float32
BF16 = jnp.bfloat16
HIGHEST = lax.Precision.HIGHEST

D_MODEL = 1024
DEPTH = 2
GRID_W = 64
D_GROUP = 512
D_MIX = 3 * D_GROUP
LRU_BLOCKS = 8
LRU_BLOCK = D_GROUP // LRU_BLOCKS
LRU_C = 8.0
N_HEADS = 4
D_HEAD = D_GROUP // N_HEADS
N_EXPERTS = 16
N_GROUPS = 4
EXPERTS_PER_GROUP = N_EXPERTS // N_GROUPS
D_FF = 512
ALPHA = (2.0 * DEPTH) ** 0.25
EPS = 1e-5
TINY = 1e-30
N_GATE = 4 * N_HEADS
D_IN = 11 * D_GROUP + N_GATE
D_IN_PAD = 45 * 128
GATE_BLK = (11 * D_GROUP) // 128

SUBLANES = 8
LANES = 128
VMEM_LIMIT = 56 * 1024 * 1024

LRU_TC = 32
LRU_PAD = SUBLANES
HG_L = 128
HG_SUB = 16
ML_L = 128
ML_SUB = 32


def _cparams(n_axes):
    return pltpu.CompilerParams(dimension_semantics=("arbitrary",) * n_axes,
                                vmem_limit_bytes=VMEM_LIMIT)


def _dot(a, b):
    return jnp.dot(a.astype(BF16), b.astype(BF16), preferred_element_type=F32)


def _dot_nt(a, b):
    return lax.dot_general(a.astype(BF16), b.astype(BF16), (((1,), (1,)), ((), ())),
                           preferred_element_type=F32)


def _dot_tn(a, b):
    return lax.dot_general(a.astype(BF16), b.astype(BF16), (((0,), (0,)), ((), ())),
                           preferred_element_type=F32)


def _dot_f32(a, b):
    return jnp.dot(a, b, precision=HIGHEST, preferred_element_type=F32)


def _dot_split3(mask, x, mask_on_left):
    m = jnp.where(mask, 1.0, 0.0).astype(BF16)
    hi = x.astype(BF16)
    r1 = x - hi.astype(F32)
    mid = r1.astype(BF16)
    lo = (r1 - mid.astype(F32)).astype(BF16)
    mm = (lambda p: jnp.dot(m, p, preferred_element_type=F32)) if mask_on_left else (
        lambda p: jnp.dot(p, m, preferred_element_type=F32))
    return mm(hi) + mm(mid) + mm(lo)


def _softplus(x):
    return jnp.maximum(x, 0.0) + jnp.log1p(jnp.exp(-jnp.abs(x)))


def _silu(x):
    return x * jax.nn.sigmoid(x)


def _tri_masks(n):
    r = lax.broadcasted_iota(jnp.int32, (n, n), 0)
    c = lax.broadcasted_iota(jnp.int32, (n, n), 1)
    return r >= c, r <= c


def _ada_kernel(c_ref, w_ref, b_ref, o_ref):
    o_ref[0] = _dot_f32(_silu(c_ref[...]), w_ref[0]) + b_ref[0]


def _ada_call(c16, w_ada, b_ada):
    tn = 1536
    return pl.pallas_call(
        _ada_kernel,
        grid=(DEPTH, 6 * D_MODEL // tn),
        in_specs=[pl.BlockSpec((16, D_MODEL), lambda l, j: (0, 0)),
                  pl.BlockSpec((1, D_MODEL, tn), lambda l, j: (l, 0, j)),
                  pl.BlockSpec((1, 1, tn), lambda l, j: (l, 0, j))],
        out_specs=pl.BlockSpec((1, 16, tn), lambda l, j: (l, 0, j)),
        out_shape=jax.ShapeDtypeStruct((DEPTH, 16, 6 * D_MODEL), F32),
        compiler_params=_cparams(2),
        name="adaln",
    )(c16, w_ada, b_ada.reshape(DEPTH, 1, 6 * D_MODEL))


INPROJ_CW = 640
ROW_TILE = 256
WIDE_TILE = 512

SHIFT1, SCALE1, GATE1, SHIFT2, SCALE2, GATE2 = range(6)


def _mod_spec(mod, T, tm, n_grid_axes=1):
    per_seq = mod.shape[0] > 1
    if n_grid_axes == 1:
        return pl.BlockSpec((None, 6, D_MODEL), lambda i: ((i * tm) // T if per_seq else 0, 0, 0))
    return pl.BlockSpec((None, 6, D_MODEL), lambda i, e: ((i * tm) // T if per_seq else 0, 0, 0))


def _inproj_kernel(x_ref, mod_ref, w_ref, b_ref, z_ref):
    m = mod_ref[...]
    hb = (x_ref[...] * (1.0 + m[SCALE1:SCALE1 + 1]) + m[SHIFT1:SHIFT1 + 1]).astype(BF16)
    for j in range(D_IN_PAD // INPROJ_CW):
        cs = slice(j * INPROJ_CW, (j + 1) * INPROJ_CW)
        z_ref[:, cs] = jnp.dot(hb, w_ref[:, cs], preferred_element_type=F32) + b_ref[:, cs]


def _inproj_call(x, T, mod, w, b, l):
    n = x.shape[0]
    tm = WIDE_TILE if (mod.shape[0] == 1 or T % WIDE_TILE == 0) else ROW_TILE
    return pl.pallas_call(
        _inproj_kernel,
        grid=(n // tm,),
        in_specs=[pl.BlockSpec((tm, D_MODEL), lambda i: (i, 0)),
                  _mod_spec(mod, T, tm),
                  pl.BlockSpec((None, D_MODEL, D_IN_PAD), lambda i: (l, 0, 0), pipeline_mode=pl.Buffered(1)),
                  pl.BlockSpec((None, 1, D_IN_PAD), lambda i: (l, 0, 0))],
        out_specs=pl.BlockSpec((tm, D_IN_PAD), lambda i: (i, 0)),
        out_shape=jax.ShapeDtypeStruct((n, D_IN_PAD), F32),
        compiler_params=_cparams(1),
        name="inproj",
    )(x, mod, w, b)


def _rglru_kernel(xr_ref, gr_ref, cw_ref, cb_ref, wg_ref, bg_ref, lam_ref, h0_ref,
                  y_ref, hT_ref, xs_ref, os_ref, ob_ref):
    T = xr_ref.shape[1]
    P = T + LRU_PAD
    TC = LRU_TC
    nch = T // TC
    CP = 256
    zeros = jnp.zeros((LRU_PAD, LANES), F32)
    for s in range(SUBLANES + 1):
        xs_ref[s * P:s * P + LRU_PAD] = zeros
    for s in range(SUBLANES):
        def copy_in(c, carry, s=s):
            r0 = pl.multiple_of(c * CP, CP)
            xs_ref[pl.ds(s * P + LRU_PAD + r0, CP)] = xr_ref[s, pl.ds(r0, CP)]
            return carry
        lax.fori_loop(0, T // CP, copy_in, 0)

    cw = cw_ref[...]
    cb = cb_ref[...]
    sp = _softplus(-lam_ref[0])

    def gates(t0, d):
        win = [xs_ref[pl.ds(t0 + k + LRU_PAD - 2, SUBLANES, stride=P)] for k in range(TC + 3)]
        xc = jnp.stack([cw[0:1] * win[i] + cw[1:2] * win[i + 1] + cw[2:3] * win[i + 2] + cw[3:4] * win[i + 3] + cb
                        for i in range(TC)])
        x2 = xc.reshape(TC * SUBLANES, LANES)
        g = (jnp.dot(x2.astype(BF16), wg_ref[0, :, d * 256:(d + 1) * 256], preferred_element_type=F32)
             + bg_ref[0, :, d * 256:(d + 1) * 256])
        r = jax.nn.sigmoid(g[:, :LANES])
        i = jax.nn.sigmoid(g[:, LANES:])
        log_a = -LRU_C * r * sp[:, d * LANES:(d + 1) * LANES]
        a = jnp.exp(log_a)
        u = jnp.sqrt(jnp.maximum(1.0 - a * a, 0.0)) * (i * x2)
        return a.reshape(TC, SUBLANES, LANES), u.reshape(TC, SUBLANES, LANES)

    def step(c, carry):
        hf, hb = carry
        tf = pl.multiple_of(c * TC, TC)
        tb = pl.multiple_of((nch - 1 - c) * TC, TC)
        af, uf = gates(tf, 0)
        ab, ub = gates(tb, 1)
        for i in range(TC):
            hf = af[i] * hf + uf[i]
            os_ref[pl.ds(tf + i, SUBLANES, stride=P)] = hf
            k = TC - 1 - i
            hb = ab[k] * hb + ub[k]
            ob_ref[pl.ds(tb + k, SUBLANES, stride=P)] = hb
        return hf, hb

    hT_ref[0], hT_ref[1] = lax.fori_loop(0, nch, step, (h0_ref[0], h0_ref[1]))

    for s in range(SUBLANES):
        def copy_out(c, carry, s=s):
            r0 = pl.multiple_of(c * CP, CP)
            o = os_ref[pl.ds(s * P + r0, CP)] + ob_ref[pl.ds(s * P + r0, CP)]
            y_ref[s, pl.ds(r0, CP)] = jax.nn.gelu(gr_ref[s, pl.ds(r0, CP)]) * o
            return carry
        lax.fori_loop(0, T // CP, copy_out, 0)


def _rglru_call(z3, h0, cw, cb, wg, bg, lam):
    B, T, _ = z3.shape
    nb = B // SUBLANES
    ncb = D_GROUP // LANES
    blk = (SUBLANES, T, LANES)
    pitch = T + LRU_PAD
    return pl.pallas_call(
        _rglru_kernel,
        grid=(nb, ncb),
        in_specs=[pl.BlockSpec(blk, lambda b, c: (b, 0, c)),
                  pl.BlockSpec(blk, lambda b, c: (b, 0, ncb + c)),
                  pl.BlockSpec((4, LANES), lambda b, c: (0, c)),
                  pl.BlockSpec((1, LANES), lambda b, c: (0, c)),
                  pl.BlockSpec((1, LANES, 4 * LANES), lambda b, c: (c, 0, 0)),
                  pl.BlockSpec((1, 1, 4 * LANES), lambda b, c: (c, 0, 0)),
                  pl.BlockSpec((1, 1, 2 * LANES), lambda b, c: (c, 0, 0)),
                  pl.BlockSpec((2, SUBLANES, LANES), lambda b, c: (0, b, c))],
        out_specs=[pl.BlockSpec(blk, lambda b, c: (b, 0, c)),
                   pl.BlockSpec((2, SUBLANES, LANES), lambda b, c: (0, b, c))],
        out_shape=[jax.ShapeDtypeStruct((B, T, D_GROUP), F32),
                   jax.ShapeDtypeStruct((2, B, D_GROUP), F32)],
        scratch_shapes=[pltpu.VMEM((SUBLANES * pitch + LRU_PAD, LANES), F32),
                        pltpu.VMEM((SUBLANES * pitch, LANES), F32),
                        pltpu.VMEM((SUBLANES * pitch, LANES), F32)],
        compiler_params=_cparams(2),
        name="rglru",
    )(z3, z3, cw, cb, wg, bg, lam, h0)


def _hgrn_kernel(q_ref, ff_ref, fb_ref, v_ref, og_ref, lb_ref, ng_ref, s0_ref,
                 y_ref, sT_ref, st_ref, of_ref, ob_ref):
    T = q_ref.shape[0]
    L = HG_L
    SUB = HG_SUB
    nch = T // L
    nsub = L // SUB
    tril, triu = _tri_masks(L)
    heads = [slice(hd * D_HEAD, (hd + 1) * D_HEAD) for hd in range(N_HEADS)]

    for d in range(2):
        for hd in range(N_HEADS):
            st_ref[d, hd] = s0_ref[d, hd].T

    def step(c, carry):
        jobs = ((0, pl.multiple_of(c * L, L), of_ref), (1, pl.multiple_of((nch - 1 - c) * L, L), ob_ref))
        gated = []
        for d, t0, _ in jobs:
            q = _silu(q_ref[pl.ds(t0, L), :])
            v = v_ref[pl.ds(t0, L), :].astype(BF16)
            lb = lb_ref[d:d + 1, :]
            sig = jax.nn.sigmoid((ff_ref if d == 0 else fb_ref)[pl.ds(t0, L), :])
            log_f = jnp.log(jnp.maximum(lb + (1.0 - lb) * sig, TINY))
            kk = (1.0 - lb) * (1.0 - sig)
            mask = tril if d == 0 else triu
            b = _dot_split3(mask, log_f, mask_on_left=True)
            gated.append((q, v, kk, b, mask))
        factored = []
        for (d, _, _), (q, v, kk, b, mask) in zip(jobs, gated):
            b_tot = b[L - 1:L] if d == 0 else b[0:1]
            subs = []
            for i in range(nsub):
                rows = slice(i * SUB, (i + 1) * SUB)
                if d == 0:
                    cols = slice(0, (i + 1) * SUB)
                    edge = b[i * SUB - 1:i * SUB] if i > 0 else 0.0
                else:
                    cols = slice(i * SUB, L)
                    edge = b[(i + 1) * SUB:(i + 1) * SUB + 1] if i < nsub - 1 else 0.0
                subs.append((rows, cols, (q[rows] * jnp.exp(b[rows] - edge)).astype(BF16),
                             (kk[cols] * jnp.exp(edge - b[cols])).astype(BF16)))
            factored.append(((q * jnp.exp(b)).astype(BF16), (kk * jnp.exp(b_tot - b)).astype(BF16),
                             jnp.exp(b_tot), subs))
        scores = [[[_dot_nt(qs[:, hs], ks[:, hs]) for (_, _, qs, ks) in subs] for hs in heads]
                  for (_, _, _, subs) in factored]
        states = [[st_ref[d, hd] for hd in range(N_HEADS)] for d, _, _ in jobs]
        inter = [[_dot_nt(qdec[:, hs], states[j][hd]) for hd, hs in enumerate(heads)]
                 for j, (qdec, _, _, _) in enumerate(factored)]
        update = [[_dot_tn(gated[j][1][:, hs], kdec[:, hs]) for hs in heads]
                  for j, (_, kdec, _, _) in enumerate(factored)]
        for j, (d, t0, o_ref) in enumerate(jobs):
            v, mask = gated[j][1], gated[j][4]
            dec, subs = factored[j][2], factored[j][3]
            for hd, hs in enumerate(heads):
                intra = [_dot(jnp.where(mask[rows, cols], scores[j][hd][i], 0.0), v[cols, hs])
                         for i, (rows, cols, _, _) in enumerate(subs)]
                o_ref[pl.ds(t0, L), hs] = jnp.concatenate(intra, axis=0) + inter[j][hd]
                st_ref[d, hd] = states[j][hd] * dec[:, hs] + update[j][hd]
        return carry

    lax.fori_loop(0, nch, step, 0)

    def combine(c, carry):
        t0 = pl.multiple_of(c * L, L)
        for hs in heads:
            o = of_ref[pl.ds(t0, L), hs] + ob_ref[pl.ds(t0, L), hs]
            o = o * lax.rsqrt(jnp.mean(jnp.square(o), -1, keepdims=True) + EPS)
            y_ref[pl.ds(t0, L), hs] = o * ng_ref[:, hs] * _silu(og_ref[pl.ds(t0, L), hs])
        return carry

    lax.fori_loop(0, nch, combine, 0)

    for d in range(2):
        for hd in range(N_HEADS):
            sT_ref[d, hd] = st_ref[d, hd].T


def _seq_col_spec(T, j):
    return pl.BlockSpec((T, D_GROUP), lambda b, j=j: (b, j))


def _hgrn_call(z2, B, s0, lower, norm_g):
    T = z2.shape[0] // B
    col = functools.partial(_seq_col_spec, T)
    st_spec = pl.BlockSpec((None, 2, N_HEADS, D_HEAD, D_HEAD), lambda b: (b, 0, 0, 0, 0))
    return pl.pallas_call(
        _hgrn_kernel,
        grid=(B,),
        in_specs=[col(2), col(3), col(4), col(5), col(6),
                  pl.BlockSpec((2, D_GROUP), lambda b: (0, 0)),
                  pl.BlockSpec((1, D_GROUP), lambda b: (0, 0)),
                  st_spec],
        out_specs=[pl.BlockSpec((T, D_GROUP), lambda b: (b, 0)), st_spec],
        out_shape=[jax.ShapeDtypeStruct((B * T, D_GROUP), F32),
                   jax.ShapeDtypeStruct((B, 2, N_HEADS, D_HEAD, D_HEAD), F32)],
        scratch_shapes=[pltpu.VMEM((2, N_HEADS, D_HEAD, D_HEAD), F32),
                        pltpu.VMEM((T, D_GROUP), F32),
                        pltpu.VMEM((T, D_GROUP), F32)],
        compiler_params=_cparams(1),
        name="hgrn2",
    )(z2, z2, z2, z2, z2, lower, norm_g, s0)


def _mlstm_kernel(q_ref, k_ref, v_ref, og_ref, g_ref, ng_ref, c0_ref, n0_ref, m0_ref,
                  y_ref, cT_ref, nT_ref, mT_ref, cn_ref, of_ref, ob_ref):
    T = q_ref.shape[0]
    L = ML_L
    SUB = ML_SUB
    nch = T // L
    nsub = L // SUB
    tril, triu = _tri_masks(L)
    gcol = lax.broadcasted_iota(jnp.int32, (L, N_GATE), 1)
    heads = [slice(hd * D_HEAD, (hd + 1) * D_HEAD) for hd in range(N_HEADS)]
    ones = jnp.ones((L, D_HEAD), BF16)

    for d in range(2):
        for hd in range(N_HEADS):
            cn_ref[d, hd, :, 0:D_HEAD] = c0_ref[d, hd]
            cn_ref[d, hd, :, D_HEAD:] = jnp.broadcast_to(n0_ref[d, hd:hd + 1, :], (D_HEAD, D_HEAD)).T
    mT_ref[...] = m0_ref[...]

    def step(c, carry):
        jobs = ((0, pl.multiple_of(c * L, L), of_ref), (1, pl.multiple_of((nch - 1 - c) * L, L), ob_ref))
        loaded, q_state = [], []
        for d, t0, _ in jobs:
            g = g_ref[pl.ds(t0, L), 0:N_GATE]
            g = jnp.where(gcol % 8 >= N_HEADS, jax.nn.log_sigmoid(g), g)
            bc = _dot_split3(tril if d == 0 else triu, g, mask_on_left=True)
            qb = q_ref[pl.ds(t0, L), :].astype(BF16)
            kf = k_ref[pl.ds(t0, L), :] * (D_HEAD ** -0.5)
            vb = v_ref[pl.ds(t0, L), :].astype(BF16)
            v1 = [jnp.concatenate([vb[:, hs], ones], axis=1) for hs in heads]
            loaded.append((g, bc, qb, kf, v1))
            q_state.append([_dot(qb[:, hs], cn_ref[d, hd]) for hd, hs in enumerate(heads)])
        scaled = []
        for j, (d, _, _) in enumerate(jobs):
            g, bc, qb, kf, v1 = loaded[j]
            per_head = []
            for hd, hs in enumerate(heads):
                ic, fc = d * 8 + hd, d * 8 + N_HEADS + hd
                b_col = bc[:, fc:fc + 1]
                r_col = g[:, ic:ic + 1] - b_col
                b_tot = b_col[L - 1:L] if d == 0 else b_col[0:1]
                m_st = mT_ref[d, hd:hd + 1, 0:1]
                blk_max = [jnp.max(r_col[i * SUB:(i + 1) * SUB], axis=0, keepdims=True) for i in range(nsub)]
                r_rep = jnp.broadcast_to(r_col, (L, D_HEAD))
                subs = []
                for i in range(nsub):
                    rows = slice(i * SUB, (i + 1) * SUB)
                    cols = slice(0, (i + 1) * SUB) if d == 0 else slice(i * SUB, L)
                    seen = blk_max[:i + 1] if d == 0 else blk_max[i:]
                    mu = jnp.maximum(m_st, functools.reduce(jnp.maximum, seen))
                    kp = (kf[cols, hs] * jnp.exp(r_rep[cols] - mu)).astype(BF16)
                    floor = jnp.exp(-b_col[rows] - mu)
                    subs.append((rows, cols, mu, kp, floor))
                per_head.append((b_tot, m_st, subs))
            scaled.append(per_head)
        scores = [[[_dot_nt(loaded[j][2][rows, hs], kp) for (rows, _, _, kp, _) in scaled[j][hd][2]]
                   for hd, hs in enumerate(heads)] for j in range(2)]
        full = [nsub - 1, 0]
        update = [[_dot_tn(scaled[j][hd][2][full[j]][3], loaded[j][4][hd]) for hd in range(N_HEADS)]
                  for j in range(2)]
        for j, (d, t0, o_ref) in enumerate(jobs):
            mask = tril if d == 0 else triu
            for hd, hs in enumerate(heads):
                b_tot, m_st, subs = scaled[j][hd]
                outs = []
                for i, (rows, cols, mu, _, floor) in enumerate(subs):
                    w = jnp.where(mask[rows, cols], scores[j][hd][i], 0.0)
                    nd = _dot(w, loaded[j][4][hd][cols]) + jnp.exp(m_st - mu) * q_state[j][hd][rows]
                    outs.append(nd[:, 0:D_HEAD] / jnp.maximum(jnp.abs(nd[:, D_HEAD:]), floor))
                o_ref[pl.ds(t0, L), hs] = jnp.concatenate(outs, axis=0)
                mu_full = subs[full[j]][2]
                cn_ref[d, hd] = jnp.exp(m_st - mu_full) * cn_ref[d, hd] + update[j][hd]
                mT_ref[d, hd:hd + 1, :] = jnp.broadcast_to(b_tot + mu_full, (1, LANES))
        return carry

    lax.fori_loop(0, nch, step, 0)

    for d in range(2):
        for hd in range(N_HEADS):
            cT_ref[d, hd] = cn_ref[d, hd, :, 0:D_HEAD]
            nT_ref[d, hd:hd + 1, :] = cn_ref[d, hd, :, D_HEAD:].T[0:1, :]

    def combine(c, carry):
        t0 = pl.multiple_of(c * L, L)
        for hs in heads:
            o = of_ref[pl.ds(t0, L), hs] + ob_ref[pl.ds(t0, L), hs]
            mu = jnp.mean(o, -1, keepdims=True)
            var = jnp.mean(jnp.square(o - mu), -1, keepdims=True)
            o = (o - mu) * lax.rsqrt(var + EPS)
            y_ref[pl.ds(t0, L), hs] = jax.nn.sigmoid(og_ref[pl.ds(t0, L), hs]) * (o * ng_ref[:, hs])
        return carry

    lax.fori_loop(0, nch, combine, 0)


def _mlstm_call(z2, B, c0, n0, m0, norm_g):
    T = z2.shape[0] // B
    col = functools.partial(_seq_col_spec, T)
    c_spec = pl.BlockSpec((None, 2, N_HEADS, D_HEAD, D_HEAD), lambda b: (b, 0, 0, 0, 0))
    v_spec = pl.BlockSpec((None, 2, N_HEADS, D_HEAD), lambda b: (b, 0, 0, 0))
    return pl.pallas_call(
        _mlstm_kernel,
        grid=(B,),
        in_specs=[col(7), col(8), col(9), col(10),
                  pl.BlockSpec((T, LANES), lambda b: (b, GATE_BLK)),
                  pl.BlockSpec((1, D_GROUP), lambda b: (0, 0)),
                  c_spec, v_spec, v_spec],
        out_specs=[pl.BlockSpec((T, D_GROUP), lambda b: (b, 0)), c_spec, v_spec, v_spec],
        out_shape=[jax.ShapeDtypeStruct((B * T, D_GROUP), F32),
                   jax.ShapeDtypeStruct((B, 2, N_HEADS, D_HEAD, D_HEAD), F32),
                   jax.ShapeDtypeStruct((B, 2, N_HEADS, D_HEAD), F32),
                   jax.ShapeDtypeStruct((B, 2, N_HEADS, D_HEAD), F32)],
        scratch_shapes=[pltpu.VMEM((2, N_HEADS, D_HEAD, 2 * D_HEAD), F32),
                        pltpu.VMEM((T, D_GROUP), F32), pltpu.VMEM((T, D_GROUP), F32)],
        compiler_params=_cparams(1),
        name="mlstm",
    )(z2, z2, z2, z2, z2, norm_g, c0, n0, m0)


def _layer_norm(v, g, b):
    mu = jnp.mean(v, -1, keepdims=True)
    var = jnp.mean(jnp.square(v - mu), -1, keepdims=True)
    return (v - mu) * lax.rsqrt(var + EPS) * g + b


EXPERT_PAIRS = ((0, 1), (0, 2), (1, 2), (1, 3), (0, 3), (2, 3))
N_CLASSES = N_GROUPS * len(EXPERT_PAIRS)
CLASS_ROWS = 32


def _route(logits_t, b_router):
    mx = jnp.max(logits_t, axis=0, keepdims=True)
    ex = jnp.exp(logits_t - mx)
    scores = ex / jnp.sum(ex, axis=0, keepdims=True)
    sel = scores + b_router
    rows = [sel[e:e + 1] for e in range(N_EXPERTS)]
    first, second, grp = [], [], []
    for gi in range(N_GROUPS):
        r = rows[gi * EXPERTS_PER_GROUP:(gi + 1) * EXPERTS_PER_GROUP]
        m1 = functools.reduce(jnp.maximum, r)
        taken = jnp.zeros_like(m1, dtype=jnp.bool_)
        f = []
        for x in r:
            hit = jnp.logical_and(x == m1, jnp.logical_not(taken))
            f.append(hit)
            taken = jnp.logical_or(taken, hit)
        rest = [jnp.where(fi, -jnp.inf, x) for fi, x in zip(f, r)]
        m2 = functools.reduce(jnp.maximum, rest)
        taken = jnp.zeros_like(m1, dtype=jnp.bool_)
        s = []
        for x in rest:
            hit = jnp.logical_and(x == m2, jnp.logical_not(taken))
            s.append(hit)
            taken = jnp.logical_or(taken, hit)
        first.append(f)
        second.append(s)
        grp.append(m1 + m2)
    gmax = functools.reduce(jnp.maximum, grp)
    taken = jnp.zeros_like(gmax, dtype=jnp.bool_)
    chosen = []
    cls = jnp.zeros_like(gmax, dtype=jnp.int32)
    for gi in range(N_GROUPS):
        best = jnp.logical_and(grp[gi] == gmax, jnp.logical_not(taken))
        taken = jnp.logical_or(taken, best)
        in_grp = [jnp.logical_and(best, jnp.logical_or(first[gi][j], second[gi][j]))
                  for j in range(EXPERTS_PER_GROUP)]
        chosen.extend(in_grp)
        for p, (a, b) in enumerate(EXPERT_PAIRS):
            cls = jnp.where(jnp.logical_and(in_grp[a], in_grp[b]), gi * len(EXPERT_PAIRS) + p, cls)
    picked = [jnp.where(ch, scores[e:e + 1], 0.0) for e, ch in enumerate(chosen)]
    denom = functools.reduce(lambda a, b: a + b, picked)
    return jnp.concatenate([p / denom for p in picked], axis=0), cls


SLABS = D_MODEL // LANES


def _outproj_kernel(yr_ref, yh_ref, ym_ref, x_ref, mod_ref, w_ref, lg_ref, lb_ref,
                    wr_ref, br_ref, x1_ref, hx_ref, cls_ref, rank_ref, count_ref, cnt_ref):
    tm = x_ref.shape[0]
    m = mod_ref[...]
    y = (jnp.dot(yr_ref[...].astype(BF16), w_ref[0:D_GROUP], preferred_element_type=F32)
         + jnp.dot(yh_ref[...].astype(BF16), w_ref[D_GROUP:2 * D_GROUP], preferred_element_type=F32)
         + jnp.dot(ym_ref[...].astype(BF16), w_ref[2 * D_GROUP:], preferred_element_type=F32))
    x1 = _layer_norm(ALPHA * x_ref[...] + m[GATE1:GATE1 + 1] * y, lg_ref[...], lb_ref[...])
    x1_ref[...] = x1
    hm = x1 * (1.0 + m[SCALE2:SCALE2 + 1]) + m[SHIFT2:SHIFT2 + 1]
    for j in range(SLABS):
        hx_ref[:, j, :] = hm[:, j * LANES:(j + 1) * LANES]
    logits_t = lax.dot_general(wr_ref[...], hm, (((1,), (1,)), ((), ())), precision=HIGHEST,
                               preferred_element_type=F32)
    cmb_t, cls = _route(logits_t, br_ref[...])
    cmb_rows = jnp.concatenate([cmb_t, jnp.zeros((LANES - N_EXPERTS, tm), F32)], axis=0)
    hx_ref[:, SLABS, :] = cmb_rows.T
    cls_ref[...] = cls

    @pl.when(pl.program_id(0) == 0)
    def _():
        cnt_ref[...] = jnp.zeros_like(cnt_ref)

    onehot = jnp.where(lax.broadcasted_iota(jnp.int32, (CLASS_ROWS, tm), 0) == cls, 1.0, 0.0)
    earlier = (lax.broadcasted_iota(jnp.int32, (tm, tm), 0) < lax.broadcasted_iota(jnp.int32, (tm, tm), 1))
    before = jnp.dot(onehot.astype(BF16), jnp.where(earlier, 1.0, 0.0).astype(BF16), preferred_element_type=F32)
    rank = jnp.sum(onehot * (before + cnt_ref[:, 0:1]), axis=0, keepdims=True)
    rank_ref[...] = rank.astype(jnp.int32)
    cnt_ref[...] = cnt_ref[...] + jnp.sum(onehot, axis=1, keepdims=True)
    count_ref[...] = cnt_ref[...]


def _outproj_call(yr, yh, ym, x, T, mod, w_out, l, ln_g, ln_b, w_router_t, b_router):
    n = x.shape[0]
    tm = WIDE_TILE if (mod.shape[0] == 1 or T % WIDE_TILE == 0) else ROW_TILE
    row = lambda w: pl.BlockSpec((tm, w), lambda i: (i, 0))
    full = lambda a: pl.BlockSpec(a.shape, lambda i: (0,) * a.ndim)
    return pl.pallas_call(
        _outproj_kernel,
        grid=(n // tm,),
        in_specs=[row(D_GROUP), row(D_GROUP), row(D_GROUP), row(D_MODEL), _mod_spec(mod, T, tm),
                  pl.BlockSpec((None, D_MIX, D_MODEL), lambda i: (l, 0, 0)),
                  full(ln_g), full(ln_b), full(w_router_t), full(b_router)],
        out_specs=[row(D_MODEL), pl.BlockSpec((tm, SLABS + 1, LANES), lambda i: (i, 0, 0)),
                   pl.BlockSpec((1, tm), lambda i: (0, i)), pl.BlockSpec((1, tm), lambda i: (0, i)),
                   pl.BlockSpec((CLASS_ROWS, LANES), lambda i: (0, 0))],
        out_shape=[jax.ShapeDtypeStruct((n, D_MODEL), F32),
                   jax.ShapeDtypeStruct((n, SLABS + 1, LANES), F32),
                   jax.ShapeDtypeStruct((1, n), jnp.int32), jax.ShapeDtypeStruct((1, n), jnp.int32),
                   jax.ShapeDtypeStruct((CLASS_ROWS, LANES), F32)],
        scratch_shapes=[pltpu.VMEM((CLASS_ROWS, LANES), F32)],
        compiler_params=_cparams(1),
        name="outproj_ln_router",
    )(yr, yh, ym, x, mod, w_out, ln_g, ln_b, w_router_t, b_router)


MOE_TILE = 256


def _row_gather(idx_ref, first, n_rows, src_hbm, dst_ref, sem):
    def body(p, carry):
        for k in range(2):
            r = 2 * p + k
            pltpu.make_async_copy(src_hbm.at[idx_ref[first + r]], dst_ref.at[:, r, :], sem).start(priority=k)
        return carry
    lax.fori_loop(0, n_rows // 2, body, 0, unroll=4)


def _row_gather_wait(n_rows, src_hbm, dst_ref, sem):
    def body(r, carry):
        pltpu.make_async_copy(src_hbm.at[0], dst_ref.at[:, 0, :], sem).wait()
        return carry
    lax.fori_loop(0, n_rows, body, 0, unroll=8)


def _gathered_tile(idx_ref, src_hbm, buf_ref, sem_ref, tm):
    i = pl.program_id(0)
    slot = lax.rem(i, 2)

    @pl.when(i == 0)
    def _():
        _row_gather(idx_ref, 0, tm, src_hbm, buf_ref.at[0], sem_ref.at[0])

    @pl.when(i + 1 < pl.num_programs(0))
    def _():
        _row_gather(idx_ref, (i + 1) * tm, tm, src_hbm, buf_ref.at[1 - slot], sem_ref.at[1 - slot])

    _row_gather_wait(tm, src_hbm, buf_ref.at[slot], sem_ref.at[slot])
    return buf_ref.at[slot]


def _moe_kernel(tile_group_ref, need_ref, src_ref, hx_hbm, wg_ref, wu_ref, wd_ref, o_ref, buf_ref, sem_ref,
                acc_ref):
    tm = o_ref.shape[0]
    i = pl.program_id(0)
    rows_ref = _gathered_tile(src_ref, hx_hbm, buf_ref, sem_ref, tm)
    hm = jnp.concatenate([rows_ref[j] for j in range(SLABS)], axis=1).astype(BF16)
    cmb = rows_ref[SLABS]
    lane = lax.broadcasted_iota(jnp.int32, cmb.shape, 1)
    first_expert = tile_group_ref[i] * EXPERTS_PER_GROUP
    acc_ref[...] = jnp.zeros_like(acc_ref)
    for k in range(EXPERTS_PER_GROUP):
        @pl.when(need_ref[i * EXPERTS_PER_GROUP + k] != 0)
        def _(k=k):
            ce = jnp.sum(jnp.where(lane == first_expert + k, cmb, 0.0), axis=1, keepdims=True)
            hg = jnp.dot(hm, wg_ref[k], preferred_element_type=F32)
            hu = jnp.dot(hm, wu_ref[k], preferred_element_type=F32)
            acc_ref[...] += jnp.dot((_silu(hg) * hu * ce).astype(BF16), wd_ref[k], preferred_element_type=F32)
    for j in range(SLABS):
        o_ref[:, j, :] = acc_ref[:, j * LANES:(j + 1) * LANES]


def _moe_call(hx, tile_group, need, src, w_gate, w_up, w_down, l):
    n_pad = src.shape[0]
    tm = MOE_TILE
    grp_w = lambda shape: pl.BlockSpec((None,) + shape, lambda i, tg, nd, sr: (l, tg[i], 0, 0))
    return pl.pallas_call(
        _moe_kernel,
        grid_spec=pltpu.PrefetchScalarGridSpec(
            num_scalar_prefetch=3,
            grid=(n_pad // tm,),
            in_specs=[pl.BlockSpec(memory_space=pl.ANY),
                      grp_w((EXPERTS_PER_GROUP, D_MODEL, D_FF)), grp_w((EXPERTS_PER_GROUP, D_MODEL, D_FF)),
                      grp_w((EXPERTS_PER_GROUP, D_FF, D_MODEL))],
            out_specs=pl.BlockSpec((tm, SLABS, LANES), lambda i, tg, nd, sr: (i, 0, 0)),
            scratch_shapes=[pltpu.VMEM((2, SLABS + 1, tm, LANES), F32), pltpu.SemaphoreType.DMA((2,)),
                            pltpu.VMEM((tm, D_MODEL), F32)]),
        out_shape=jax.ShapeDtypeStruct((n_pad, SLABS, LANES), F32),
        compiler_params=_cparams(1),
        name="moe_sorted",
    )(tile_group, need, src, hx, w_gate, w_up, w_down)


def _ln2_kernel(pos_ref, f_hbm, x1_ref, mod_ref, lg_ref, lb_ref, o_ref, buf_ref, sem_ref):
    tm = o_ref.shape[0]
    rows_ref = _gathered_tile(pos_ref, f_hbm, buf_ref, sem_ref, tm)
    f = jnp.concatenate([rows_ref[j] for j in range(SLABS)], axis=1)
    v = ALPHA * x1_ref[...] + mod_ref[GATE2:GATE2 + 1, :] * f
    o_ref[...] = _layer_norm(v, lg_ref[...], lb_ref[...])


def _ln2_call(f_sorted, pos, x1, T, mod, ln_g, ln_b):
    n = x1.shape[0]
    per_seq = mod.shape[0] > 1
    tm = WIDE_TILE if (not per_seq or T % WIDE_TILE == 0) else ROW_TILE
    return pl.pallas_call(
        _ln2_kernel,
        grid_spec=pltpu.PrefetchScalarGridSpec(
            num_scalar_prefetch=1,
            grid=(n // tm,),
            in_specs=[pl.BlockSpec(memory_space=pl.ANY),
                      pl.BlockSpec((tm, D_MODEL), lambda i, ps: (i, 0)),
                      pl.BlockSpec((None, 6, D_MODEL), lambda i, ps: ((i * tm) // T if per_seq else 0, 0, 0)),
                      pl.BlockSpec((1, D_MODEL), lambda i, ps: (0, 0)),
                      pl.BlockSpec((1, D_MODEL), lambda i, ps: (0, 0))],
            out_specs=pl.BlockSpec((tm, D_MODEL), lambda i, ps: (i, 0)),
            scratch_shapes=[pltpu.VMEM((2, SLABS, tm, LANES), F32), pltpu.SemaphoreType.DMA((2,))]),
        out_shape=jax.ShapeDtypeStruct((n, D_MODEL), F32),
        compiler_params=_cparams(1),
        name="moe_combine_ln",
    )(pos, f_sorted, x1, mod, ln_g, ln_b)


def _dispatch_plan(cls, rank, class_count, tm):
    n = cls.shape[0]
    n_pairs = len(EXPERT_PAIRS)
    n_pad = n + N_GROUPS * tm
    i32 = jnp.int32
    count = class_count.reshape(N_GROUPS, n_pairs)
    padded = ((jnp.sum(count, axis=1) + tm - 1) // tm) * tm
    ends = jnp.cumsum(padded)
    class_start = ((ends - padded)[:, None] + jnp.cumsum(count, axis=1) - count).reshape(N_CLASSES)
    pos = class_start[cls] + rank
    src = jnp.zeros((n_pad,), i32).at[pos].set(jnp.arange(n, dtype=i32))
    tile_start = jnp.arange(n_pad // tm, dtype=i32) * tm
    tile_group = jnp.minimum(jnp.sum((tile_start[:, None] >= ends[None, :]).astype(i32), axis=1), N_GROUPS - 1)
    class_end = class_start + count.reshape(N_CLASSES)
    overlap = ((class_start[None, :] < tile_start[:, None] + tm) & (class_end[None, :] > tile_start[:, None])
               & (class_end > class_start)[None, :]).astype(i32)
    pair_has = jnp.array([[int(k in p) for k in range(EXPERTS_PER_GROUP)] for p in EXPERT_PAIRS], i32)
    need = jnp.minimum(overlap @ jnp.tile(pair_has, (N_GROUPS, 1)), 1).reshape(-1)
    return pos, src, tile_group, need


def _grid_pos_embed(n_tokens):
    rows = n_tokens // GRID_W
    r = jnp.repeat(jnp.arange(rows, dtype=F32), GRID_W)
    c = jnp.tile(jnp.arange(GRID_W, dtype=F32), rows)
    q = D_MODEL // 4
    freq = jnp.exp(-jnp.log(10000.0) * jnp.arange(q, dtype=F32) / q)
    ar = r[:, None] * freq
    ac = c[:, None] * freq
    return jnp.concatenate([jnp.sin(ar), jnp.cos(ar), jnp.sin(ac), jnp.cos(ac)], axis=-1)


def _lru_gate_params(wa, ba, wx, bx, lam):
    ncb = D_GROUP // LANES

    def dense(w):
        z = jnp.zeros((ncb, LANES, LANES), F32)
        z = z.at[:, :LRU_BLOCK, :LRU_BLOCK].set(w[0::2])
        return z.at[:, LRU_BLOCK:, LRU_BLOCK:].set(w[1::2])

    wg = jnp.concatenate([dense(wa[0]), dense(wx[0]), dense(wa[1]), dense(wx[1])], axis=-1).astype(BF16)
    per_blk = lambda v: v.reshape(ncb, 1, LANES)
    bg = jnp.concatenate([per_blk(ba[0]), per_blk(bx[0]), per_blk(ba[1]), per_blk(bx[1])], axis=-1)
    lm = jnp.concatenate([per_blk(lam[0]), per_blk(lam[1])], axis=-1)
    return wg, bg, lm


def _mixer_states(B, l, states):
    if states is None:
        return (jnp.zeros((2, B, D_GROUP), F32),
                jnp.zeros((B, 2, N_HEADS, D_HEAD, D_HEAD), F32),
                jnp.zeros((B, 2, N_HEADS, D_HEAD, D_HEAD), F32),
                jnp.zeros((B, 2, N_HEADS, D_HEAD), F32),
                jnp.zeros((B, 2, N_HEADS, D_HEAD), F32))
    h, s, c, n, m = states
    return (jnp.swapaxes(h[:, l], 0, 1), s[:, l], c[:, l], n[:, l],
            jnp.broadcast_to(m[:, l][..., None], (B, 2, N_HEADS, D_HEAD)))


def kernel(x_prompt, x_sample, state_lru_h, state_hgrn_S, state_mlstm_C, state_mlstm_n, state_mlstm_m,
           c, c_ctx, w_ada, b_ada, w_in, b_in, conv_w, conv_b, lru_wa, lru_ba, lru_wx, lru_bx, lru_lam,
           hg_lb, hg_norm_g, ml_norm_g, w_out, ln1_g, ln1_b, ln2_g, ln2_b,
           w_router, b_router, w_gate, w_up, w_down):
    Bp, Tp, _ = x_prompt.shape
    Bs, Ts, _ = x_sample.shape
    assert Bs % SUBLANES == 0 and Bp % SUBLANES == 0 and Bs + 1 <= 16
    assert Tp % ROW_TILE == 0 and Ts % ROW_TILE == 0

    lb_soft = jax.nn.softmax(hg_lb.astype(F32), axis=0)
    hg_lower = jnp.cumsum(lb_soft, axis=0) - lb_soft[0:1]

    c16 = jnp.concatenate([c, c_ctx[None], jnp.zeros((16 - Bs - 1, D_MODEL), F32)], axis=0)
    mod = _ada_call(c16, w_ada, b_ada)

    w_in_p = jnp.pad(w_in.astype(BF16), ((0, 0), (0, 0), (0, D_IN_PAD - D_IN)))
    b_in_p = jnp.pad(b_in, ((0, 0), (0, D_IN_PAD - D_IN))).reshape(DEPTH, 1, D_IN_PAD)
    w_out_b = w_out.astype(BF16)
    w_gate_b, w_up_b, w_down_b = w_gate.astype(BF16), w_up.astype(BF16), w_down.astype(BF16)
    w_router_t = w_router.T
    b_router_c = b_router.reshape(N_EXPERTS, 1)

    xs = x_sample + _grid_pos_embed(Ts).astype(x_sample.dtype)
    streams = [
        dict(x=x_prompt.reshape(Bp * Tp, D_MODEL), T=Tp, B=Bp, states=None),
        dict(x=xs.reshape(Bs * Ts, D_MODEL), T=Ts, B=Bs,
             states=(state_lru_h, state_hgrn_S, state_mlstm_C, state_mlstm_n, state_mlstm_m)),
    ]
    finals = []
    for l in range(DEPTH):
        wg, bg, lm = _lru_gate_params(lru_wa[l], lru_ba[l], lru_wx[l], lru_bx[l], lru_lam[l])
        row = lambda v: v[l].reshape(1, -1)
        for si, st in enumerate(streams):
            T, B = st["T"], st["B"]
            mod_s = (mod[l, :Bs] if si == 1 else mod[l, Bs:Bs + 1]).reshape(-1, 6, D_MODEL)
            h0, s0, c0, n0, m0 = _mixer_states(B, l, st["states"])
            z = _inproj_call(st["x"], T, mod_s, w_in_p, b_in_p, l)
            y_r, h_T = _rglru_call(z.reshape(B, T, D_IN_PAD), h0, conv_w[l], row(conv_b), wg, bg, lm)
            y_h, s_T = _hgrn_call(z, B, s0, hg_lower[l], row(hg_norm_g))
            y_m, c_T, n_T, m_T = _mlstm_call(z, B, c0, n0, m0, row(ml_norm_g))
            x1, hx, cls, rank, cnt = _outproj_call(y_r.reshape(B * T, D_GROUP), y_h, y_m, st["x"], T, mod_s,
                                                   w_out_b, l, row(ln1_g), row(ln1_b), w_router_t, b_router_c)
            pos, src, tile_group, need = _dispatch_plan(cls[0], rank[0], cnt[:N_CLASSES, 0].astype(jnp.int32),
                                                        MOE_TILE)
            f_sorted = _moe_call(hx, tile_group, need, src, w_gate_b, w_up_b, w_down_b, l)
            st["x"] = _ln2_call(f_sorted, pos, x1, T, mod_s, row(ln2_g), row(ln2_b))
            if si == 0:
                finals.append((jnp.swapaxes(h_T, 0, 1), s_T, c_T, n_T, m_T[..., 0]))
    outs = [st["x"].reshape(st["B"], st["T"], D_MODEL) for st in streams]
    stack = lambda i: jnp.stack([f[i] for f in finals], axis=1)
    return (outs[0], outs[1], stack(0), stack(1), stack(2), stack(3), stack(4))
```

```python
import functools

import jax
import jax.numpy as jnp
from jax import lax
from jax.experimental import pallas as pl
from jax.experimental.pallas import tpu as pltpu

F32 = jnp.float32
BF16 = jnp.bfloat16
HIGHEST = lax.Precision.HIGHEST

D_MODEL = 1024
DEPTH = 2
GRID_W = 64
D_GROUP = 512
D_MIX = 3 * D_GROUP
LRU_BLOCKS = 8
LRU_BLOCK = D_GROUP // LRU_BLOCKS
LRU_C = 8.0
N_HEADS = 4
D_HEAD = D_GROUP // N_HEADS
N_EXPERTS = 16
N_GROUPS = 4
EXPERTS_PER_GROUP = N_EXPERTS // N_GROUPS
D_FF = 512
ALPHA = (2.0 * DEPTH) ** 0.25
EPS = 1e-5
TINY = 1e-30
N_GATE = 4 * N_HEADS
D_IN = 11 * D_GROUP + N_GATE
D_IN_PAD = 45 * 128
GATE_BLK = (11 * D_GROUP) // 128

SUBLANES = 8
LANES = 128
VMEM_LIMIT = 56 * 1024 * 1024

LRU_TC = 64
LRU_PAD = SUBLANES
HG_L = 128
HG_SUB = 16
ML_L = 128
ML_SUB = 32


def _cparams(n_axes):
    return pltpu.CompilerParams(dimension_semantics=("arbitrary",) * n_axes,
                                vmem_limit_bytes=VMEM_LIMIT)


def _dot(a, b):
    return jnp.dot(a.astype(BF16), b.astype(BF16), preferred_element_type=F32)


def _dot_nt(a, b):
    return lax.dot_general(a.astype(BF16), b.astype(BF16), (((1,), (1,)), ((), ())),
                           preferred_element_type=F32)


def _dot_tn(a, b):
    return lax.dot_general(a.astype(BF16), b.astype(BF16), (((0,), (0,)), ((), ())),
                           preferred_element_type=F32)


def _dot_f32(a, b):
    return jnp.dot(a, b, precision=HIGHEST, preferred_element_type=F32)


def _dot_split3(mask, x, mask_on_left):
    m = jnp.where(mask, 1.0, 0.0).astype(BF16)
    hi = x.astype(BF16)
    r1 = x - hi.astype(F32)
    mid = r1.astype(BF16)
    lo = (r1 - mid.astype(F32)).astype(BF16)
    mm = (lambda p: jnp.dot(m, p, preferred_element_type=F32)) if mask_on_left else (
        lambda p: jnp.dot(p, m, preferred_element_type=F32))
    return mm(hi) + mm(mid) + mm(lo)


def _softplus(x):
    return jnp.maximum(x, 0.0) + jnp.log1p(jnp.exp(-jnp.abs(x)))


def _silu(x):
    return x * jax.nn.sigmoid(x)


def _tri_masks(n):
    r = lax.broadcasted_iota(jnp.int32, (n, n), 0)
    c = lax.broadcasted_iota(jnp.int32, (n, n), 1)
    return r >= c, r <= c


def _ada_kernel(c_ref, w_ref, b_ref, o_ref):
    o_ref[0] = _dot_f32(_silu(c_ref[...]), w_ref[0]) + b_ref[0]


def _ada_call(c16, w_ada, b_ada):
    tn = 1536
    return pl.pallas_call(
        _ada_kernel,
        grid=(DEPTH, 6 * D_MODEL // tn),
        in_specs=[pl.BlockSpec((16, D_MODEL), lambda l, j: (0, 0)),
                  pl.BlockSpec((1, D_MODEL, tn), lambda l, j: (l, 0, j)),
                  pl.BlockSpec((1, 1, tn), lambda l, j: (l, 0, j))],
        out_specs=pl.BlockSpec((1, 16, tn), lambda l, j: (l, 0, j)),
        out_shape=jax.ShapeDtypeStruct((DEPTH, 16, 6 * D_MODEL), F32),
        compiler_params=_cparams(2),
        name="adaln",
    )(c16, w_ada, b_ada.reshape(DEPTH, 1, 6 * D_MODEL))


INPROJ_CW = 640
ROW_TILE = 256
WIDE_TILE = 512

SHIFT1, SCALE1, GATE1, SHIFT2, SCALE2, GATE2 = range(6)


def _mod_spec(mod, T, tm, n_grid_axes=1):
    per_seq = mod.shape[0] > 1
    if n_grid_axes == 1:
        return pl.BlockSpec((None, 6, D_MODEL), lambda i: ((i * tm) // T if per_seq else 0, 0, 0))
    return pl.BlockSpec((None, 6, D_MODEL), lambda i, e: ((i * tm) // T if per_seq else 0, 0, 0))


def _inproj_kernel(x_ref, mod_ref, w_ref, b_ref, z_ref):
    m = mod_ref[...]
    hb = (x_ref[...] * (1.0 + m[SCALE1:SCALE1 + 1]) + m[SHIFT1:SHIFT1 + 1]).astype(BF16)
    for j in range(D_IN_PAD // INPROJ_CW):
        cs = slice(j * INPROJ_CW, (j + 1) * INPROJ_CW)
        z_ref[:, cs] = jnp.dot(hb, w_ref[:, cs], preferred_element_type=F32) + b_ref[:, cs]


def _inproj_call(x, T, mod, w, b, l):
    n = x.shape[0]
    tm = WIDE_TILE if (mod.shape[0] == 1 or T % WIDE_TILE == 0) else ROW_TILE
    return pl.pallas_call(
        _inproj_kernel,
        grid=(n // tm,),
        in_specs=[pl.BlockSpec((tm, D_MODEL), lambda i: (i, 0)),
                  _mod_spec(mod, T, tm),
                  pl.BlockSpec((None, D_MODEL, D_IN_PAD), lambda i: (l, 0, 0), pipeline_mode=pl.Buffered(1)),
                  pl.BlockSpec((None, 1, D_IN_PAD), lambda i: (l, 0, 0))],
        out_specs=pl.BlockSpec((tm, D_IN_PAD), lambda i: (i, 0)),
        out_shape=jax.ShapeDtypeStruct((n, D_IN_PAD), F32),
        compiler_params=_cparams(1),
        name="inproj",
    )(x, mod, w, b)


def _rglru_kernel(xr_ref, gr_ref, cw_ref, cb_ref, wg_ref, bg_ref, lam_ref, h0_ref,
                  y_ref, hT_ref, xs_ref, os_ref, ob_ref):
    T = xr_ref.shape[1]
    P = T + LRU_PAD
    TC = LRU_TC
    nch = T // TC
    CP = 256
    zeros = jnp.zeros((LRU_PAD, LANES), F32)
    for s in range(SUBLANES + 1):
        xs_ref[s * P:s * P + LRU_PAD] = zeros
    for s in range(SUBLANES):
        def copy_in(c, carry, s=s):
            r0 = pl.multiple_of(c * CP, CP)
            xs_ref[pl.ds(s * P + LRU_PAD + r0, CP)] = xr_ref[s, pl.ds(r0, CP)]
            return carry
        lax.fori_loop(0, T // CP, copy_in, 0)

    cw = cw_ref[...]
    cb = cb_ref[...]
    sp = _softplus(-lam_ref[0])

    def gates(t0, d):
        win = [xs_ref[pl.ds(t0 + k + LRU_PAD - 2, SUBLANES, stride=P)] for k in range(TC + 3)]
        xc = jnp.stack([cw[0:1] * win[i] + cw[1:2] * win[i + 1] + cw[2:3] * win[i + 2] + cw[3:4] * win[i + 3] + cb
                        for i in range(TC)])
        x2 = xc.reshape(TC * SUBLANES, LANES)
        g = (jnp.dot(x2.astype(BF16), wg_ref[0, :, d * 256:(d + 1) * 256], preferred_element_type=F32)
             + bg_ref[0, :, d * 256:(d + 1) * 256])
        r = jax.nn.sigmoid(g[:, :LANES])
        i = jax.nn.sigmoid(g[:, LANES:])
        log_a = -LRU_C * r * sp[:, d * LANES:(d + 1) * LANES]
        a = jnp.exp(log_a)
        u = jnp.sqrt(jnp.maximum(1.0 - a * a, 0.0)) * (i * x2)
        return a.reshape(TC, SUBLANES, LANES), u.reshape(TC, SUBLANES, LANES)

    def step(c, carry):
        hf, hb = carry
        tf = pl.multiple_of(c * TC, TC)
        tb = pl.multiple_of((nch - 1 - c) * TC, TC)
        af, uf = gates(tf, 0)
        ab, ub = gates(tb, 1)
        for i in range(TC):
            hf = af[i] * hf + uf[i]
            os_ref[pl.ds(tf + i, SUBLANES, stride=P)] = hf
            k = TC - 1 - i
            hb = ab[k] * hb + ub[k]
            ob_ref[pl.ds(tb + k, SUBLANES, stride=P)] = hb
        return hf, hb

    hT_ref[0], hT_ref[1] = lax.fori_loop(0, nch, step, (h0_ref[0], h0_ref[1]))

    for s in range(SUBLANES):
        def copy_out(c, carry, s=s):
            r0 = pl.multiple_of(c * CP, CP)
            o = os_ref[pl.ds(s * P + r0, CP)] + ob_ref[pl.ds(s * P + r0, CP)]
            y_ref[s, pl.ds(r0, CP)] = jax.nn.gelu(gr_ref[s, pl.ds(r0, CP)]) * o
            return carry
        lax.fori_loop(0, T // CP, copy_out, 0)


def _rglru_call(z3, h0, cw, cb, wg, bg, lam):
    B, T, _ = z3.shape
    nb = B // SUBLANES
    ncb = D_GROUP // LANES
    blk = (SUBLANES, T, LANES)
    pitch = T + LRU_PAD
    return pl.pallas_call(
        _rglru_kernel,
        grid=(nb, ncb),
        in_specs=[pl.BlockSpec(blk, lambda b, c: (b, 0, c)),
                  pl.BlockSpec(blk, lambda b, c: (b, 0, ncb + c)),
                  pl.BlockSpec((4, LANES), lambda b, c: (0, c)),
                  pl.BlockSpec((1, LANES), lambda b, c: (0, c)),
                  pl.BlockSpec((1, LANES, 4 * LANES), lambda b, c: (c, 0, 0)),
                  pl.BlockSpec((1, 1, 4 * LANES), lambda b, c: (c, 0, 0)),
                  pl.BlockSpec((1, 1, 2 * LANES), lambda b, c: (c, 0, 0)),
                  pl.BlockSpec((2, SUBLANES, LANES), lambda b, c: (0, b, c))],
        out_specs=[pl.BlockSpec(blk, lambda b, c: (b, 0, c)),
                   pl.BlockSpec((2, SUBLANES, LANES), lambda b, c: (0, b, c))],
        out_shape=[jax.ShapeDtypeStruct((B, T, D_GROUP), F32),
                   jax.ShapeDtypeStruct((2, B, D_GROUP), F32)],
        scratch_shapes=[pltpu.VMEM((SUBLANES * pitch + LRU_PAD, LANES), F32),
                        pltpu.VMEM((SUBLANES * pitch, LANES), F32),
                        pltpu.VMEM((SUBLANES * pitch, LANES), F32)],
        compiler_params=_cparams(2),
        name="rglru",
    )(z3, z3, cw, cb, wg, bg, lam, h0)


def _hgrn_kernel(q_ref, ff_ref, fb_ref, v_ref, og_ref, lb_ref, ng_ref, s0_ref,
                 y_ref, sT_ref, st_ref, of_ref, ob_ref):
    T = q_ref.shape[0]
    L = HG_L
    SUB = HG_SUB
    nch = T // L
    nsub = L // SUB
    tril, triu = _tri_masks(L)
    heads = [slice(hd * D_HEAD, (hd + 1) * D_HEAD) for hd in range(N_HEADS)]

    for d in range(2):
        for hd in range(N_HEADS):
            st_ref[d, hd] = s0_ref[d, hd].T

    def step(c, carry):
        jobs = ((0, pl.multiple_of(c * L, L), of_ref), (1, pl.multiple_of((nch - 1 - c) * L, L), ob_ref))
        gated = []
        for d, t0, _ in jobs:
            q = _silu(q_ref[pl.ds(t0, L), :])
            v = v_ref[pl.ds(t0, L), :].astype(BF16)
            lb = lb_ref[d:d + 1, :]
            sig = jax.nn.sigmoid((ff_ref if d == 0 else fb_ref)[pl.ds(t0, L), :])
            log_f = jnp.log(jnp.maximum(lb + (1.0 - lb) * sig, TINY))
            kk = (1.0 - lb) * (1.0 - sig)
            mask = tril if d == 0 else triu
            b = _dot_split3(mask, log_f, mask_on_left=True)
            gated.append((q, v, kk, b, mask))
        factored = []
        for (d, _, _), (q, v, kk, b, mask) in zip(jobs, gated):
            b_tot = b[L - 1:L] if d == 0 else b[0:1]
            subs = []
            for i in range(nsub):
                rows = slice(i * SUB, (i + 1) * SUB)
                if d == 0:
                    cols = slice(0, (i + 1) * SUB)
                    edge = b[i * SUB - 1:i * SUB] if i > 0 else 0.0
                else:
                    cols = slice(i * SUB, L)
                    edge = b[(i + 1) * SUB:(i + 1) * SUB + 1] if i < nsub - 1 else 0.0
                subs.append((rows, cols, (q[rows] * jnp.exp(b[rows] - edge)).astype(BF16),
                             (kk[cols] * jnp.exp(edge - b[cols])).astype(BF16)))
            factored.append(((q * jnp.exp(b)).astype(BF16), (kk * jnp.exp(b_tot - b)).astype(BF16),
                             jnp.exp(b_tot), subs))
        scores = [[[_dot_nt(qs[:, hs], ks[:, hs]) for (_, _, qs, ks) in subs] for hs in heads]
                  for (_, _, _, subs) in factored]
        states = [[st_ref[d, hd] for hd in range(N_HEADS)] for d, _, _ in jobs]
        inter = [[_dot_nt(qdec[:, hs], states[j][hd]) for hd, hs in enumerate(heads)]
                 for j, (qdec, _, _, _) in enumerate(factored)]
        update = [[_dot_tn(gated[j][1][:, hs], kdec[:, hs]) for hs in heads]
                  for j, (_, kdec, _, _) in enumerate(factored)]
        for j, (d, t0, o_ref) in enumerate(jobs):
            v, mask = gated[j][1], gated[j][4]
            dec, subs = factored[j][2], factored[j][3]
            for hd, hs in enumerate(heads):
                intra = [_dot(jnp.where(mask[rows, cols], scores[j][hd][i], 0.0), v[cols, hs])
                         for i, (rows, cols, _, _) in enumerate(subs)]
                o_ref[pl.ds(t0, L), hs] = jnp.concatenate(intra, axis=0) + inter[j][hd]
                st_ref[d, hd] = states[j][hd] * dec[:, hs] + update[j][hd]
        return carry

    lax.fori_loop(0, nch, step, 0)

    def combine(c, carry):
        t0 = pl.multiple_of(c * L, L)
        for hs in heads:
            o = of_ref[pl.ds(t0, L), hs] + ob_ref[pl.ds(t0, L), hs]
            o = o * lax.rsqrt(jnp.mean(jnp.square(o), -1, keepdims=True) + EPS)
            y_ref[pl.ds(t0, L), hs] = o * ng_ref[:, hs] * _silu(og_ref[pl.ds(t0, L), hs])
        return carry

    lax.fori_loop(0, nch, combine, 0)

    for d in range(2):
        for hd in range(N_HEADS):
            sT_ref[d, hd] = st_ref[d, hd].T


def _seq_col_spec(T, j):
    return pl.BlockSpec((T, D_GROUP), lambda b, j=j: (b, j))


def _hgrn_call(z2, B, s0, lower, norm_g):
    T = z2.shape[0] // B
    col = functools.partial(_seq_col_spec, T)
    st_spec = pl.BlockSpec((None, 2, N_HEADS, D_HEAD, D_HEAD), lambda b: (b, 0, 0, 0, 0))
    return pl.pallas_call(
        _hgrn_kernel,
        grid=(B,),
        in_specs=[col(2), col(3), col(4), col(5), col(6),
                  pl.BlockSpec((2, D_GROUP), lambda b: (0, 0)),
                  pl.BlockSpec((1, D_GROUP), lambda b: (0, 0)),
                  st_spec],
        out_specs=[pl.BlockSpec((T, D_GROUP), lambda b: (b, 0)), st_spec],
        out_shape=[jax.ShapeDtypeStruct((B * T, D_GROUP), F32),
                   jax.ShapeDtypeStruct((B, 2, N_HEADS, D_HEAD, D_HEAD), F32)],
        scratch_shapes=[pltpu.VMEM((2, N_HEADS, D_HEAD, D_HEAD), F32),
                        pltpu.VMEM((T, D_GROUP), F32),
                        pltpu.VMEM((T, D_GROUP), F32)],
        compiler_params=_cparams(1),
        name="hgrn2",
    )(z2, z2, z2, z2, z2, lower, norm_g, s0)


def _mlstm_kernel(q_ref, k_ref, v_ref, og_ref, g_ref, ng_ref, c0_ref, n0_ref, m0_ref,
                  y_ref, cT_ref, nT_ref, mT_ref, cn_ref, of_ref, ob_ref):
    T = q_ref.shape[0]
    L = ML_L
    SUB = ML_SUB
    nch = T // L
    nsub = L // SUB
    tril, triu = _tri_masks(L)
    gcol = lax.broadcasted_iota(jnp.int32, (L, N_GATE), 1)
    heads = [slice(hd * D_HEAD, (hd + 1) * D_HEAD) for hd in range(N_HEADS)]
    ones = jnp.ones((L, D_HEAD), BF16)

    for d in range(2):
        for hd in range(N_HEADS):
            cn_ref[d, hd, :, 0:D_HEAD] = c0_ref[d, hd]
            cn_ref[d, hd, :, D_HEAD:] = jnp.broadcast_to(n0_ref[d, hd:hd + 1, :], (D_HEAD, D_HEAD)).T
    mT_ref[...] = m0_ref[...]

    def step(c, carry):
        jobs = ((0, pl.multiple_of(c * L, L), of_ref), (1, pl.multiple_of((nch - 1 - c) * L, L), ob_ref))
        loaded, q_state = [], []
        for d, t0, _ in jobs:
            g = g_ref[pl.ds(t0, L), 0:N_GATE]
            g = jnp.where(gcol % 8 >= N_HEADS, jax.nn.log_sigmoid(g), g)
            bc = _dot_split3(tril if d == 0 else triu, g, mask_on_left=True)
            qb = q_ref[pl.ds(t0, L), :].astype(BF16)
            kf = k_ref[pl.ds(t0, L), :] * (D_HEAD ** -0.5)
            vb = v_ref[pl.ds(t0, L), :].astype(BF16)
            v1 = [jnp.concatenate([vb[:, hs], ones], axis=1) for hs in heads]
            loaded.append((g, bc, qb, kf, v1))
            q_state.append([_dot(qb[:, hs], cn_ref[d, hd]) for hd, hs in enumerate(heads)])
        scaled = []
        for j, (d, _, _) in enumerate(jobs):
            g, bc, qb, kf, v1 = loaded[j]
            per_head = []
            for hd, hs in enumerate(heads):
                ic, fc = d * 8 + hd, d * 8 + N_HEADS + hd
                b_col = bc[:, fc:fc + 1]
                r_col = g[:, ic:ic + 1] - b_col
                b_tot = b_col[L - 1:L] if d == 0 else b_col[0:1]
                m_st = mT_ref[d, hd:hd + 1, 0:1]
                blk_max = [jnp.max(r_col[i * SUB:(i + 1) * SUB], axis=0, keepdims=True) for i in range(nsub)]
                r_rep = jnp.broadcast_to(r_col, (L, D_HEAD))
                subs = []
                for i in range(nsub):
                    rows = slice(i * SUB, (i + 1) * SUB)
                    cols = slice(0, (i + 1) * SUB) if d == 0 else slice(i * SUB, L)
                    seen = blk_max[:i + 1] if d == 0 else blk_max[i:]
                    mu = jnp.maximum(m_st, functools.reduce(jnp.maximum, seen))
                    kp = (kf[cols, hs] * jnp.exp(r_rep[cols] - mu)).astype(BF16)
                    floor = jnp.exp(-b_col[rows] - mu)
                    subs.append((rows, cols, mu, kp, floor))
                per_head.append((b_tot, m_st, subs))
            scaled.append(per_head)
        scores = [[[_dot_nt(loaded[j][2][rows, hs], kp) for (rows, _, _, kp, _) in scaled[j][hd][2]]
                   for hd, hs in enumerate(heads)] for j in range(2)]
        full = [nsub - 1, 0]
        update = [[_dot_tn(scaled[j][hd][2][full[j]][3], loaded[j][4][hd]) for hd in range(N_HEADS)]
                  for j in range(2)]
        for j, (d, t0, o_ref) in enumerate(jobs):
            mask = tril if d == 0 else triu
            for hd, hs in enumerate(heads):
                b_tot, m_st, subs = scaled[j][hd]
                outs = []
                for i, (rows, cols, mu, _, floor) in enumerate(subs):
                    w = jnp.where(mask[rows, cols], scores[j][hd][i], 0.0)
                    nd = _dot(w, loaded[j][4][hd][cols]) + jnp.exp(m_st - mu) * q_state[j][hd][rows]
                    outs.append(nd[:, 0:D_HEAD] / jnp.maximum(jnp.abs(nd[:, D_HEAD:]), floor))
                o_ref[pl.ds(t0, L), hs] = jnp.concatenate(outs, axis=0)
                mu_full = subs[full[j]][2]
                cn_ref[d, hd] = jnp.exp(m_st - mu_full) * cn_ref[d, hd] + update[j][hd]
                mT_ref[d, hd:hd + 1, :] = jnp.broadcast_to(b_tot + mu_full, (1, LANES))
        return carry

    lax.fori_loop(0, nch, step, 0)

    for d in range(2):
        for hd in range(N_HEADS):
            cT_ref[d, hd] = cn_ref[d, hd, :, 0:D_HEAD]
            nT_ref[d, hd:hd + 1, :] = cn_ref[d, hd, :, D_HEAD:].T[0:1, :]

    def combine(c, carry):
        t0 = pl.multiple_of(c * L, L)
        for hs in heads:
            o = of_ref[pl.ds(t0, L), hs] + ob_ref[pl.ds(t0, L), hs]
            mu = jnp.mean(o, -1, keepdims=True)
            var = jnp.mean(jnp.square(o - mu), -1, keepdims=True)
            o = (o - mu) * lax.rsqrt(var + EPS)
            y_ref[pl.ds(t0, L), hs] = jax.nn.sigmoid(og_ref[pl.ds(t0, L), hs]) * (o * ng_ref[:, hs])
        return carry

    lax.fori_loop(0, nch, combine, 0)


def _mlstm_call(z2, B, c0, n0, m0, norm_g):
    T = z2.shape[0] // B
    col = functools.partial(_seq_col_spec, T)
    c_spec = pl.BlockSpec((None, 2, N_HEADS, D_HEAD, D_HEAD), lambda b: (b, 0, 0, 0, 0))
    v_spec = pl.BlockSpec((None, 2, N_HEADS, D_HEAD), lambda b: (b, 0, 0, 0))
    return pl.pallas_call(
        _mlstm_kernel,
        grid=(B,),
        in_specs=[col(7), col(8), col(9), col(10),
                  pl.BlockSpec((T, LANES), lambda b: (b, GATE_BLK)),
                  pl.BlockSpec((1, D_GROUP), lambda b: (0, 0)),
                  c_spec, v_spec, v_spec],
        out_specs=[pl.BlockSpec((T, D_GROUP), lambda b: (b, 0)), c_spec, v_spec, v_spec],
        out_shape=[jax.ShapeDtypeStruct((B * T, D_GROUP), F32),
                   jax.ShapeDtypeStruct((B, 2, N_HEADS, D_HEAD, D_HEAD), F32),
                   jax.ShapeDtypeStruct((B, 2, N_HEADS, D_HEAD), F32),
                   jax.ShapeDtypeStruct((B, 2, N_HEADS, D_HEAD), F32)],
        scratch_shapes=[pltpu.VMEM((2, N_HEADS, D_HEAD, 2 * D_HEAD), F32),
                        pltpu.VMEM((T, D_GROUP), F32), pltpu.VMEM((T, D_GROUP), F32)],
        compiler_params=_cparams(1),
        name="mlstm",
    )(z2, z2, z2, z2, z2, norm_g, c0, n0, m0)


def _layer_norm(v, g, b):
    mu = jnp.mean(v, -1, keepdims=True)
    var = jnp.mean(jnp.square(v - mu), -1, keepdims=True)
    return (v - mu) * lax.rsqrt(var + EPS) * g + b


EXPERT_PAIRS = ((0, 1), (0, 2), (1, 2), (1, 3), (0, 3), (2, 3))
N_CLASSES = N_GROUPS * len(EXPERT_PAIRS)
CLASS_ROWS = 32


def _route(logits_t, b_router):
    mx = jnp.max(logits_t, axis=0, keepdims=True)
    ex = jnp.exp(logits_t - mx)
    scores = ex / jnp.sum(ex, axis=0, keepdims=True)
    sel = scores + b_router
    rows = [sel[e:e + 1] for e in range(N_EXPERTS)]
    first, second, grp = [], [], []
    for gi in range(N_GROUPS):
        r = rows[gi * EXPERTS_PER_GROUP:(gi + 1) * EXPERTS_PER_GROUP]
        m1 = functools.reduce(jnp.maximum, r)
        taken = jnp.zeros_like(m1, dtype=jnp.bool_)
        f = []
        for x in r:
            hit = jnp.logical_and(x == m1, jnp.logical_not(taken))
            f.append(hit)
            taken = jnp.logical_or(taken, hit)
        rest = [jnp.where(fi, -jnp.inf, x) for fi, x in zip(f, r)]
        m2 = functools.reduce(jnp.maximum, rest)
        taken = jnp.zeros_like(m1, dtype=jnp.bool_)
        s = []
        for x in rest:
            hit = jnp.logical_and(x == m2, jnp.logical_not(taken))
            s.append(hit)
            taken = jnp.logical_or(taken, hit)
        first.append(f)
        second.append(s)
        grp.append(m1 + m2)
    gmax = functools.reduce(jnp.maximum, grp)
    taken = jnp.zeros_like(gmax, dtype=jnp.bool_)
    chosen = []
    cls = jnp.zeros_like(gmax, dtype=jnp.int32)
    for gi in range(N_GROUPS):
        best = jnp.logical_and(grp[gi] == gmax, jnp.logical_not(taken))
        taken = jnp.logical_or(taken, best)
        in_grp = [jnp.logical_and(best, jnp.logical_or(first[gi][j], second[gi][j]))
                  for j in range(EXPERTS_PER_GROUP)]
        chosen.extend(in_grp)
        for p, (a, b) in enumerate(EXPERT_PAIRS):
            cls = jnp.where(jnp.logical_and(in_grp[a], in_grp[b]), gi * len(EXPERT_PAIRS) + p, cls)
    picked = [jnp.where(ch, scores[e:e + 1], 0.0) for e, ch in enumerate(chosen)]
    denom = functools.reduce(lambda a, b: a + b, picked)
    return jnp.concatenate([p / denom for p in picked], axis=0), cls


SLABS = D_MODEL // LANES


def _outproj_kernel(yr_ref, yh_ref, ym_ref, x_ref, mod_ref, w_ref, lg_ref, lb_ref,
                    wr_ref, br_ref, x1_ref, hx_ref, cls_ref, rank_ref, count_ref, cnt_ref):
    tm = x_ref.shape[0]
    m = mod_ref[...]
    y = (jnp.dot(yr_ref[...].astype(BF16), w_ref[0:D_GROUP], preferred_element_type=F32)
         + jnp.dot(yh_ref[...].astype(BF16), w_ref[D_GROUP:2 * D_GROUP], preferred_element_type=F32)
         + jnp.dot(ym_ref[...].astype(BF16), w_ref[2 * D_GROUP:], preferred_element_type=F32))
    x1 = _layer_norm(ALPHA * x_ref[...] + m[GATE1:GATE1 + 1] * y, lg_ref[...], lb_ref[...])
    x1_ref[...] = x1
    hm = x1 * (1.0 + m[SCALE2:SCALE2 + 1]) + m[SHIFT2:SHIFT2 + 1]
    for j in range(SLABS):
        hx_ref[:, j, :] = hm[:, j * LANES:(j + 1) * LANES]
    logits_t = lax.dot_general(wr_ref[...], hm, (((1,), (1,)), ((), ())), precision=HIGHEST,
                               preferred_element_type=F32)
    cmb_t, cls = _route(logits_t, br_ref[...])
    cmb_rows = jnp.concatenate([cmb_t, jnp.zeros((LANES - N_EXPERTS, tm), F32)], axis=0)
    hx_ref[:, SLABS, :] = cmb_rows.T
    cls_ref[...] = cls

    @pl.when(pl.program_id(0) == 0)
    def _():
        cnt_ref[...] = jnp.zeros_like(cnt_ref)

    onehot = jnp.where(lax.broadcasted_iota(jnp.int32, (CLASS_ROWS, tm), 0) == cls, 1.0, 0.0)
    earlier = (lax.broadcasted_iota(jnp.int32, (tm, tm), 0) < lax.broadcasted_iota(jnp.int32, (tm, tm), 1))
    before = jnp.dot(onehot.astype(BF16), jnp.where(earlier, 1.0, 0.0).astype(BF16), preferred_element_type=F32)
    rank = jnp.sum(onehot * (before + cnt_ref[:, 0:1]), axis=0, keepdims=True)
    rank_ref[...] = rank.astype(jnp.int32)
    cnt_ref[...] = cnt_ref[...] + jnp.sum(onehot, axis=1, keepdims=True)
    count_ref[...] = cnt_ref[...]


def _outproj_call(yr, yh, ym, x, T, mod, w_out, l, ln_g, ln_b, w_router_t, b_router):
    n = x.shape[0]
    tm = WIDE_TILE if (mod.shape[0] == 1 or T % WIDE_TILE == 0) else ROW_TILE
    row = lambda w: pl.BlockSpec((tm, w), lambda i: (i, 0))
    full = lambda a: pl.BlockSpec(a.shape, lambda i: (0,) * a.ndim)
    return pl.pallas_call(
        _outproj_kernel,
        grid=(n // tm,),
        in_specs=[row(D_GROUP), row(D_GROUP), row(D_GROUP), row(D_MODEL), _mod_spec(mod, T, tm),
                  pl.BlockSpec((None, D_MIX, D_MODEL), lambda i: (l, 0, 0)),
                  full(ln_g), full(ln_b), full(w_router_t), full(b_router)],
        out_specs=[row(D_MODEL), pl.BlockSpec((tm, SLABS + 1, LANES), lambda i: (i, 0, 0)),
                   pl.BlockSpec((1, tm), lambda i: (0, i)), pl.BlockSpec((1, tm), lambda i: (0, i)),
                   pl.BlockSpec((CLASS_ROWS, LANES), lambda i: (0, 0))],
        out_shape=[jax.ShapeDtypeStruct((n, D_MODEL), F32),
                   jax.ShapeDtypeStruct((n, SLABS + 1, LANES), F32),
                   jax.ShapeDtypeStruct((1, n), jnp.int32), jax.ShapeDtypeStruct((1, n), jnp.int32),
                   jax.ShapeDtypeStruct((CLASS_ROWS, LANES), F32)],
        scratch_shapes=[pltpu.VMEM((CLASS_ROWS, LANES), F32)],
        compiler_params=_cparams(1),
        name="outproj_ln_router",
    )(yr, yh, ym, x, mod, w_out, ln_g, ln_b, w_router_t, b_router)


MOE_TILE = 256


def _row_gather(idx_ref, first, n_rows, src_hbm, dst_ref, sem):
    def body(p, carry):
        for k in range(2):
            r = 2 * p + k
            pltpu.make_async_copy(src_hbm.at[idx_ref[first + r]], dst_ref.at[:, r, :], sem).start(priority=k)
        return carry
    lax.fori_loop(0, n_rows // 2, body, 0, unroll=4)


def _row_gather_wait(n_rows, src_hbm, dst_ref, sem):
    def body(r, carry):
        pltpu.make_async_copy(src_hbm.at[0], dst_ref.at[:, 0, :], sem).wait()
        return carry
    lax.fori_loop(0, n_rows, body, 0, unroll=8)


def _gathered_tile(idx_ref, src_hbm, buf_ref, sem_ref, tm):
    i = pl.program_id(0)
    slot = lax.rem(i, 2)

    @pl.when(i == 0)
    def _():
        _row_gather(idx_ref, 0, tm, src_hbm, buf_ref.at[0], sem_ref.at[0])

    @pl.when(i + 1 < pl.num_programs(0))
    def _():
        _row_gather(idx_ref, (i + 1) * tm, tm, src_hbm, buf_ref.at[1 - slot], sem_ref.at[1 - slot])

    _row_gather_wait(tm, src_hbm, buf_ref.at[slot], sem_ref.at[slot])
    return buf_ref.at[slot]


def _moe_kernel(tile_group_ref, need_ref, src_ref, hx_hbm, wg_ref, wu_ref, wd_ref, o_ref, buf_ref, sem_ref,
                acc_ref):
    tm = o_ref.shape[0]
    i = pl.program_id(0)
    rows_ref = _gathered_tile(src_ref, hx_hbm, buf_ref, sem_ref, tm)
    hm = jnp.concatenate([rows_ref[j] for j in range(SLABS)], axis=1).astype(BF16)
    cmb = rows_ref[SLABS]
    lane = lax.broadcasted_iota(jnp.int32, cmb.shape, 1)
    first_expert = tile_group_ref[i] * EXPERTS_PER_GROUP
    acc_ref[...] = jnp.zeros_like(acc_ref)
    for k in range(EXPERTS_PER_GROUP):
        @pl.when(need_ref[i * EXPERTS_PER_GROUP + k] != 0)
        def _(k=k):
            ce = jnp.sum(jnp.where(lane == first_expert + k, cmb, 0.0), axis=1, keepdims=True)
            hg = jnp.dot(hm, wg_ref[k], preferred_element_type=F32)
            hu = jnp.dot(hm, wu_ref[k], preferred_element_type=F32)
            acc_ref[...] += jnp.dot((_silu(hg) * hu * ce).astype(BF16), wd_ref[k], preferred_element_type=F32)
    for j in range(SLABS):
        o_ref[:, j, :] = acc_ref[:, j * LANES:(j + 1) * LANES]


def _moe_call(hx, tile_group, need, src, w_gate, w_up, w_down, l):
    n_pad = src.shape[0]
    tm = MOE_TILE
    grp_w = lambda shape: pl.BlockSpec((None,) + shape, lambda i, tg, nd, sr: (l, tg[i], 0, 0))
    return pl.pallas_call(
        _moe_kernel,
        grid_spec=pltpu.PrefetchScalarGridSpec(
            num_scalar_prefetch=3,
            grid=(n_pad // tm,),
            in_specs=[pl.BlockSpec(memory_space=pl.ANY),
                      grp_w((EXPERTS_PER_GROUP, D_MODEL, D_FF)), grp_w((EXPERTS_PER_GROUP, D_MODEL, D_FF)),
                      grp_w((EXPERTS_PER_GROUP, D_FF, D_MODEL))],
            out_specs=pl.BlockSpec((tm, SLABS, LANES), lambda i, tg, nd, sr: (i, 0, 0)),
            scratch_shapes=[pltpu.VMEM((2, SLABS + 1, tm, LANES), F32), pltpu.SemaphoreType.DMA((2,)),
                            pltpu.VMEM((tm, D_MODEL), F32)]),
        out_shape=jax.ShapeDtypeStruct((n_pad, SLABS, LANES), F32),
        compiler_params=_cparams(1),
        name="moe_sorted",
    )(tile_group, need, src, hx, w_gate, w_up, w_down)


def _ln2_kernel(pos_ref, f_hbm, x1_ref, mod_ref, lg_ref, lb_ref, o_ref, buf_ref, sem_ref):
    tm = o_ref.shape[0]
    rows_ref = _gathered_tile(pos_ref, f_hbm, buf_ref, sem_ref, tm)
    f = jnp.concatenate([rows_ref[j] for j in range(SLABS)], axis=1)
    v = ALPHA * x1_ref[...] + mod_ref[GATE2:GATE2 + 1, :] * f
    o_ref[...] = _layer_norm(v, lg_ref[...], lb_ref[...])


def _ln2_call(f_sorted, pos, x1, T, mod, ln_g, ln_b):
    n = x1.shape[0]
    per_seq = mod.shape[0] > 1
    tm = WIDE_TILE if (not per_seq or T % WIDE_TILE == 0) else ROW_TILE
    return pl.pallas_call(
        _ln2_kernel,
        grid_spec=pltpu.PrefetchScalarGridSpec(
            num_scalar_prefetch=1,
            grid=(n // tm,),
            in_specs=[pl.BlockSpec(memory_space=pl.ANY),
                      pl.BlockSpec((tm, D_MODEL), lambda i, ps: (i, 0)),
                      pl.BlockSpec((None, 6, D_MODEL), lambda i, ps: ((i * tm) // T if per_seq else 0, 0, 0)),
                      pl.BlockSpec((1, D_MODEL), lambda i, ps: (0, 0)),
                      pl.BlockSpec((1, D_MODEL), lambda i, ps: (0, 0))],
            out_specs=pl.BlockSpec((tm, D_MODEL), lambda i, ps: (i, 0)),
            scratch_shapes=[pltpu.VMEM((2, SLABS, tm, LANES), F32), pltpu.SemaphoreType.DMA((2,))]),
        out_shape=jax.ShapeDtypeStruct((n, D_MODEL), F32),
        compiler_params=_cparams(1),
        name="moe_combine_ln",
    )(pos, f_sorted, x1, mod, ln_g, ln_b)


def _dispatch_plan(cls, rank, class_count, tm):
    n = cls.shape[0]
    n_pairs = len(EXPERT_PAIRS)
    n_pad = n + N_GROUPS * tm
    i32 = jnp.int32
    count = class_count.reshape(N_GROUPS, n_pairs)
    padded = ((jnp.sum(count, axis=1) + tm - 1) // tm) * tm
    ends = jnp.cumsum(padded)
    class_start = ((ends - padded)[:, None] + jnp.cumsum(count, axis=1) - count).reshape(N_CLASSES)
    pos = class_start[cls] + rank
    src = jnp.zeros((n_pad,), i32).at[pos].set(jnp.arange(n, dtype=i32))
    tile_start = jnp.arange(n_pad // tm, dtype=i32) * tm
    tile_group = jnp.minimum(jnp.sum((tile_start[:, None] >= ends[None, :]).astype(i32), axis=1), N_GROUPS - 1)
    class_end = class_start + count.reshape(N_CLASSES)
    overlap = ((class_start[None, :] < tile_start[:, None] + tm) & (class_end[None, :] > tile_start[:, None])
               & (class_end > class_start)[None, :]).astype(i32)
    pair_has = jnp.array([[int(k in p) for k in range(EXPERTS_PER_GROUP)] for p in EXPERT_PAIRS], i32)
    need = jnp.minimum(overlap @ jnp.tile(pair_has, (N_GROUPS, 1)), 1).reshape(-1)
    return pos, src, tile_group, need


def _grid_pos_embed(n_tokens):
    rows = n_tokens // GRID_W
    r = jnp.repeat(jnp.arange(rows, dtype=F32), GRID_W)
    c = jnp.tile(jnp.arange(GRID_W, dtype=F32), rows)
    q = D_MODEL // 4
    freq = jnp.exp(-jnp.log(10000.0) * jnp.arange(q, dtype=F32) / q)
    ar = r[:, None] * freq
    ac = c[:, None] * freq
    return jnp.concatenate([jnp.sin(ar), jnp.cos(ar), jnp.sin(ac), jnp.cos(ac)], axis=-1)


def _lru_gate_params(wa, ba, wx, bx, lam):
    ncb = D_GROUP // LANES

    def dense(w):
        z = jnp.zeros((ncb, LANES, LANES), F32)
        z = z.at[:, :LRU_BLOCK, :LRU_BLOCK].set(w[0::2])
        return z.at[:, LRU_BLOCK:, LRU_BLOCK:].set(w[1::2])

    wg = jnp.concatenate([dense(wa[0]), dense(wx[0]), dense(wa[1]), dense(wx[1])], axis=-1).astype(BF16)
    per_blk = lambda v: v.reshape(ncb, 1, LANES)
    bg = jnp.concatenate([per_blk(ba[0]), per_blk(bx[0]), per_blk(ba[1]), per_blk(bx[1])], axis=-1)
    lm = jnp.concatenate([per_blk(lam[0]), per_blk(lam[1])], axis=-1)
    return wg, bg, lm


def _mixer_states(B, l, states):
    if states is None:
        return (jnp.zeros((2, B, D_GROUP), F32),
                jnp.zeros((B, 2, N_HEADS, D_HEAD, D_HEAD), F32),
                jnp.zeros((B, 2, N_HEADS, D_HEAD, D_HEAD), F32),
                jnp.zeros((B, 2, N_HEADS, D_HEAD), F32),
                jnp.zeros((B, 2, N_HEADS, D_HEAD), F32))
    h, s, c, n, m = states
    return (jnp.swapaxes(h[:, l], 0, 1), s[:, l], c[:, l], n[:, l],
            jnp.broadcast_to(m[:, l][..., None], (B, 2, N_HEADS, D_HEAD)))


def kernel(x_prompt, x_sample, state_lru_h, state_hgrn_S, state_mlstm_C, state_mlstm_n, state_mlstm_m,
           c, c_ctx, w_ada, b_ada, w_in, b_in, conv_w, conv_b, lru_wa, lru_ba, lru_wx, lru_bx, lru_lam,
           hg_lb, hg_norm_g, ml_norm_g, w_out, ln1_g, ln1_b, ln2_g, ln2_b,
           w_router, b_router, w_gate, w_up, w_down):
    Bp, Tp, _ = x_prompt.shape
    Bs, Ts, _ = x_sample.shape
    assert Bs % SUBLANES == 0 and Bp % SUBLANES == 0 and Bs + 1 <= 16
    assert Tp % ROW_TILE == 0 and Ts % ROW_TILE == 0

    lb_soft = jax.nn.softmax(hg_lb.astype(F32), axis=0)
    hg_lower = jnp.cumsum(lb_soft, axis=0) - lb_soft[0:1]

    c16 = jnp.concatenate([c, c_ctx[None], jnp.zeros((16 - Bs - 1, D_MODEL), F32)], axis=0)
    mod = _ada_call(c16, w_ada, b_ada)

    w_in_p = jnp.pad(w_in.astype(BF16), ((0, 0), (0, 0), (0, D_IN_PAD - D_IN)))
    b_in_p = jnp.pad(b_in, ((0, 0), (0, D_IN_PAD - D_IN))).reshape(DEPTH, 1, D_IN_PAD)
    w_out_b = w_out.astype(BF16)
    w_gate_b, w_up_b, w_down_b = w_gate.astype(BF16), w_up.astype(BF16), w_down.astype(BF16)
    w_router_t = w_router.T
    b_router_c = b_router.reshape(N_EXPERTS, 1)

    xs = x_sample + _grid_pos_embed(Ts).astype(x_sample.dtype)
    streams = [
        dict(x=x_prompt.reshape(Bp * Tp, D_MODEL), T=Tp, B=Bp, states=None),
        dict(x=xs.reshape(Bs * Ts, D_MODEL), T=Ts, B=Bs,
             states=(state_lru_h, state_hgrn_S, state_mlstm_C, state_mlstm_n, state_mlstm_m)),
    ]
    finals = []
    for l in range(DEPTH):
        wg, bg, lm = _lru_gate_params(lru_wa[l], lru_ba[l], lru_wx[l], lru_bx[l], lru_lam[l])
        row = lambda v: v[l].reshape(1, -1)
        for si, st in enumerate(streams):
            T, B = st["T"], st["B"]
            mod_s = (mod[l, :Bs] if si == 1 else mod[l, Bs:Bs + 1]).reshape(-1, 6, D_MODEL)
            h0, s0, c0, n0, m0 = _mixer_states(B, l, st["states"])
            z = _inproj_call(st["x"], T, mod_s, w_in_p, b_in_p, l)
            y_r, h_T = _rglru_call(z.reshape(B, T, D_IN_PAD), h0, conv_w[l], row(conv_b), wg, bg, lm)
            y_h, s_T = _hgrn_call(z, B, s0, hg_lower[l], row(hg_norm_g))
            y_m, c_T, n_T, m_T = _mlstm_call(z, B, c0, n0, m0, row(ml_norm_g))
            x1, hx, cls, rank, cnt = _outproj_call(y_r.reshape(B * T, D_GROUP), y_h, y_m, st["x"], T, mod_s,
                                                   w_out_b, l, row(ln1_g), row(ln1_b), w_router_t, b_router_c)
            pos, src, tile_group, need = _dispatch_plan(cls[0], rank[0], cnt[:N_CLASSES, 0].astype(jnp.int32),
                                                        MOE_TILE)
            f_sorted = _moe_call(hx, tile_group, need, src, w_gate_b, w_up_b, w_down_b, l)
            st["x"] = _ln2_call(f_sorted, pos, x1, T, mod_s, row(ln2_g), row(ln2_b))
            if si == 0:
                finals.append((jnp.swapaxes(h_T, 0, 1), s_T, c_T, n_T, m_T[..., 0]))
    outs = [st["x"].reshape(st["B"], st["T"], D_MODEL) for st in streams]
    stack = lambda i: jnp.stack([f[i] for f in finals], axis=1)
    return (outs[0], outs[1], stack(0), stack(1), stack(2), stack(3), stack(4))
```
